```python
import math
import jax, jax.numpy as jnp
from jax import lax
import numpy as np

D_MODEL = 2048
BATCH = 2
SEQ = 8192
DEPTH = 1

MEM_LEN = 256
N_HEADS_A = 8
N_KV_A = 2
HEAD_DIM_A = 128
ATTN_WIDTH = N_HEADS_A * HEAD_DIM_A
KV_WIDTH = N_KV_A * HEAD_DIM_A
IDX_HEADS = 16
IDX_DIM = 64
TOPK_MAX = 256
BLOCK_Q = 128
REL_BUCKETS = 32
REL_MAX_DIST = 128
SSM_WIDTH = D_MODEL // 4
SSM_GROUP = 16
SSM_GROUPS = SSM_WIDTH // SSM_GROUP
SSM_STATE = 64
X_HEADS = 4
X_HEAD_DIM = 128
X_WIDTH = X_HEADS * X_HEAD_DIM
D_FF = (11 * D_MODEL) // 4
CONV_WIDTH = 3
EPS = 1e-6
IN_SPLITS = (ATTN_WIDTH, KV_WIDTH, KV_WIDTH, IDX_HEADS * IDX_DIM, IDX_DIM, IDX_HEADS, SSM_WIDTH, D_MODEL, D_MODEL)
IN_COLS = sum(IN_SPLITS)

kernel_name = "hybrid_dsa_s5_gated_block"


def rms_norm(x, g):
    xf = x.astype(jnp.float32)
    y = xf * lax.rsqrt(jnp.mean(xf * xf, axis=-1, keepdims=True) + EPS)
    return (y * g.astype(jnp.float32)).astype(x.dtype)


def t5_bucket(n):
    n = jnp.maximum(n, 0)
    max_exact = REL_BUCKETS // 2
    nf = jnp.maximum(n, 1).astype(jnp.float32)
    large = max_exact + (jnp.log(nf / max_exact) / math.log(REL_MAX_DIST / max_exact)
                         * (REL_BUCKETS - max_exact)).astype(jnp.int32)
    large = jnp.minimum(large, REL_BUCKETS - 1)
    return jnp.where(n < max_exact, n, large)


def sparse_attention(q, k, v, qi, ki, wi, rel_bias):
    B, S = q.shape[0], q.shape[1]
    nb = S // BLOCK_Q
    k_sel = min(TOPK_MAX, S // 4)
    grp = N_HEADS_A // N_KV_A
    scale = HEAD_DIM_A ** -0.5
    neg = jnp.finfo(jnp.float32).min
    key_pos = jnp.arange(S, dtype=jnp.int32)

    def to_blocks(a):
        return jnp.moveaxis(a.reshape((B, nb, BLOCK_Q) + a.shape[2:]), 1, 0)

    starts = jnp.arange(nb, dtype=jnp.int32) * BLOCK_Q

    def one_block(args):
        qb, qib, wib, start = args
        q_pos = start + jnp.arange(BLOCK_Q, dtype=jnp.int32)
        causal = key_pos[None, :] <= q_pos[:, None]
        dots = jnp.einsum('bqhd,bsd->bqhs', qib, ki)
        idx_score = jnp.einsum('bqh,bqhs->bqs', wib, jax.nn.relu(dots)).astype(jnp.float32)
        idx_score = jnp.where(causal[None], idx_score, -jnp.inf)
        _, sel = lax.top_k(idx_score, k_sel)
        k_g = jax.vmap(lambda kk, ii: kk[ii])(k, sel)
        v_g = jax.vmap(lambda vv, ii: vv[ii])(v, sel)
        qg = qb.reshape(B, BLOCK_Q, N_KV_A, grp, HEAD_DIM_A)
        logits = jnp.einsum('bqhgd,bqnhd->bqhgn', qg, k_g).astype(jnp.float32) * scale
        dist = q_pos[None, :, None] - sel
        bias = rel_bias[t5_bucket(dist)].astype(jnp.float32)
        bias = bias.reshape(B, BLOCK_Q, k_sel, N_KV_A, grp).transpose(0, 1, 3, 4, 2)
        valid = (dist >= 0)[:, :, None, None, :]
        logits = jnp.where(valid, logits + bias, neg)
        p = jax.nn.softmax(logits, axis=-1).astype(v.dtype)
        out = jnp.einsum('bqhgn,bqnhd->bqhgd', p, v_g)
        return out.reshape(B, BLOCK_Q, ATTN_WIDTH)

    outs = lax.map(one_block, (to_blocks(q), to_blocks(qi), to_blocks(wi), starts))
    return jnp.moveaxis(outs, 0, 1).reshape(B, S, ATTN_WIDTH)


def s5_branch(u, lam_re, lam_im, log_dt, b_re, b_im, c_re, c_im, d_skip, w_glu, b_glu):
    B, S, _ = u.shape
    ug = u.reshape(B, S, SSM_GROUPS, SSM_GROUP).astype(jnp.float32)
    dt = jnp.exp(log_dt.astype(jnp.float32))[:, None]
    lr = lam_re.astype(jnp.float32)
    li = lam_im.astype(jnp.float32)
    mag = jnp.exp(lr * dt)
    ar = mag * jnp.cos(li * dt)
    ai = mag * jnp.sin(li * dt)
    den = lr * lr + li * li
    nr = ar - 1.0
    coef_r = (nr * lr + ai * li) / den
    coef_i = (ai * lr - nr * li) / den
    br = b_re.astype(jnp.float32)
    bi = b_im.astype(jnp.float32)
    bbar_re = coef_r[..., None] * br - coef_i[..., None] * bi
    bbar_im = coef_r[..., None] * bi + coef_i[..., None] * br
    bu_re = jnp.einsum('bsgc,gpc->bsgp', ug, bbar_re)
    bu_im = jnp.einsum('bsgc,gpc->bsgp', ug, bbar_im)
    a_re = jnp.broadcast_to(ar, bu_re.shape)
    a_im = jnp.broadcast_to(ai, bu_im.shape)

    def combine(left, right):
        a1r, a1i, b1r, b1i = left
        a2r, a2i, b2r, b2i = right
        return (a2r * a1r - a2i * a1i,
                a2r * a1i + a2i * a1r,
                a2r * b1r - a2i * b1i + b2r,
                a2r * b1i + a2i * b1r + b2i)

    _, _, xr, xi = lax.associative_scan(combine, (a_re, a_im, bu_re, bu_im), axis=1)
    y = (jnp.einsum('bsgp,gcp->bsgc', xr, c_re.astype(jnp.float32))
         - jnp.einsum('bsgp,gcp->bsgc', xi, c_im.astype(jnp.float32))
         + d_skip.astype(jnp.float32) * ug)
    y = y.reshape(B, S, SSM_WIDTH).astype(u.dtype)
    g = jax.nn.gelu(y)
    return g * jax.nn.sigmoid(g @ w_glu + b_glu)


def cross_attention(h, mem_n, wq, wkv, wo):
    B, S, _ = h.shape
    M = mem_n.shape[1]
    q = (h @ wq).reshape(B, S, X_HEADS, X_HEAD_DIM)
    k, v = jnp.split(mem_n @ wkv, 2, axis=-1)
    k = k.reshape(B, M, X_HEADS, X_HEAD_DIM)
    v = v.reshape(B, M, X_HEADS, X_HEAD_DIM)
    logits = jnp.einsum('bshd,bmhd->bhsm', q, k).astype(jnp.float32) * (X_HEAD_DIM ** -0.5)
    p = jax.nn.softmax(logits, axis=-1).astype(v.dtype)
    out = jnp.einsum('bhsm,bmhd->bshd', p, v).reshape(B, S, X_WIDTH)
    return out @ wo


def conv_ffn(u, w_up, conv_w, conv_b, w_down):
    hdn = u @ w_up
    c = hdn.shape[-1]
    hdn = lax.conv_general_dilated(hdn, conv_w[:, None, :].astype(hdn.dtype), window_strides=(1,),
                                   padding=[(CONV_WIDTH - 1, 0)],
                                   dimension_numbers=('NWC', 'WIO', 'NWC'),
                                   feature_group_count=c) + conv_b
    a, b = jnp.split(hdn, 2, axis=-1)
    return (jax.nn.silu(a) * b) @ w_down


def setup_inputs(seed: int = 0) -> dict:
    key = jax.random.key(seed)
    ks = jax.random.split(key, 32)
    L = DEPTH
    f32 = jnp.float32

    def nrm(k, shape, scale):
        return jax.random.normal(k, shape, f32) * scale

    def gain(k, shape):
        return 1.0 + 0.02 * jax.random.normal(k, shape, f32)

    lam_re = -0.5 + 0.01 * jax.random.normal(ks[5], (L, SSM_GROUPS, SSM_STATE), f32)
    lam_im = (math.pi * jnp.arange(SSM_STATE, dtype=f32))[None, None, :] \
        + 0.01 * jax.random.normal(ks[6], (L, SSM_GROUPS, SSM_STATE), f32)
    uu = jax.random.uniform(ks[7], (L, SSM_GROUPS), f32)
    log_dt = math.log(1e-3) + uu * (math.log(1e-1) - math.log(1e-3))
    return {
        'x': nrm(ks[0], (BATCH, SEQ, D_MODEL), 1.0),
        'mem': nrm(ks[1], (BATCH, MEM_LEN, D_MODEL), 1.0),
        'rel_bias': nrm(ks[2], (REL_BUCKETS, N_HEADS_A), 0.1),
        'norm_mix': gain(ks[3], (L, D_MODEL)),
        'w_in': nrm(ks[4], (L, D_MODEL, IN_COLS), D_MODEL ** -0.5),
        'ssm_lambda_re': lam_re,
        'ssm_lambda_im': lam_im,
        'ssm_log_dt': log_dt,
        'ssm_b_re': nrm(ks[8], (L, SSM_GROUPS, SSM_STATE, SSM_GROUP), (2 * SSM_GROUP) ** -0.5),
        'ssm_b_im': nrm(ks[9], (L, SSM_GROUPS, SSM_STATE, SSM_GROUP), (2 * SSM_GROUP) ** -0.5),
        'ssm_c_re': nrm(ks[10], (L, SSM_GROUPS, SSM_GROUP, SSM_STATE), (2 * SSM_STATE) ** -0.5),
        'ssm_c_im': nrm(ks[11], (L, SSM_GROUPS, SSM_GROUP, SSM_STATE), (2 * SSM_STATE) ** -0.5),
        'ssm_d': nrm(ks[12], (L, SSM_GROUPS, SSM_GROUP), 1.0),
        'ssm_w_glu': nrm(ks[13], (L, SSM_WIDTH, SSM_WIDTH), SSM_WIDTH ** -0.5),
        'ssm_b_glu': nrm(ks[14], (L, SSM_WIDTH), 0.02),
        'w_branch_a': nrm(ks[15], (L, ATTN_WIDTH, D_MODEL), ATTN_WIDTH ** -0.5),
        'w_branch_b': nrm(ks[16], (L, SSM_WIDTH, D_MODEL), SSM_WIDTH ** -0.5),
        'w_out': nrm(ks[17], (L, D_MODEL, D_MODEL), D_MODEL ** -0.5),
        'norm_cross': gain(ks[18], (L, D_MODEL)),
        'norm_mem': gain(ks[19], (L, D_MODEL)),
        'w_cross_q': nrm(ks[20], (L, D_MODEL, X_WIDTH), D_MODEL ** -0.5),
        'w_cross_kv': nrm(ks[21], (L, D_MODEL, 2 * X_WIDTH), D_MODEL ** -0.5),
        'w_cross_o': nrm(ks[22], (L, X_WIDTH, D_MODEL), X_WIDTH ** -0.5),
        'norm_ffn': gain(ks[23], (L, D_MODEL)),
        'w_up': nrm(ks[24], (L, D_MODEL, 2 * D_FF), D_MODEL ** -0.5),
        'ffn_conv_w': nrm(ks[25], (L, CONV_WIDTH, 2 * D_FF), CONV_WIDTH ** -0.5),
        'ffn_conv_b': nrm(ks[26], (L, 2 * D_FF), 0.02),
        'w_down': nrm(ks[27], (L, D_FF, D_MODEL), D_FF ** -0.5),
        'norm_final': gain(ks[28], (D_MODEL,)),
    }


def reference(x, mem, rel_bias, norm_mix, w_in, ssm_lambda_re, ssm_lambda_im, ssm_log_dt,
              ssm_b_re, ssm_b_im, ssm_c_re, ssm_c_im, ssm_d, ssm_w_glu, ssm_b_glu,
              w_branch_a, w_branch_b, w_out, norm_cross, norm_mem, w_cross_q, w_cross_kv,
              w_cross_o, norm_ffn, w_up, ffn_conv_w, ffn_conv_b, w_down, norm_final):
    B, S, _ = x.shape
    offsets = [int(o) for o in np.cumsum(IN_SPLITS)[:-1]]
    idx_w_scale = (IDX_HEADS * IDX_DIM) ** -0.5
    for l in range(DEPTH):
        u = rms_norm(x, norm_mix[l])
        q, k, v, qi, ki, wi, s_in, g_a, g_b = jnp.split(u @ w_in[l], offsets, axis=-1)
        y_a = sparse_attention(
            q.reshape(B, S, N_HEADS_A, HEAD_DIM_A),
            k.reshape(B, S, N_KV_A, HEAD_DIM_A),
            v.reshape(B, S, N_KV_A, HEAD_DIM_A),
            qi.reshape(B, S, IDX_HEADS, IDX_DIM), ki, wi * idx_w_scale, rel_bias)
        y_b = s5_branch(s_in, ssm_lambda_re[l], ssm_lambda_im[l], ssm_log_dt[l], ssm_b_re[l],
                        ssm_b_im[l], ssm_c_re[l], ssm_c_im[l], ssm_d[l], ssm_w_glu[l], ssm_b_glu[l])
        z = jax.nn.sigmoid(g_a) * (y_a @ w_branch_a[l]) + jax.nn.sigmoid(g_b) * (y_b @ w_branch_b[l])
        x = x + z @ w_out[l]
        x = x + cross_attention(rms_norm(x, norm_cross[l]), rms_norm(mem, norm_mem[l]),
                                w_cross_q[l], w_cross_kv[l], w_cross_o[l])
        x = x + conv_ffn(rms_norm(x, norm_ffn[l]), w_up[l], ffn_conv_w[l], ffn_conv_b[l], w_down[l])
    return rms_norm(x, norm_final)
```

```python
import functools
import math

import jax
import jax.numpy as jnp
import numpy as np
from jax import lax
from jax.experimental import pallas as pl
from jax.experimental.pallas import tpu as pltpu

N_HEADS_A = 8
N_KV_A = 2
HEAD_DIM_A = 128
ATTN_WIDTH = N_HEADS_A * HEAD_DIM_A
KV_WIDTH = N_KV_A * HEAD_DIM_A
IDX_HEADS = 16
IDX_DIM = 64
TOPK_MAX = 256
BLOCK_Q = 128
REL_BUCKETS = 32
REL_MAX_DIST = 128
SSM_GROUP = 16
SSM_GROUPS = 32
SSM_STATE = 64
SSM_WIDTH = SSM_GROUP * SSM_GROUPS
SSM_STATES = SSM_GROUPS * SSM_STATE
X_HEADS = 4
X_HEAD_DIM = 128
CONV_WIDTH = 3
EPS = 1e-6

V7X_LANES = 128
V7X_SUBLANES = 8
V7X_VMEM_BYTES = 64 * 1024 * 1024
VMEM_LIMIT_BYTES = V7X_VMEM_BYTES - 8 * 1024 * 1024

KEY_CHUNK = 2 * BLOCK_Q
N_BIAS_TILES = 4
BISECT_ITERS = 32
ROW_TILE = 512
COL_TILE = 512
SCAN_CHUNK = 128

F32 = jnp.float32
BF16 = jnp.bfloat16
_NT = (((1,), (1,)), ((), ()))


def _params(n_axes):
    return pltpu.CompilerParams(dimension_semantics=("arbitrary",) * n_axes,
                                vmem_limit_bytes=VMEM_LIMIT_BYTES)


def _rms(x, gain, eps):
    return x * lax.rsqrt(jnp.mean(x * x, axis=-1, keepdims=True) + eps) * gain


def _norm_matmul_kernel(x_ref, g_ref, w_ref, o_ref, xn_ref):
    @pl.when(pl.program_id(1) == 0)
    def _():
        xn_ref[...] = _rms(x_ref[...], g_ref[...], EPS).astype(BF16)

    o_ref[...] = jnp.dot(xn_ref[...], w_ref[...], preferred_element_type=F32).astype(o_ref.dtype)


def _norm_matmul(x, gain, w, out_dtype, tn, name):
    m, k = x.shape
    n = w.shape[1]
    tm = min(ROW_TILE, m)
    return pl.pallas_call(
        _norm_matmul_kernel,
        grid=(m // tm, n // tn),
        in_specs=[pl.BlockSpec((tm, k), lambda i, j: (i, 0)),
                  pl.BlockSpec((1, k), lambda i, j: (0, 0)),
                  pl.BlockSpec((k, tn), lambda i, j: (0, j))],
        out_specs=pl.BlockSpec((tm, tn), lambda i, j: (i, j)),
        out_shape=jax.ShapeDtypeStruct((m, n), out_dtype),
        scratch_shapes=[pltpu.VMEM((tm, k), BF16)],
        compiler_params=_params(2),
        name=name,
    )(x, gain.reshape(1, k), w)


def _matmul_residual_kernel(a_ref, w_ref, r_ref, o_ref):
    o_ref[...] = r_ref[...] + jnp.dot(a_ref[...], w_ref[...], preferred_element_type=F32)


def _matmul_residual(a, w, res, name):
    m, k = a.shape
    n = w.shape[1]
    tm, tn = ROW_TILE, COL_TILE
    return pl.pallas_call(
        _matmul_residual_kernel,
        grid=(m // tm, n // tn),
        in_specs=[pl.BlockSpec((tm, k), lambda i, j: (i, 0)),
                  pl.BlockSpec((k, tn), lambda i, j: (0, j)),
                  pl.BlockSpec((tm, tn), lambda i, j: (i, j))],
        out_specs=pl.BlockSpec((tm, tn), lambda i, j: (i, j)),
        out_shape=jax.ShapeDtypeStruct((m, n), F32),
        compiler_params=_params(2),
        name=name,
    )(a, w, res)


def _gated_merge_kernel(ya_ref, yb_ref, wa_ref, wb_ref, ga_ref, gb_ref, o_ref):
    za = jnp.dot(ya_ref[...], wa_ref[...], preferred_element_type=F32)
    zb = jnp.dot(yb_ref[...], wb_ref[...], preferred_element_type=F32)
    z = jax.nn.sigmoid(ga_ref[...]) * za + jax.nn.sigmoid(gb_ref[...]) * zb
    o_ref[...] = z.astype(o_ref.dtype)


def _gated_merge(ya, yb, wa, wb, gates):
    m = ya.shape[0]
    n = wa.shape[1]
    tm, tn = ROW_TILE, COL_TILE
    nj = n // tn
    return pl.pallas_call(
        _gated_merge_kernel,
        grid=(m // tm, nj),
        in_specs=[pl.BlockSpec((tm, ya.shape[1]), lambda i, j: (i, 0)),
                  pl.BlockSpec((tm, yb.shape[1]), lambda i, j: (i, 0)),
                  pl.BlockSpec((wa.shape[0], tn), lambda i, j: (0, j)),
                  pl.BlockSpec((wb.shape[0], tn), lambda i, j: (0, j)),
                  pl.BlockSpec((tm, tn), lambda i, j: (i, j)),
                  pl.BlockSpec((tm, tn), lambda i, j: (i, j + nj))],
        out_specs=pl.BlockSpec((tm, tn), lambda i, j: (i, j)),
        out_shape=jax.ShapeDtypeStruct((m, n), BF16),
        compiler_params=_params(2),
        name="gated_merge",
    )(ya, yb, wa, wb, gates, gates)


def _t5_bucket(n):
    max_exact = REL_BUCKETS // 2
    n = jnp.maximum(n, 0)
    nf = jnp.maximum(n, 1).astype(F32)
    large = max_exact + (jnp.log(nf / max_exact) / math.log(REL_MAX_DIST / max_exact)
                         * (REL_BUCKETS - max_exact)).astype(jnp.int32)
    large = jnp.minimum(large, REL_BUCKETS - 1)
    return jnp.where(n < max_exact, n, large)


def _bias_tiles_kernel(rb_ref, o_ref):
    shape = (BLOCK_Q, KEY_CHUNK)
    r = lax.broadcasted_iota(jnp.int32, shape, 0)
    c = lax.broadcasted_iota(jnp.int32, shape, 1)
    for o in range(N_BIAS_TILES):
        bucket = _t5_bucket(o * BLOCK_Q + r - c)
        for h in range(N_HEADS_A):
            def pick(b, t, bucket=bucket, h=h):
                return jnp.where(bucket == b, rb_ref[b, h], t)
            o_ref[o, h] = lax.fori_loop(0, REL_BUCKETS, pick, jnp.zeros(shape, F32))


def _bias_tiles(rel_bias):
    return pl.pallas_call(
        _bias_tiles_kernel,
        in_specs=[pl.BlockSpec(memory_space=pltpu.SMEM)],
        out_specs=pl.BlockSpec(memory_space=pltpu.VMEM),
        out_shape=jax.ShapeDtypeStruct((N_BIAS_TILES, N_HEADS_A, BLOCK_Q, KEY_CHUNK), F32),
        name="rel_bias_tiles",
    )(rel_bias)


def _sparse_attn_kernel(rb_ref, bias_ref, q_ref, qi_ref, kwq_ref, kwk_ref, k_ref, v_ref, o_ref,
                        qih_ref, wb_ref, sc_ref, m_ref, l_ref, acc_ref):
    i = pl.program_id(1)
    t0 = i * BLOCK_Q
    n_chunks = (i + 2) // 2
    pair_shape = (BLOCK_Q, V7X_LANES)
    tile_shape = (BLOCK_Q, KEY_CHUNK)
    n_pairs = IDX_HEADS // 2
    rows = n_pairs * BLOCK_Q

    lane = lax.broadcasted_iota(jnp.int32, pair_shape, 1)
    zero = jnp.zeros(pair_shape, BF16)
    for p in range(n_pairs):
        tile = qi_ref[:, p * V7X_LANES:(p + 1) * V7X_LANES]
        qih_ref[0, p] = jnp.where(lane < IDX_DIM, tile, zero)
        qih_ref[1, p] = jnp.where(lane >= IDX_DIM, tile, zero)
    w_idx = kwq_ref[:, IDX_DIM:IDX_DIM + IDX_HEADS].astype(F32) * ((IDX_HEADS * IDX_DIM) ** -0.5)
    for p in range(n_pairs):
        for par in range(2):
            h = 2 * p + par
            wb_ref[par, p] = jnp.broadcast_to(w_idx[:, h:h + 1], tile_shape)

    q_pos = t0 + lax.broadcasted_iota(jnp.int32, tile_shape, 0)
    col = lax.broadcasted_iota(jnp.int32, tile_shape, 1)

    def score_chunk(kc, carry):
        rmax, rmin = carry
        ks = pl.multiple_of(kc * KEY_CHUNK, KEY_CHUNK)
        k_even = kwk_ref[pl.ds(ks, KEY_CHUNK), 0:V7X_LANES]
        k_odd = kwk_ref[pl.ds(ks, KEY_CHUNK), V7X_LANES:2 * V7X_LANES]
        d_even = lax.dot_general(qih_ref[0].reshape(rows, V7X_LANES), k_even, _NT,
                                 preferred_element_type=F32)
        d_odd = lax.dot_general(qih_ref[1].reshape(rows, V7X_LANES), k_odd, _NT,
                                preferred_element_type=F32)
        s = (jnp.sum(jnp.maximum(d_even, 0.0).reshape(n_pairs, BLOCK_Q, KEY_CHUNK) * wb_ref[0], axis=0)
             + jnp.sum(jnp.maximum(d_odd, 0.0).reshape(n_pairs, BLOCK_Q, KEY_CHUNK) * wb_ref[1], axis=0))
        causal = (ks + col) <= q_pos
        sc_ref[kc] = jnp.where(causal, s, -jnp.inf)
        rmax = jnp.maximum(rmax, jnp.max(jnp.where(causal, s, -jnp.inf), axis=-1, keepdims=True))
        rmin = jnp.minimum(rmin, jnp.min(jnp.where(causal, s, jnp.inf), axis=-1, keepdims=True))
        return rmax, rmin

    col1 = (BLOCK_Q, 1)
    rmax, rmin = lax.fori_loop(0, n_chunks, score_chunk,
                               (jnp.full(col1, -jnp.inf, F32), jnp.full(col1, jnp.inf, F32)))

    def bisect(_, carry):
        lo, hi = carry
        mid = lo + (hi - lo) * 0.5

        def count_chunk(kc, c):
            return c + jnp.where(sc_ref[kc] >= mid, 1.0, 0.0)

        c = lax.fori_loop(0, n_chunks, count_chunk, jnp.zeros(tile_shape, F32))
        enough = jnp.sum(c, axis=-1, keepdims=True) >= float(TOPK_MAX)
        return jnp.where(enough, mid, lo), jnp.where(enough, hi, mid)

    lo, _ = lax.fori_loop(0, BISECT_ITERS, bisect, (rmin, rmax))
    n_causal = t0 + lax.broadcasted_iota(jnp.int32, col1, 0) + 1
    thr = jnp.where(n_causal > TOPK_MAX, lo, jnp.finfo(F32).min)

    m_ref[...] = jnp.full(m_ref.shape, -1e30, F32)
    l_ref[...] = jnp.zeros(l_ref.shape, F32)
    acc_ref[...] = jnp.zeros(acc_ref.shape, F32)
    scale = HEAD_DIM_A ** -0.5
    grp = N_HEADS_A // N_KV_A

    def attend(near):
        def body(kc, carry):
            ks = pl.multiple_of(kc * KEY_CHUNK, KEY_CHUNK)
            keep = sc_ref[kc] >= thr
            for kv in range(N_KV_A):
                k_c = k_ref[pl.ds(ks, KEY_CHUNK), kv * HEAD_DIM_A:(kv + 1) * HEAD_DIM_A]
                v_c = v_ref[pl.ds(ks, KEY_CHUNK), kv * HEAD_DIM_A:(kv + 1) * HEAD_DIM_A]
                for g in range(grp):
                    h = kv * grp + g
                    q_h = q_ref[:, h * HEAD_DIM_A:(h + 1) * HEAD_DIM_A]
                    lg = lax.dot_general(q_h, k_c, _NT, preferred_element_type=F32) * scale
                    if near:
                        lg = lg + bias_ref[i - 2 * kc, h]
                    else:
                        lg = lg + rb_ref[REL_BUCKETS - 1, h]
                    lg = jnp.where(keep, lg, -jnp.inf)
                    m_old = m_ref[h]
                    m_new = jnp.maximum(m_old, jnp.max(lg, axis=-1, keepdims=True))
                    alpha = jnp.exp(m_old - m_new)
                    p = jnp.exp(lg - m_new)
                    l_ref[h] = alpha * l_ref[h] + jnp.sum(p, axis=-1, keepdims=True)
                    acc_ref[h] = alpha * acc_ref[h] + jnp.dot(p.astype(BF16), v_c,
                                                              preferred_element_type=F32)
                    m_ref[h] = m_new
            return carry
        return body

    n_far = jnp.maximum(n_chunks - 2, 0)
    lax.fori_loop(0, n_far, attend(False), 0)
    lax.fori_loop(n_far, n_chunks, attend(True), 0)

    for h in range(N_HEADS_A):
        o_ref[:, h * HEAD_DIM_A:(h + 1) * HEAD_DIM_A] = (acc_ref[h] / l_ref[h]).astype(o_ref.dtype)


def _sparse_attention(rel_bias, bias_tiles, proj, kw, batch, seq):
    nb = seq // BLOCK_Q
    kblk = 2 * ATTN_WIDTH // KV_WIDTH
    return pl.pallas_call(
        _sparse_attn_kernel,
        grid=(batch, nb),
        in_specs=[pl.BlockSpec(memory_space=pltpu.SMEM),
                  pl.BlockSpec((N_BIAS_TILES, N_HEADS_A, BLOCK_Q, KEY_CHUNK), lambda b, i: (0, 0, 0, 0)),
                  pl.BlockSpec((BLOCK_Q, ATTN_WIDTH), lambda b, i: (b * nb + i, 0)),
                  pl.BlockSpec((BLOCK_Q, IDX_HEADS * IDX_DIM), lambda b, i: (b * nb + i, 1)),
                  pl.BlockSpec((BLOCK_Q, 2 * V7X_LANES), lambda b, i: (b * nb + i, 0)),
                  pl.BlockSpec((seq, 2 * V7X_LANES), lambda b, i: (b, 0)),
                  pl.BlockSpec((seq, KV_WIDTH), lambda b, i: (b, kblk)),
                  pl.BlockSpec((seq, KV_WIDTH), lambda b, i: (b, kblk + 1))],
        out_specs=pl.BlockSpec((BLOCK_Q, ATTN_WIDTH), lambda b, i: (b * nb + i, 0)),
        out_shape=jax.ShapeDtypeStruct((batch * seq, ATTN_WIDTH), BF16),
        scratch_shapes=[pltpu.VMEM((2, IDX_HEADS // 2, BLOCK_Q, V7X_LANES), BF16),
                        pltpu.VMEM((2, IDX_HEADS // 2, BLOCK_Q, KEY_CHUNK), F32),
                        pltpu.VMEM((seq // KEY_CHUNK, BLOCK_Q, KEY_CHUNK), F32),
                        pltpu.VMEM((N_HEADS_A, BLOCK_Q, 1), F32),
                        pltpu.VMEM((N_HEADS_A, BLOCK_Q, 1), F32),
                        pltpu.VMEM((N_HEADS_A, BLOCK_Q, HEAD_DIM_A), F32)],
        compiler_params=_params(2),
        name="sparse_attention",
    )(rel_bias, bias_tiles, proj, proj, kw, kw, proj, proj)


def _s5_input_kernel(u_ref, lr_ref, li_ref, ldt_ref, bre_ref, bim_ref,
                     bur_ref, bui_ref, ar_ref, ai_ref, bbr_ref, bbi_ref):
    @pl.when(pl.program_id(0) == 0)
    def _():
        dt = jnp.exp(ldt_ref[...])
        lr = lr_ref[...]
        li = li_ref[...]
        mag = jnp.exp(lr * dt)
        ar = mag * jnp.cos(li * dt)
        ai = mag * jnp.sin(li * dt)
        den = lr * lr + li * li
        nr = ar - 1.0
        cr = (nr * lr + ai * li) / den
        ci = (ai * lr - nr * li) / den
        bre = bre_ref[...]
        bim = bim_ref[...]
        bbr_ref[...] = (cr * bre - ci * bim).astype(BF16)
        bbi_ref[...] = (cr * bim + ci * bre).astype(BF16)
        ar_ref[...] = ar
        ai_ref[...] = ai

    u = u_ref[...].astype(BF16)
    bur_ref[...] = jnp.dot(u, bbr_ref[...], preferred_element_type=F32)
    bui_ref[...] = jnp.dot(u, bbi_ref[...], preferred_element_type=F32)


def _s5_input(us, lam_re, lam_im, log_dt, bre_bd, bim_bd):
    m = us.shape[0]
    tm = ROW_TILE
    row = pl.BlockSpec((1, SSM_STATES), lambda i: (0, 0))
    mat = pl.BlockSpec((SSM_WIDTH, SSM_STATES), lambda i: (0, 0))
    out = pl.BlockSpec((tm, SSM_STATES), lambda i: (i, 0))
    return pl.pallas_call(
        _s5_input_kernel,
        grid=(m // tm,),
        in_specs=[pl.BlockSpec((tm, SSM_WIDTH), lambda i: (i, 0)), row, row, row, mat, mat],
        out_specs=[out, out, row, row],
        out_shape=[jax.ShapeDtypeStruct((m, SSM_STATES), F32)] * 2
        + [jax.ShapeDtypeStruct((1, SSM_STATES), F32)] * 2,
        scratch_shapes=[pltpu.VMEM((SSM_WIDTH, SSM_STATES), BF16)] * 2,
        compiler_params=_params(1),
        name="s5_input",
    )(us, lam_re, lam_im, log_dt, bre_bd, bim_bd)


def _s5_scan_kernel(ar_ref, ai_ref, br_ref, bi_ref, xr_ref, xi_ref, sr_ref, si_ref):
    @pl.when(pl.program_id(0) == 0)
    def _():
        sr_ref[...] = jnp.zeros(sr_ref.shape, F32)
        si_ref[...] = jnp.zeros(si_ref.shape, F32)

    ar = ar_ref[...]
    ai = ai_ref[...]
    steps = br_ref.shape[1]

    def step(t, carry):
        xr, xi = carry
        nr = ar * xr - ai * xi + br_ref[:, t]
        ni = ar * xi + ai * xr + bi_ref[:, t]
        xr_ref[:, t] = nr
        xi_ref[:, t] = ni
        return nr, ni

    xr, xi = lax.fori_loop(0, steps, step, (sr_ref[...], si_ref[...]), unroll=8)
    sr_ref[...] = xr
    si_ref[...] = xi


def _s5_scan(a_re, a_im, bu_re, bu_im):
    batch, seq, rows, lanes = bu_re.shape
    blk = pl.BlockSpec((batch, SCAN_CHUNK, rows, lanes), lambda c: (0, c, 0, 0))
    par = pl.BlockSpec((rows, lanes), lambda c: (0, 0))
    return pl.pallas_call(
        _s5_scan_kernel,
        grid=(seq // SCAN_CHUNK,),
        in_specs=[par, par, blk, blk],
        out_specs=[blk, blk],
        out_shape=[jax.ShapeDtypeStruct(bu_re.shape, F32)] * 2,
        scratch_shapes=[pltpu.VMEM((batch, rows, lanes), F32)] * 2,
        compiler_params=_params(1),
        name="s5_scan",
    )(a_re, a_im, bu_re, bu_im)


def _s5_output_kernel(xr_ref, xi_ref, u_ref, cr_ref, ci_ref, d_ref, wg_ref, bg_ref, o_ref):
    y = (jnp.dot(xr_ref[...].astype(BF16), cr_ref[...], preferred_element_type=F32)
         - jnp.dot(xi_ref[...].astype(BF16), ci_ref[...], preferred_element_type=F32)
         + d_ref[...] * u_ref[...])
    g = jax.nn.gelu(y)
    gate = jax.nn.sigmoid(jnp.dot(g.astype(BF16), wg_ref[...], preferred_element_type=F32) + bg_ref[...])
    o_ref[...] = (g * gate).astype(o_ref.dtype)


def _s5_output(xr, xi, us, cre_bd, cim_bd, d_row, w_glu, b_glu):
    m = us.shape[0]
    tm = ROW_TILE
    state = pl.BlockSpec((tm, SSM_STATES), lambda i: (i, 0))
    chan = pl.BlockSpec((tm, SSM_WIDTH), lambda i: (i, 0))
    cmat = pl.BlockSpec((SSM_STATES, SSM_WIDTH), lambda i: (0, 0))
    row = pl.BlockSpec((1, SSM_WIDTH), lambda i: (0, 0))
    return pl.pallas_call(
        _s5_output_kernel,
        grid=(m // tm,),
        in_specs=[state, state, chan, cmat, cmat, row,
                  pl.BlockSpec((SSM_WIDTH, SSM_WIDTH), lambda i: (0, 0)), row],
        out_specs=chan,
        out_shape=jax.ShapeDtypeStruct((m, SSM_WIDTH), BF16),
        compiler_params=_params(1),
        name="s5_output",
    )(xr, xi, us, cre_bd, cim_bd, d_row, w_glu, b_glu)


def _block_diag(blocks):
    g, r, c = blocks.shape
    eye = jnp.eye(g, dtype=blocks.dtype)
    return (eye[:, None, :, None] * blocks[:, :, None, :]).reshape(g * r, g * c)


def _cross_attn_kernel(q_ref, k_ref, v_ref, o_ref):
    scale = X_HEAD_DIM ** -0.5
    for h in range(X_HEADS):
        cols = slice(h * X_HEAD_DIM, (h + 1) * X_HEAD_DIM)
        lg = lax.dot_general(q_ref[:, cols], k_ref[:, cols], _NT, preferred_element_type=F32) * scale
        p = jnp.exp(lg - jnp.max(lg, axis=-1, keepdims=True))
        o = jnp.dot(p.astype(BF16), v_ref[:, cols], preferred_element_type=F32)
        o_ref[:, cols] = (o / jnp.sum(p, axis=-1, keepdims=True)).astype(o_ref.dtype)


def _cross_attn(q, kv, batch, seq, mem_len):
    width = X_HEADS * X_HEAD_DIM
    tq = ROW_TILE
    nt = seq // tq
    return pl.pallas_call(
        _cross_attn_kernel,
        grid=(batch, nt),
        in_specs=[pl.BlockSpec((tq, width), lambda b, i: (b * nt + i, 0)),
                  pl.BlockSpec((mem_len, width), lambda b, i: (b, 0)),
                  pl.BlockSpec((mem_len, width), lambda b, i: (b, 1))],
        out_specs=pl.BlockSpec((tq, width), lambda b, i: (b * nt + i, 0)),
        out_shape=jax.ShapeDtypeStruct((batch * seq, width), BF16),
        compiler_params=_params(2),
        name="cross_attention",
    )(q, kv, kv)


def _conv_ffn_kernel(x_ref, g_ref, wa_ref, wb_ref, cwa_ref, cwb_ref, cba_ref, cbb_ref, wd_ref, gf_ref,
                     o_ref, xn_ref, ha_ref, hb_ref, ta_ref, tb_ref, acc_ref, *, tiles_per_seq):
    i = pl.program_id(0)
    j = pl.program_id(1)
    tm = x_ref.shape[0]
    halo = V7X_SUBLANES

    @pl.when(j == 0)
    def _():
        xn_ref[...] = _rms(x_ref[...], g_ref[...], EPS).astype(BF16)
        acc_ref[...] = jnp.zeros(acc_ref.shape, F32)

    @pl.when(i % tiles_per_seq == 0)
    def _():
        ha_ref[0:halo] = jnp.zeros((halo, ha_ref.shape[1]), F32)
        hb_ref[0:halo] = jnp.zeros((halo, hb_ref.shape[1]), F32)

    @pl.when(i % tiles_per_seq != 0)
    def _():
        ha_ref[0:halo] = ta_ref[j]
        hb_ref[0:halo] = tb_ref[j]

    xn = xn_ref[...]
    ha_ref[halo:halo + tm] = jnp.dot(xn, wa_ref[...], preferred_element_type=F32)
    hb_ref[halo:halo + tm] = jnp.dot(xn, wb_ref[...], preferred_element_type=F32)
    ta_ref[j] = ha_ref[tm:tm + halo]
    tb_ref[j] = hb_ref[tm:tm + halo]

    def conv(h_ref, cw_ref, cb_ref):
        out = cb_ref[...]
        for tap in range(CONV_WIDTH):
            start = halo - (CONV_WIDTH - 1) + tap
            out = out + h_ref[start:start + tm] * cw_ref[tap:tap + 1]
        return out

    a = conv(ha_ref, cwa_ref, cba_ref)
    b = conv(hb_ref, cwb_ref, cbb_ref)
    act = (a * jax.nn.sigmoid(a) * b).astype(BF16)
    acc_ref[...] += jnp.dot(act, wd_ref[...], preferred_element_type=F32)

    @pl.when(j == pl.num_programs(1) - 1)
    def _():
        o_ref[...] = _rms(x_ref[...] + acc_ref[...], gf_ref[...], EPS)


def _conv_ffn(x, gain, w_up, conv_w, conv_b, w_down, gain_final, seq):
    m, d = x.shape
    d_ff = w_down.shape[0]
    tm, tf = ROW_TILE, COL_TILE
    nj = d_ff // tf
    halo = V7X_SUBLANES
    return pl.pallas_call(
        functools.partial(_conv_ffn_kernel, tiles_per_seq=seq // tm),
        grid=(m // tm, nj),
        in_specs=[pl.BlockSpec((tm, d), lambda i, j: (i, 0)),
                  pl.BlockSpec((1, d), lambda i, j: (0, 0)),
                  pl.BlockSpec((d, tf), lambda i, j: (0, j)),
                  pl.BlockSpec((d, tf), lambda i, j: (0, j + nj)),
                  pl.BlockSpec((CONV_WIDTH, tf), lambda i, j: (0, j)),
                  pl.BlockSpec((CONV_WIDTH, tf), lambda i, j: (0, j + nj)),
                  pl.BlockSpec((1, tf), lambda i, j: (0, j)),
                  pl.BlockSpec((1, tf), lambda i, j: (0, j + nj)),
                  pl.BlockSpec((tf, d), lambda i, j: (j, 0)),
                  pl.BlockSpec((1, d), lambda i, j: (0, 0))],
        out_specs=pl.BlockSpec((tm, d), lambda i, j: (i, 0)),
        out_shape=jax.ShapeDtypeStruct((m, d), F32),
        scratch_shapes=[pltpu.VMEM((tm, d), BF16),
                        pltpu.VMEM((tm + halo, tf), F32),
                        pltpu.VMEM((tm + halo, tf), F32),
                        pltpu.VMEM((nj, halo, tf), F32),
                        pltpu.VMEM((nj, halo, tf), F32),
                        pltpu.VMEM((tm, d), F32)],
        compiler_params=_params(2),
        name="conv_ffn",
    )(x, gain.reshape(1, d), w_up, w_up, conv_w, conv_w, conv_b.reshape(1, -1), conv_b.reshape(1, -1),
      w_down, gain_final.reshape(1, d))


def kernel(x, mem, rel_bias, norm_mix, w_in, ssm_lambda_re, ssm_lambda_im, ssm_log_dt, ssm_b_re, ssm_b_im, ssm_c_re, ssm_c_im, ssm_d, ssm_w_glu, ssm_b_glu, w_branch_a, w_branch_b, w_out, norm_cross, norm_mem, w_cross_q, w_cross_kv, w_cross_o, norm_ffn, w_up, ffn_conv_w, ffn_conv_b, w_down, norm_final):
    batch, seq, d_model = x.shape
    mem_len = mem.shape[1]
    depth = w_in.shape[0]
    assert depth == 1, "the final rmsnorm is fused into the last layer's ConvFFN kernel"
    assert seq % ROW_TILE == 0 and seq % KEY_CHUNK == 0 and seq >= 4 * TOPK_MAX
    m = batch * seq
    xf = x.reshape(m, d_model)
    memf = mem.reshape(batch * mem_len, d_model)
    bias_tiles = _bias_tiles(rel_bias)

    splits = (ATTN_WIDTH, KV_WIDTH, KV_WIDTH, IDX_HEADS * IDX_DIM, IDX_DIM, IDX_HEADS, SSM_WIDTH, d_model, d_model)
    offs = [int(o) for o in np.cumsum((0,) + splits)]
    for l in range(depth):
        w = w_in[l]
        wq, wk, wv, wqi, wki, wwi, wss, wga, wgb = [w[:, offs[n]:offs[n + 1]] for n in range(len(splits))]
        pad = jnp.zeros((d_model, 2 * V7X_LANES - 2 * IDX_DIM - IDX_HEADS), w.dtype)
        w_proj = jnp.concatenate([wq, wqi, wk, wv], axis=1).astype(BF16)
        w_kw = jnp.concatenate([wki, wwi, pad, wki], axis=1).astype(BF16)
        w_gate = jnp.concatenate([wga, wgb], axis=1).astype(BF16)

        proj = _norm_matmul(xf, norm_mix[l], w_proj, BF16, COL_TILE, "proj_attn")
        kw = _norm_matmul(xf, norm_mix[l], w_kw, BF16, 2 * V7X_LANES, "proj_index")
        us = _norm_matmul(xf, norm_mix[l], wss.astype(BF16), F32, COL_TILE, "proj_ssm")
        gates = _norm_matmul(xf, norm_mix[l], w_gate, F32, COL_TILE, "proj_gates")

        y_a = _sparse_attention(rel_bias, bias_tiles, proj, kw, batch, seq)

        bre_bd = _block_diag(jnp.transpose(ssm_b_re[l], (0, 2, 1)))
        bim_bd = _block_diag(jnp.transpose(ssm_b_im[l], (0, 2, 1)))
        cre_bd = _block_diag(jnp.transpose(ssm_c_re[l], (0, 2, 1))).astype(BF16)
        cim_bd = _block_diag(jnp.transpose(ssm_c_im[l], (0, 2, 1))).astype(BF16)
        log_dt = jnp.broadcast_to(ssm_log_dt[l][:, None], (SSM_GROUPS, SSM_STATE)).reshape(1, SSM_STATES)
        bu_re, bu_im, a_re, a_im = _s5_input(us, ssm_lambda_re[l].reshape(1, SSM_STATES),
                                             ssm_lambda_im[l].reshape(1, SSM_STATES), log_dt, bre_bd, bim_bd)
        state_rows = SSM_STATES // V7X_LANES
        sshape = (batch, seq, state_rows, V7X_LANES)
        xs_re, xs_im = _s5_scan(a_re.reshape(state_rows, V7X_LANES), a_im.reshape(state_rows, V7X_LANES),
                                bu_re.reshape(sshape), bu_im.reshape(sshape))
        y_b = _s5_output(xs_re.reshape(m, SSM_STATES), xs_im.reshape(m, SSM_STATES), us, cre_bd, cim_bd,
                         ssm_d[l].reshape(1, SSM_WIDTH), ssm_w_glu[l].astype(BF16),
                         ssm_b_glu[l].reshape(1, SSM_WIDTH))

        z = _gated_merge(y_a, y_b, w_branch_a[l].astype(BF16), w_branch_b[l].astype(BF16), gates)
        xf = _matmul_residual(z, w_out[l].astype(BF16), xf, "out_proj")

        qx = _norm_matmul(xf, norm_cross[l], w_cross_q[l].astype(BF16), BF16, COL_TILE, "cross_q")
        kvx = _norm_matmul(memf, norm_mem[l], w_cross_kv[l].astype(BF16), BF16, COL_TILE, "cross_kv")
        ox = _cross_attn(qx, kvx, batch, seq, mem_len)
        xf = _matmul_residual(ox, w_cross_o[l].astype(BF16), xf, "cross_out")

        xf = _conv_ffn(xf, norm_ffn[l], w_up[l].astype(BF16), ffn_conv_w[l], ffn_conv_b[l],
                       w_down[l].astype(BF16), norm_final, seq)
    return xf.reshape(batch, seq, d_model)
```

```python
import functools
import math

import jax
import jax.numpy as jnp
import numpy as np
from jax import lax
from jax.experimental import pallas as pl
from jax.experimental.pallas import tpu as pltpu

N_HEADS_A = 8
N_KV_A = 2
HEAD_DIM_A = 128
ATTN_WIDTH = N_HEADS_A * HEAD_DIM_A
KV_WIDTH = N_KV_A * HEAD_DIM_A
IDX_HEADS = 16
IDX_DIM = 64
TOPK_MAX = 256
BLOCK_Q = 128
REL_BUCKETS = 32
REL_MAX_DIST = 128
SSM_GROUP = 16
SSM_GROUPS = 32
SSM_STATE = 64
SSM_WIDTH = SSM_GROUP * SSM_GROUPS
SSM_STATES = SSM_GROUPS * SSM_STATE
X_HEADS = 4
X_HEAD_DIM = 128
CONV_WIDTH = 3
EPS = 1e-6

V7X_LANES = 128
V7X_SUBLANES = 8
V7X_VMEM_BYTES = 64 * 1024 * 1024
VMEM_LIMIT_BYTES = V7X_VMEM_BYTES - 8 * 1024 * 1024

GRP = N_HEADS_A // N_KV_A
GRP_LANES = GRP * BLOCK_Q
KEY_CHUNK = 2 * BLOCK_Q
N_BIAS_TILES = 4
MAX_BISECT_ITERS = 64
BISECT_ROUND = 4
COUNT_ROWS = 64
LOG2_E = math.log2(math.e)
ROW_TILE = 512
COL_TILE = 512
SCAN_CHUNK = 128

F32 = jnp.float32
BF16 = jnp.bfloat16
_NT = (((1,), (1,)), ((), ()))


def _params(n_axes):
    return pltpu.CompilerParams(dimension_semantics=("arbitrary",) * n_axes,
                                vmem_limit_bytes=VMEM_LIMIT_BYTES)


def _rms(x, gain, eps):
    return x * lax.rsqrt(jnp.mean(x * x, axis=-1, keepdims=True) + eps) * gain


def _norm_matmul_kernel(x_ref, g_ref, w_ref, o_ref, xn_ref):
    @pl.when(pl.program_id(1) == 0)
    def _():
        xn_ref[...] = _rms(x_ref[...], g_ref[...], EPS).astype(BF16)

    o_ref[...] = jnp.dot(xn_ref[...], w_ref[...], preferred_element_type=F32).astype(o_ref.dtype)


def _norm_matmul(x, gain, w, out_dtype, tn, name):
    m, k = x.shape
    n = w.shape[1]
    tm = min(ROW_TILE, m)
    return pl.pallas_call(
        _norm_matmul_kernel,
        grid=(m // tm, n // tn),
        in_specs=[pl.BlockSpec((tm, k), lambda i, j: (i, 0)),
                  pl.BlockSpec((1, k), lambda i, j: (0, 0)),
                  pl.BlockSpec((k, tn), lambda i, j: (0, j))],
        out_specs=pl.BlockSpec((tm, tn), lambda i, j: (i, j)),
        out_shape=jax.ShapeDtypeStruct((m, n), out_dtype),
        scratch_shapes=[pltpu.VMEM((tm, k), BF16)],
        compiler_params=_params(2),
        name=name,
    )(x, gain.reshape(1, k), w)


def _proj_attn_kernel(x_ref, g_ref, wt_ref, wn_ref, qt_ref, vt_ref, wit_ref, kn_ref):
    xn = _rms(x_ref[...], g_ref[...], EPS).astype(BF16)
    res = lax.dot_general(wt_ref[...], xn, _NT, preferred_element_type=F32)
    nq = qt_ref.shape[0]
    qt_ref[...] = res[0:nq].astype(BF16)
    for c in range(vt_ref.shape[0]):
        vt_ref[c] = res[nq:nq + KV_WIDTH, c * KEY_CHUNK:(c + 1) * KEY_CHUNK].astype(BF16)
    wit_ref[...] = res[nq + KV_WIDTH:nq + KV_WIDTH + IDX_HEADS]
    kn_ref[...] = jnp.dot(xn, wn_ref[...], preferred_element_type=F32).astype(BF16)


def _proj_attn(x, gain, w_t, w_n):
    m, k = x.shape
    tm = ROW_TILE
    nq = ATTN_WIDTH + IDX_HEADS * IDX_DIM
    return pl.pallas_call(
        _proj_attn_kernel,
        grid=(m // tm,),
        in_specs=[pl.BlockSpec((tm, k), lambda i: (i, 0)),
                  pl.BlockSpec((1, k), lambda i: (0, 0)),
                  pl.BlockSpec(w_t.shape, lambda i: (0, 0)),
                  pl.BlockSpec(w_n.shape, lambda i: (0, 0))],
        out_specs=[pl.BlockSpec((nq, tm), lambda i: (0, i)),
                   pl.BlockSpec((tm // KEY_CHUNK, KV_WIDTH, KEY_CHUNK), lambda i: (i, 0, 0)),
                   pl.BlockSpec((IDX_HEADS, tm), lambda i: (0, i)),
                   pl.BlockSpec((tm, w_n.shape[1]), lambda i: (i, 0))],
        out_shape=[jax.ShapeDtypeStruct((nq, m), BF16),
                   jax.ShapeDtypeStruct((m // KEY_CHUNK, KV_WIDTH, KEY_CHUNK), BF16),
                   jax.ShapeDtypeStruct((IDX_HEADS, m), F32),
                   jax.ShapeDtypeStruct((m, w_n.shape[1]), BF16)],
        compiler_params=_params(1),
        name="proj_attn",
    )(x, gain.reshape(1, k), w_t, w_n)


def _matmul_residual_kernel(a_ref, w_ref, r_ref, o_ref):
    o_ref[...] = r_ref[...] + jnp.dot(a_ref[...], w_ref[...], preferred_element_type=F32)


def _matmul_residual(a, w, res, name):
    m, k = a.shape
    n = w.shape[1]
    tm, tn = ROW_TILE, COL_TILE
    return pl.pallas_call(
        _matmul_residual_kernel,
        grid=(m // tm, n // tn),
        in_specs=[pl.BlockSpec((tm, k), lambda i, j: (i, 0)),
                  pl.BlockSpec((k, tn), lambda i, j: (0, j)),
                  pl.BlockSpec((tm, tn), lambda i, j: (i, j))],
        out_specs=pl.BlockSpec((tm, tn), lambda i, j: (i, j)),
        out_shape=jax.ShapeDtypeStruct((m, n), F32),
        compiler_params=_params(2),
        name=name,
    )(a, w, res)


def _gated_merge_kernel(ya_ref, yb_ref, wa_ref, wb_ref, ga_ref, gb_ref, o_ref):
    za = jnp.dot(ya_ref[...], wa_ref[...], preferred_element_type=F32)
    zb = jnp.dot(yb_ref[...], wb_ref[...], preferred_element_type=F32)
    z = jax.nn.sigmoid(ga_ref[...]) * za + jax.nn.sigmoid(gb_ref[...]) * zb
    o_ref[...] = z.astype(o_ref.dtype)


def _gated_merge(ya, yb, wa, wb, gates):
    m = ya.shape[0]
    n = wa.shape[1]
    tm, tn = ROW_TILE, COL_TILE
    nj = n // tn
    return pl.pallas_call(
        _gated_merge_kernel,
        grid=(m // tm, nj),
        in_specs=[pl.BlockSpec((tm, ya.shape[1]), lambda i, j: (i, 0)),
                  pl.BlockSpec((tm, yb.shape[1]), lambda i, j: (i, 0)),
                  pl.BlockSpec((wa.shape[0], tn), lambda i, j: (0, j)),
                  pl.BlockSpec((wb.shape[0], tn), lambda i, j: (0, j)),
                  pl.BlockSpec((tm, tn), lambda i, j: (i, j)),
                  pl.BlockSpec((tm, tn), lambda i, j: (i, j + nj))],
        out_specs=pl.BlockSpec((tm, tn), lambda i, j: (i, j)),
        out_shape=jax.ShapeDtypeStruct((m, n), BF16),
        compiler_params=_params(2),
        name="gated_merge",
    )(ya, yb, wa, wb, gates, gates)


def _t5_bucket(n):
    max_exact = REL_BUCKETS // 2
    n = jnp.maximum(n, 0)
    nf = jnp.maximum(n, 1).astype(F32)
    large = max_exact + (jnp.log(nf / max_exact) / math.log(REL_MAX_DIST / max_exact)
                         * (REL_BUCKETS - max_exact)).astype(jnp.int32)
    large = jnp.minimum(large, REL_BUCKETS - 1)
    return jnp.where(n < max_exact, n, large)


def _bias_tiles_kernel(rb_ref, o_ref):
    shape = (KEY_CHUNK, BLOCK_Q)
    key = lax.broadcasted_iota(jnp.int32, shape, 0)
    qry = lax.broadcasted_iota(jnp.int32, shape, 1)
    for o in range(N_BIAS_TILES):
        bucket = _t5_bucket(o * BLOCK_Q + qry - key)
        for h in range(N_HEADS_A):
            def pick(b, t, bucket=bucket, h=h):
                return jnp.where(bucket == b, rb_ref[b, h], t)
            tile = lax.fori_loop(0, REL_BUCKETS, pick, jnp.zeros(shape, F32))
            j = h % GRP
            o_ref[o, h // GRP, :, j * BLOCK_Q:(j + 1) * BLOCK_Q] = (tile - rb_ref[REL_BUCKETS - 1, h]) * LOG2_E


def _bias_tiles(rel_bias):
    return pl.pallas_call(
        _bias_tiles_kernel,
        in_specs=[pl.BlockSpec(memory_space=pltpu.SMEM)],
        out_specs=pl.BlockSpec(memory_space=pltpu.VMEM),
        out_shape=jax.ShapeDtypeStruct((N_BIAS_TILES, N_KV_A, KEY_CHUNK, GRP_LANES), F32),
        name="rel_bias_tiles",
    )(rel_bias)


def _sparse_attn_kernel(bias_ref, qt_ref, qit_ref, wit_ref, kw_ref, k_ref, vt_ref, o_ref,
                        qz_ref, qiz_ref, sc_ref, m_ref, l_ref, acc_ref):
    i = pl.program_id(1)
    t0 = i * BLOCK_Q
    n_chunks = (i + 2) // 2
    tile = (KEY_CHUNK, BLOCK_Q)
    row1 = (1, BLOCK_Q)
    k_sel = float(TOPK_MAX)

    @pl.when((pl.program_id(0) == 0) & (i == 0))
    def _():
        qiz_ref[...] = jnp.zeros(qiz_ref.shape, BF16)

    for h in range(IDX_HEADS):
        qiz_ref[0:IDX_DIM, h * BLOCK_Q:(h + 1) * BLOCK_Q] = qit_ref[h * IDX_DIM:(h + 1) * IDX_DIM, :]
    for h in range(N_HEADS_A):
        qz_ref[:, h * BLOCK_Q:(h + 1) * BLOCK_Q] = qt_ref[h * HEAD_DIM_A:(h + 1) * HEAD_DIM_A, :]
    w_rows = wit_ref[...] * ((IDX_HEADS * IDX_DIM) ** -0.5)

    key_row = lax.broadcasted_iota(jnp.int32, tile, 0)
    q_pos = t0 + lax.broadcasted_iota(jnp.int32, tile, 1)

    def score_chunk(kc, carry):
        rmax, rmin = carry
        ks = pl.multiple_of(kc * KEY_CHUNK, KEY_CHUNK)
        d = jnp.dot(kw_ref[pl.ds(ks, KEY_CHUNK), :], qiz_ref[...], preferred_element_type=F32)
        s = jnp.zeros(tile, F32)
        for h in range(IDX_HEADS):
            s = s + jnp.maximum(d[:, h * BLOCK_Q:(h + 1) * BLOCK_Q], 0.0) * w_rows[h:h + 1, :]
        causal = (ks + key_row) <= q_pos
        sc_ref[kc] = jnp.where(causal, s, -jnp.inf)
        rmax = jnp.maximum(rmax, jnp.max(jnp.where(causal, s, -jnp.inf), axis=0, keepdims=True))
        rmin = jnp.minimum(rmin, jnp.min(jnp.where(causal, s, jnp.inf), axis=0, keepdims=True))
        return rmax, rmin

    rmax, rmin = lax.fori_loop(0, n_chunks, score_chunk,
                               (jnp.full(row1, -jnp.inf, F32), jnp.full(row1, jnp.inf, F32)))

    n_causal = (t0 + lax.broadcasted_iota(jnp.int32, row1, 1) + 1).astype(F32)
    short = n_causal <= k_sel

    def count_ge(t):
        def body(kc, c):
            hit = jnp.where(sc_ref[kc] >= t, 1.0, 0.0)
            return c + jnp.sum(hit.reshape(KEY_CHUNK // COUNT_ROWS, COUNT_ROWS, BLOCK_Q), axis=0)
        c = lax.fori_loop(0, n_chunks, body, jnp.zeros((COUNT_ROWS, BLOCK_Q), F32))
        return jnp.sum(c, axis=0, keepdims=True)

    def halve(_, carry):
        lo, hi, c_lo, _ = carry
        mid = lo + (hi - lo) * 0.5
        cnt = count_ge(mid)
        enough = cnt >= k_sel
        shrinks = jnp.where((mid > lo) & (mid < hi), 1.0, 0.0)
        return jnp.where(enough, mid, lo), jnp.where(enough, hi, mid), jnp.where(enough, cnt, c_lo), shrinks

    def pending(carry):
        rounds, todo, _, _, _ = carry
        return (rounds < MAX_BISECT_ITERS // BISECT_ROUND) & (todo > 0.0)

    def bisect_round(carry):
        rounds, _, lo, hi, c_lo = carry
        lo, hi, c_lo, shrinks = lax.fori_loop(0, BISECT_ROUND, halve, (lo, hi, c_lo, jnp.ones(row1, F32)))
        settled = short | (c_lo == k_sel) | (shrinks == 0.0)
        return rounds + 1, jnp.sum(jnp.where(settled, 0.0, 1.0)), lo, hi, c_lo

    _, _, lo, _, _ = lax.while_loop(pending, bisect_round,
                                    (jnp.int32(0), jnp.float32(1.0), rmin, rmax, n_causal))
    thr = jnp.where(short, jnp.finfo(F32).min, lo)
    thr_g = jnp.concatenate([thr] * GRP, axis=1)

    m_ref[...] = jnp.full(m_ref.shape, -1e30, F32)
    l_ref[...] = jnp.zeros(l_ref.shape, F32)
    acc_ref[...] = jnp.zeros(acc_ref.shape, F32)
    scale = HEAD_DIM_A ** -0.5 * LOG2_E

    def attend(near):
        def body(kc, carry):
            ks = pl.multiple_of(kc * KEY_CHUNK, KEY_CHUNK)
            sc = sc_ref[kc]
            keep = jnp.concatenate([sc] * GRP, axis=1) >= thr_g
            for g in range(N_KV_A):
                k_c = k_ref[pl.ds(ks, KEY_CHUNK), g * HEAD_DIM_A:(g + 1) * HEAD_DIM_A]
                lg = jnp.dot(k_c, qz_ref[:, g * GRP_LANES:(g + 1) * GRP_LANES],
                             preferred_element_type=F32) * scale
                if near:
                    lg = lg + bias_ref[i - 2 * kc, g]
                lg = jnp.where(keep, lg, -jnp.inf)
                m_old = m_ref[g]
                m_new = jnp.maximum(m_old, jnp.max(lg, axis=0, keepdims=True))
                alpha = jnp.exp2(m_old - m_new)
                p = jnp.exp2(lg - m_new)
                l_ref[g] = alpha * l_ref[g] + jnp.sum(p, axis=0, keepdims=True)
                pv = jnp.dot(vt_ref[kc, g * HEAD_DIM_A:(g + 1) * HEAD_DIM_A, :], p.astype(BF16),
                             preferred_element_type=F32)
                acc_ref[g] = alpha * acc_ref[g] + pv
                m_ref[g] = m_new
            return carry
        return body

    n_far = jnp.maximum(n_chunks - 2, 0)
    lax.fori_loop(0, n_far, attend(False), 0)
    lax.fori_loop(n_far, n_chunks, attend(True), 0)

    for g in range(N_KV_A):
        out_t = acc_ref[g] / l_ref[g]
        for j in range(GRP):
            h = g * GRP + j
            o_ref[:, h * HEAD_DIM_A:(h + 1) * HEAD_DIM_A] = (
                out_t[:, j * BLOCK_Q:(j + 1) * BLOCK_Q].T.astype(o_ref.dtype))


def _sparse_attention(bias_tiles, q_t, v_t, wi_t, k_n, batch, seq):
    nb = seq // BLOCK_Q
    n_seq_chunks = seq // KEY_CHUNK
    kw_blk = KV_WIDTH // V7X_LANES
    return pl.pallas_call(
        _sparse_attn_kernel,
        grid=(batch, nb),
        in_specs=[pl.BlockSpec(bias_tiles.shape, lambda b, i: (0, 0, 0, 0)),
                  pl.BlockSpec((ATTN_WIDTH, BLOCK_Q), lambda b, i: (0, b * nb + i)),
                  pl.BlockSpec((IDX_HEADS * IDX_DIM, BLOCK_Q), lambda b, i: (1, b * nb + i)),
                  pl.BlockSpec((IDX_HEADS, BLOCK_Q), lambda b, i: (0, b * nb + i)),
                  pl.BlockSpec((seq, V7X_LANES), lambda b, i: (b, kw_blk)),
                  pl.BlockSpec((seq, KV_WIDTH), lambda b, i: (b, 0)),
                  pl.BlockSpec((n_seq_chunks, KV_WIDTH, KEY_CHUNK), lambda b, i: (b, 0, 0))],
        out_specs=pl.BlockSpec((BLOCK_Q, ATTN_WIDTH), lambda b, i: (b * nb + i, 0)),
        out_shape=jax.ShapeDtypeStruct((batch * seq, ATTN_WIDTH), BF16),
        scratch_shapes=[pltpu.VMEM((HEAD_DIM_A, N_HEADS_A * BLOCK_Q), BF16),
                        pltpu.VMEM((V7X_LANES, IDX_HEADS * BLOCK_Q), BF16),
                        pltpu.VMEM((n_seq_chunks, KEY_CHUNK, BLOCK_Q), F32),
                        pltpu.VMEM((N_KV_A, 1, GRP_LANES), F32),
                        pltpu.VMEM((N_KV_A, 1, GRP_LANES), F32),
                        pltpu.VMEM((N_KV_A, HEAD_DIM_A, GRP_LANES), F32)],
        compiler_params=_params(2),
        name="sparse_attention",
    )(bias_tiles, q_t, q_t, wi_t, k_n, k_n, v_t)


def _s5_input_kernel(u_ref, lr_ref, li_ref, ldt_ref, bre_ref, bim_ref,
                     bur_ref, bui_ref, ar_ref, ai_ref, bbr_ref, bbi_ref):
    @pl.when(pl.program_id(0) == 0)
    def _():
        dt = jnp.exp(ldt_ref[...])
        lr = lr_ref[...]
        li = li_ref[...]
        mag = jnp.exp(lr * dt)
        ar = mag * jnp.cos(li * dt)
        ai = mag * jnp.sin(li * dt)
        den = lr * lr + li * li
        nr = ar - 1.0
        cr = (nr * lr + ai * li) / den
        ci = (ai * lr - nr * li) / den
        bre = bre_ref[...]
        bim = bim_ref[...]
        bbr_ref[...] = (cr * bre - ci * bim).astype(BF16)
        bbi_ref[...] = (cr * bim + ci * bre).astype(BF16)
        ar_ref[...] = ar
        ai_ref[...] = ai

    u = u_ref[...].astype(BF16)
    bur_ref[...] = jnp.dot(u, bbr_ref[...], preferred_element_type=F32)
    bui_ref[...] = jnp.dot(u, bbi_ref[...], preferred_element_type=F32)


def _s5_input(us, lam_re, lam_im, log_dt, bre_bd, bim_bd):
    m = us.shape[0]
    tm = ROW_TILE
    row = pl.BlockSpec((1, SSM_STATES), lambda i: (0, 0))
    mat = pl.BlockSpec((SSM_WIDTH, SSM_STATES), lambda i: (0, 0))
    out = pl.BlockSpec((tm, SSM_STATES), lambda i: (i, 0))
    return pl.pallas_call(
        _s5_input_kernel,
        grid=(m // tm,),
        in_specs=[pl.BlockSpec((tm, SSM_WIDTH), lambda i: (i, 0)), row, row, row, mat, mat],
        out_specs=[out, out, row, row],
        out_shape=[jax.ShapeDtypeStruct((m, SSM_STATES), F32)] * 2
        + [jax.ShapeDtypeStruct((1, SSM_STATES), F32)] * 2,
        scratch_shapes=[pltpu.VMEM((SSM_WIDTH, SSM_STATES), BF16)] * 2,
        compiler_params=_params(1),
        name="s5_input",
    )(us, lam_re, lam_im, log_dt, bre_bd, bim_bd)


def _s5_scan_kernel(ar_ref, ai_ref, br_ref, bi_ref, xr_ref, xi_ref, sr_ref, si_ref):
    @pl.when(pl.program_id(0) == 0)
    def _():
        sr_ref[...] = jnp.zeros(sr_ref.shape, F32)
        si_ref[...] = jnp.zeros(si_ref.shape, F32)

    ar = ar_ref[...]
    ai = ai_ref[...]
    steps = br_ref.shape[1]

    def step(t, carry):
        xr, xi = carry
        nr = ar * xr - ai * xi + br_ref[:, t]
        ni = ar * xi + ai * xr + bi_ref[:, t]
        xr_ref[:, t] = nr
        xi_ref[:, t] = ni
        return nr, ni

    xr, xi = lax.fori_loop(0, steps, step, (sr_ref[...], si_ref[...]), unroll=8)
    sr_ref[...] = xr
    si_ref[...] = xi


def _s5_scan(a_re, a_im, bu_re, bu_im):
    batch, seq, rows, lanes = bu_re.shape
    blk = pl.BlockSpec((batch, SCAN_CHUNK, rows, lanes), lambda c: (0, c, 0, 0))
    par = pl.BlockSpec((rows, lanes), lambda c: (0, 0))
    return pl.pallas_call(
        _s5_scan_kernel,
        grid=(seq // SCAN_CHUNK,),
        in_specs=[par, par, blk, blk],
        out_specs=[blk, blk],
        out_shape=[jax.ShapeDtypeStruct(bu_re.shape, F32)] * 2,
        scratch_shapes=[pltpu.VMEM((batch, rows, lanes), F32)] * 2,
        compiler_params=_params(1),
        name="s5_scan",
    )(a_re, a_im, bu_re, bu_im)


def _s5_output_kernel(xr_ref, xi_ref, u_ref, cr_ref, ci_ref, d_ref, wg_ref, bg_ref, o_ref):
    y = (jnp.dot(xr_ref[...].astype(BF16), cr_ref[...], preferred_element_type=F32)
         - jnp.dot(xi_ref[...].astype(BF16), ci_ref[...], preferred_element_type=F32)
         + d_ref[...] * u_ref[...])
    g = jax.nn.gelu(y)
    gate = jax.nn.sigmoid(jnp.dot(g.astype(BF16), wg_ref[...], preferred_element_type=F32) + bg_ref[...])
    o_ref[...] = (g * gate).astype(o_ref.dtype)


def _s5_output(xr, xi, us, cre_bd, cim_bd, d_row, w_glu, b_glu):
    m = us.shape[0]
    tm = ROW_TILE
    state = pl.BlockSpec((tm, SSM_STATES), lambda i: (i, 0))
    chan = pl.BlockSpec((tm, SSM_WIDTH), lambda i: (i, 0))
    cmat = pl.BlockSpec((SSM_STATES, SSM_WIDTH), lambda i: (0, 0))
    row = pl.BlockSpec((1, SSM_WIDTH), lambda i: (0, 0))
    return pl.pallas_call(
        _s5_output_kernel,
        grid=(m // tm,),
        in_specs=[state, state, chan, cmat, cmat, row,
                  pl.BlockSpec((SSM_WIDTH, SSM_WIDTH), lambda i: (0, 0)), row],
        out_specs=chan,
        out_shape=jax.ShapeDtypeStruct((m, SSM_WIDTH), BF16),
        compiler_params=_params(1),
        name="s5_output",
    )(xr, xi, us, cre_bd, cim_bd, d_row, w_glu, b_glu)


def _block_diag(blocks):
    g, r, c = blocks.shape
    eye = jnp.eye(g, dtype=blocks.dtype)
    return (eye[:, None, :, None] * blocks[:, :, None, :]).reshape(g * r, g * c)


def _cross_attn_kernel(q_ref, k_ref, v_ref, o_ref):
    scale = X_HEAD_DIM ** -0.5
    for h in range(X_HEADS):
        cols = slice(h * X_HEAD_DIM, (h + 1) * X_HEAD_DIM)
        lg = lax.dot_general(q_ref[:, cols], k_ref[:, cols], _NT, preferred_element_type=F32) * scale
        p = jnp.exp(lg - jnp.max(lg, axis=-1, keepdims=True))
        o = jnp.dot(p.astype(BF16), v_ref[:, cols], preferred_element_type=F32)
        o_ref[:, cols] = (o / jnp.sum(p, axis=-1, keepdims=True)).astype(o_ref.dtype)


def _cross_attn(q, kv, batch, seq, mem_len):
    width = X_HEADS * X_HEAD_DIM
    tq = ROW_TILE
    nt = seq // tq
    return pl.pallas_call(
        _cross_attn_kernel,
        grid=(batch, nt),
        in_specs=[pl.BlockSpec((tq, width), lambda b, i: (b * nt + i, 0)),
                  pl.BlockSpec((mem_len, width), lambda b, i: (b, 0)),
                  pl.BlockSpec((mem_len, width), lambda b, i: (b, 1))],
        out_specs=pl.BlockSpec((tq, width), lambda b, i: (b * nt + i, 0)),
        out_shape=jax.ShapeDtypeStruct((batch * seq, width), BF16),
        compiler_params=_params(2),
        name="cross_attention",
    )(q, kv, kv)


def _conv_ffn_kernel(x_ref, g_ref, wa_ref, wb_ref, cwa_ref, cwb_ref, cba_ref, cbb_ref, wd_ref, gf_ref,
                     o_ref, xn_ref, ha_ref, hb_ref, ta_ref, tb_ref, acc_ref, *, tiles_per_seq):
    i = pl.program_id(0)
    j = pl.program_id(1)
    tm = x_ref.shape[0]
    halo = V7X_SUBLANES

    @pl.when(j == 0)
    def _():
        xn_ref[...] = _rms(x_ref[...], g_ref[...], EPS).astype(BF16)
        acc_ref[...] = jnp.zeros(acc_ref.shape, F32)

    @pl.when(i % tiles_per_seq == 0)
    def _():
        ha_ref[0:halo] = jnp.zeros((halo, ha_ref.shape[1]), F32)
        hb_ref[0:halo] = jnp.zeros((halo, hb_ref.shape[1]), F32)

    @pl.when(i % tiles_per_seq != 0)
    def _():
        ha_ref[0:halo] = ta_ref[j]
        hb_ref[0:halo] = tb_ref[j]

    xn = xn_ref[...]
    ha_ref[halo:halo + tm] = jnp.dot(xn, wa_ref[...], preferred_element_type=F32)
    hb_ref[halo:halo + tm] = jnp.dot(xn, wb_ref[...], preferred_element_type=F32)
    ta_ref[j] = ha_ref[tm:tm + halo]
    tb_ref[j] = hb_ref[tm:tm + halo]

    def conv(h_ref, cw_ref, cb_ref):
        out = cb_ref[...]
        for tap in range(CONV_WIDTH):
            start = halo - (CONV_WIDTH - 1) + tap
            out = out + h_ref[start:start + tm] * cw_ref[tap:tap + 1]
        return out

    a = conv(ha_ref, cwa_ref, cba_ref)
    b = conv(hb_ref, cwb_ref, cbb_ref)
    act = (a * jax.nn.sigmoid(a) * b).astype(BF16)
    acc_ref[...] += jnp.dot(act, wd_ref[...], preferred_element_type=F32)

    @pl.when(j == pl.num_programs(1) - 1)
    def _():
        o_ref[...] = _rms(x_ref[...] + acc_ref[...], gf_ref[...], EPS)


def _conv_ffn(x, gain, w_up, conv_w, conv_b, w_down, gain_final, seq):
    m, d = x.shape
    d_ff = w_down.shape[0]
    tm, tf = ROW_TILE, COL_TILE
    nj = d_ff // tf
    halo = V7X_SUBLANES
    return pl.pallas_call(
        functools.partial(_conv_ffn_kernel, tiles_per_seq=seq // tm),
        grid=(m // tm, nj),
        in_specs=[pl.BlockSpec((tm, d), lambda i, j: (i, 0)),
                  pl.BlockSpec((1, d), lambda i, j: (0, 0)),
                  pl.BlockSpec((d, tf), lambda i, j: (0, j)),
                  pl.BlockSpec((d, tf), lambda i, j: (0, j + nj)),
                  pl.BlockSpec((CONV_WIDTH, tf), lambda i, j: (0, j)),
                  pl.BlockSpec((CONV_WIDTH, tf), lambda i, j: (0, j + nj)),
                  pl.BlockSpec((1, tf), lambda i, j: (0, j)),
                  pl.BlockSpec((1, tf), lambda i, j: (0, j + nj)),
                  pl.BlockSpec((tf, d), lambda i, j: (j, 0)),
                  pl.BlockSpec((1, d), lambda i, j: (0, 0))],
        out_specs=pl.BlockSpec((tm, d), lambda i, j: (i, 0)),
        out_shape=jax.ShapeDtypeStruct((m, d), F32),
        scratch_shapes=[pltpu.VMEM((tm, d), BF16),
                        pltpu.VMEM((tm + halo, tf), F32),
                        pltpu.VMEM((tm + halo, tf), F32),
                        pltpu.VMEM((nj, halo, tf), F32),
                        pltpu.VMEM((nj, halo, tf), F32),
                        pltpu.VMEM((tm, d), F32)],
        compiler_params=_params(2),
        name="conv_ffn",
    )(x, gain.reshape(1, d), w_up, w_up, conv_w, conv_w, conv_b.reshape(1, -1), conv_b.reshape(1, -1),
      w_down, gain_final.reshape(1, d))


def kernel(x, mem, rel_bias, norm_mix, w_in, ssm_lambda_re, ssm_lambda_im, ssm_log_dt, ssm_b_re, ssm_b_im, ssm_c_re, ssm_c_im, ssm_d, ssm_w_glu, ssm_b_glu, w_branch_a, w_branch_b, w_out, norm_cross, norm_mem, w_cross_q, w_cross_kv, w_cross_o, norm_ffn, w_up, ffn_conv_w, ffn_conv_b, w_down, norm_final):
    batch, seq, d_model = x.shape
    mem_len = mem.shape[1]
    depth = w_in.shape[0]
    assert depth == 1, "the final rmsnorm is fused into the last layer's ConvFFN kernel"
    assert seq % ROW_TILE == 0 and seq % KEY_CHUNK == 0 and seq >= 4 * TOPK_MAX
    m = batch * seq
    xf = x.reshape(m, d_model)
    memf = mem.reshape(batch * mem_len, d_model)
    bias_tiles = _bias_tiles(rel_bias)

    splits = (ATTN_WIDTH, KV_WIDTH, KV_WIDTH, IDX_HEADS * IDX_DIM, IDX_DIM, IDX_HEADS, SSM_WIDTH, d_model, d_model)
    offs = [int(o) for o in np.cumsum((0,) + splits)]
    for l in range(depth):
        w = w_in[l]
        wq, wk, wv, wqi, wki, wwi, wss, wga, wgb = [w[:, offs[n]:offs[n + 1]] for n in range(len(splits))]
        pad = jnp.zeros((d_model, V7X_LANES - IDX_DIM - IDX_HEADS), w.dtype)
        w_t = jnp.concatenate([wq, wqi, wv, wwi], axis=1).T.astype(BF16)
        w_n = jnp.concatenate([wk, wki, wwi, pad], axis=1).astype(BF16)
        w_gate = jnp.concatenate([wga, wgb], axis=1).astype(BF16)

        q_t, v_t, wi_t, k_n = _proj_attn(xf, norm_mix[l], w_t, w_n)
        us = _norm_matmul(xf, norm_mix[l], wss.astype(BF16), F32, COL_TILE, "proj_ssm")
        gates = _norm_matmul(xf, norm_mix[l], w_gate, F32, COL_TILE, "proj_gates")

        y_a = _sparse_attention(bias_tiles, q_t, v_t, wi_t, k_n, batch, seq)

        bre_bd = _block_diag(jnp.transpose(ssm_b_re[l], (0, 2, 1)))
        bim_bd = _block_diag(jnp.transpose(ssm_b_im[l], (0, 2, 1)))
        cre_bd = _block_diag(jnp.transpose(ssm_c_re[l], (0, 2, 1))).astype(BF16)
        cim_bd = _block_diag(jnp.transpose(ssm_c_im[l], (0, 2, 1))).astype(BF16)
        log_dt = jnp.broadcast_to(ssm_log_dt[l][:, None], (SSM_GROUPS, SSM_STATE)).reshape(1, SSM_STATES)
        bu_re, bu_im, a_re, a_im = _s5_input(us, ssm_lambda_re[l].reshape(1, SSM_STATES),
                                             ssm_lambda_im[l].reshape(1, SSM_STATES), log_dt, bre_bd, bim_bd)
        state_rows = SSM_STATES // V7X_LANES
        sshape = (batch, seq, state_rows, V7X_LANES)
        xs_re, xs_im = _s5_scan(a_re.reshape(state_rows, V7X_LANES), a_im.reshape(state_rows, V7X_LANES),
                                bu_re.reshape(sshape), bu_im.reshape(sshape))
        y_b = _s5_output(xs_re.reshape(m, SSM_STATES), xs_im.reshape(m, SSM_STATES), us, cre_bd, cim_bd,
                         ssm_d[l].reshape(1, SSM_WIDTH), ssm_w_glu[l].astype(BF16),
                         ssm_b_glu[l].reshape(1, SSM_WIDTH))

        z = _gated_merge(y_a, y_b, w_branch_a[l].astype(BF16), w_branch_b[l].astype(BF16), gates)
        xf = _matmul_residual(z, w_out[l].astype(BF16), xf, "out_proj")

        qx = _norm_matmul(xf, norm_cross[l], w_cross_q[l].astype(BF16), BF16, COL_TILE, "cross_q")
        kvx = _norm_matmul(memf, norm_mem[l], w_cross_kv[l].astype(BF16), BF16, COL_TILE, "cross_kv")
        ox = _cross_attn(qx, kvx, batch, seq, mem_len)
        xf = _matmul_residual(ox, w_cross_o[l].astype(BF16), xf, "cross_out")

        xf = _conv_ffn(xf, norm_ffn[l], w_up[l].astype(BF16), ffn_conv_w[l], ffn_conv_b[l],
                       w_down[l].astype(BF16), norm_final, seq)
    return xf.reshape(batch, seq, d_model)
```

```python
import functools
import math

import jax
import jax.numpy as jnp
import numpy as np
from jax import lax
from jax.experimental import pallas as pl
from jax.experimental.pallas import tpu as pltpu

N_HEADS_A = 8
N_KV_A = 2
HEAD_DIM_A = 128
ATTN_WIDTH = N_HEADS_A * HEAD_DIM_A
KV_WIDTH = N_KV_A * HEAD_DIM_A
IDX_HEADS = 16
IDX_DIM = 64
TOPK_MAX = 256
BLOCK_Q = 128
REL_BUCKETS = 32
REL_MAX_DIST = 128
SSM_GROUP = 16
SSM_GROUPS = 32
SSM_STATE = 64
SSM_WIDTH = SSM_GROUP * SSM_GROUPS
SSM_STATES = SSM_GROUPS * SSM_STATE
X_HEADS = 4
X_HEAD_DIM = 128
CONV_WIDTH = 3
EPS = 1e-6

V7X_LANES = 128
V7X_SUBLANES = 8
V7X_VMEM_BYTES = 64 * 1024 * 1024
VMEM_LIMIT_BYTES = V7X_VMEM_BYTES - 8 * 1024 * 1024

GRP = N_HEADS_A // N_KV_A
GRP_LANES = GRP * BLOCK_Q
KEY_CHUNK = 2 * BLOCK_Q
N_BIAS_TILES = 4
MAX_BISECT_ITERS = 64
BISECT_ROUND = 4
COUNT_ROWS = 64
LOG2_E = math.log2(math.e)
ROW_TILE = 1024
SMALL_ROW_TILE = 512
COL_TILE = 512
FFN_COL_TILE = 256
SCAN_CHUNK = 128

F32 = jnp.float32
BF16 = jnp.bfloat16
_NT = (((1,), (1,)), ((), ()))


def _params(n_axes):
    return pltpu.CompilerParams(dimension_semantics=("arbitrary",) * n_axes,
                                vmem_limit_bytes=VMEM_LIMIT_BYTES)


def _rms(x, gain, eps):
    return x * lax.rsqrt(jnp.mean(x * x, axis=-1, keepdims=True) + eps) * gain


def _norm_matmul_kernel(x_ref, g_ref, w_ref, o_ref, xn_ref, *, sigmoid):
    @pl.when(pl.program_id(1) == 0)
    def _():
        xn_ref[...] = _rms(x_ref[...], g_ref[...], EPS).astype(BF16)

    y = jnp.dot(xn_ref[...], w_ref[...], preferred_element_type=F32)
    if sigmoid:
        y = jax.nn.sigmoid(y)
    o_ref[...] = y.astype(o_ref.dtype)


def _norm_matmul(x, gain, w, out_dtype, tn, name, sigmoid=False):
    m, k = x.shape
    n = w.shape[1]
    tm = min(ROW_TILE, m)
    return pl.pallas_call(
        functools.partial(_norm_matmul_kernel, sigmoid=sigmoid),
        grid=(m // tm, n // tn),
        in_specs=[pl.BlockSpec((tm, k), lambda i, j: (i, 0)),
                  pl.BlockSpec((1, k), lambda i, j: (0, 0)),
                  pl.BlockSpec((k, tn), lambda i, j: (0, j))],
        out_specs=pl.BlockSpec((tm, tn), lambda i, j: (i, j)),
        out_shape=jax.ShapeDtypeStruct((m, n), out_dtype),
        scratch_shapes=[pltpu.VMEM((tm, k), BF16)],
        compiler_params=_params(2),
        name=name,
    )(x, gain.reshape(1, k), w)


def _proj_attn_kernel(x_ref, g_ref, wt_ref, wn_ref, qt_ref, vt_ref, wit_ref, kn_ref):
    xn = _rms(x_ref[...], g_ref[...], EPS).astype(BF16)
    res = lax.dot_general(wt_ref[...], xn, _NT, preferred_element_type=F32)
    nq = qt_ref.shape[0]
    qt_ref[0:ATTN_WIDTH] = (res[0:ATTN_WIDTH] * (HEAD_DIM_A ** -0.5 * LOG2_E)).astype(BF16)
    qt_ref[ATTN_WIDTH:nq] = res[ATTN_WIDTH:nq].astype(BF16)
    for c in range(vt_ref.shape[0]):
        vt_ref[c] = res[nq:nq + KV_WIDTH, c * KEY_CHUNK:(c + 1) * KEY_CHUNK].astype(BF16)
    wit_ref[...] = res[nq + KV_WIDTH:nq + KV_WIDTH + IDX_HEADS]
    kn_ref[...] = jnp.dot(xn, wn_ref[...], preferred_element_type=F32).astype(BF16)


def _proj_attn(x, gain, w_t, w_n):
    m, k = x.shape
    tm = SMALL_ROW_TILE
    nq = ATTN_WIDTH + IDX_HEADS * IDX_DIM
    return pl.pallas_call(
        _proj_attn_kernel,
        grid=(m // tm,),
        in_specs=[pl.BlockSpec((tm, k), lambda i: (i, 0)),
                  pl.BlockSpec((1, k), lambda i: (0, 0)),
                  pl.BlockSpec(w_t.shape, lambda i: (0, 0)),
                  pl.BlockSpec(w_n.shape, lambda i: (0, 0))],
        out_specs=[pl.BlockSpec((nq, tm), lambda i: (0, i)),
                   pl.BlockSpec((tm // KEY_CHUNK, KV_WIDTH, KEY_CHUNK), lambda i: (i, 0, 0)),
                   pl.BlockSpec((IDX_HEADS, tm), lambda i: (0, i)),
                   pl.BlockSpec((tm, w_n.shape[1]), lambda i: (i, 0))],
        out_shape=[jax.ShapeDtypeStruct((nq, m), BF16),
                   jax.ShapeDtypeStruct((m // KEY_CHUNK, KV_WIDTH, KEY_CHUNK), BF16),
                   jax.ShapeDtypeStruct((IDX_HEADS, m), F32),
                   jax.ShapeDtypeStruct((m, w_n.shape[1]), BF16)],
        compiler_params=_params(1),
        name="proj_attn",
    )(x, gain.reshape(1, k), w_t, w_n)


def _matmul_residual_kernel(a_ref, w_ref, r_ref, o_ref):
    o_ref[...] = r_ref[...] + jnp.dot(a_ref[...], w_ref[...], preferred_element_type=F32)


def _matmul_residual(a, w, res, name):
    m, k = a.shape
    n = w.shape[1]
    tm, tn = ROW_TILE, COL_TILE
    return pl.pallas_call(
        _matmul_residual_kernel,
        grid=(m // tm, n // tn),
        in_specs=[pl.BlockSpec((tm, k), lambda i, j: (i, 0)),
                  pl.BlockSpec((k, tn), lambda i, j: (0, j)),
                  pl.BlockSpec((tm, tn), lambda i, j: (i, j))],
        out_specs=pl.BlockSpec((tm, tn), lambda i, j: (i, j)),
        out_shape=jax.ShapeDtypeStruct((m, n), F32),
        compiler_params=_params(2),
        name=name,
    )(a, w, res)


def _gated_merge_kernel(ya_ref, yb_ref, wa_ref, wb_ref, ga_ref, gb_ref, o_ref):
    za = jnp.dot(ya_ref[...], wa_ref[...], preferred_element_type=F32)
    zb = jnp.dot(yb_ref[...], wb_ref[...], preferred_element_type=F32)
    z = ga_ref[...].astype(F32) * za + gb_ref[...].astype(F32) * zb
    o_ref[...] = z.astype(o_ref.dtype)


def _gated_merge(ya, yb, wa, wb, gates):
    m = ya.shape[0]
    n = wa.shape[1]
    tm, tn = ROW_TILE, COL_TILE
    nj = n // tn
    return pl.pallas_call(
        _gated_merge_kernel,
        grid=(m // tm, nj),
        in_specs=[pl.BlockSpec((tm, ya.shape[1]), lambda i, j: (i, 0)),
                  pl.BlockSpec((tm, yb.shape[1]), lambda i, j: (i, 0)),
                  pl.BlockSpec((wa.shape[0], tn), lambda i, j: (0, j)),
                  pl.BlockSpec((wb.shape[0], tn), lambda i, j: (0, j)),
                  pl.BlockSpec((tm, tn), lambda i, j: (i, j)),
                  pl.BlockSpec((tm, tn), lambda i, j: (i, j + nj))],
        out_specs=pl.BlockSpec((tm, tn), lambda i, j: (i, j)),
        out_shape=jax.ShapeDtypeStruct((m, n), BF16),
        compiler_params=_params(2),
        name="gated_merge",
    )(ya, yb, wa, wb, gates, gates)


def _t5_bucket(n):
    max_exact = REL_BUCKETS // 2
    n = jnp.maximum(n, 0)
    nf = jnp.maximum(n, 1).astype(F32)
    large = max_exact + (jnp.log(nf / max_exact) / math.log(REL_MAX_DIST / max_exact)
                         * (REL_BUCKETS - max_exact)).astype(jnp.int32)
    large = jnp.minimum(large, REL_BUCKETS - 1)
    return jnp.where(n < max_exact, n, large)


def _bias_tiles_kernel(rb_ref, o_ref):
    shape = (KEY_CHUNK, BLOCK_Q)
    key = lax.broadcasted_iota(jnp.int32, shape, 0)
    qry = lax.broadcasted_iota(jnp.int32, shape, 1)
    for o in range(N_BIAS_TILES):
        bucket = _t5_bucket(o * BLOCK_Q + qry - key)
        for h in range(N_HEADS_A):
            def pick(b, t, bucket=bucket, h=h):
                return jnp.where(bucket == b, rb_ref[b, h], t)
            tile = lax.fori_loop(0, REL_BUCKETS, pick, jnp.zeros(shape, F32))
            j = h % GRP
            o_ref[o, h // GRP, :, j * BLOCK_Q:(j + 1) * BLOCK_Q] = (tile - rb_ref[REL_BUCKETS - 1, h]) * LOG2_E


def _bias_tiles(rel_bias):
    return pl.pallas_call(
        _bias_tiles_kernel,
        in_specs=[pl.BlockSpec(memory_space=pltpu.SMEM)],
        out_specs=pl.BlockSpec(memory_space=pltpu.VMEM),
        out_shape=jax.ShapeDtypeStruct((N_BIAS_TILES, N_KV_A, KEY_CHUNK, GRP_LANES), F32),
        name="rel_bias_tiles",
    )(rel_bias)


def _sparse_attn_kernel(bias_ref, qt_ref, qit_ref, wit_ref, kw_ref, k_ref, vt_ref, o_ref,
                        qz_ref, qiz_ref, sc_ref, prod_ref, m_ref, l_ref, acc_ref):
    i = pl.program_id(1)
    t0 = i * BLOCK_Q
    n_chunks = (i + 2) // 2
    tile = (KEY_CHUNK, BLOCK_Q)
    row1 = (1, BLOCK_Q)
    k_sel = float(TOPK_MAX)

    @pl.when((pl.program_id(0) == 0) & (i == 0))
    def _():
        qiz_ref[...] = jnp.zeros(qiz_ref.shape, BF16)

    for h in range(IDX_HEADS):
        qiz_ref[0:IDX_DIM, h * BLOCK_Q:(h + 1) * BLOCK_Q] = qit_ref[h * IDX_DIM:(h + 1) * IDX_DIM, :]
    for h in range(N_HEADS_A):
        qz_ref[:, h * BLOCK_Q:(h + 1) * BLOCK_Q] = qt_ref[h * HEAD_DIM_A:(h + 1) * HEAD_DIM_A, :]
    w_rows = wit_ref[...] * ((IDX_HEADS * IDX_DIM) ** -0.5)

    key_row = lax.broadcasted_iota(jnp.int32, tile, 0)
    q_pos = t0 + lax.broadcasted_iota(jnp.int32, tile, 1)

    last = n_chunks - 1
    n_pairs = (n_chunks + 1) // 2

    def idx_products(kc, slot):
        ks = pl.multiple_of(jnp.minimum(kc, last) * KEY_CHUNK, KEY_CHUNK)
        prod_ref[slot] = jnp.dot(kw_ref[pl.ds(ks, KEY_CHUNK), :], qiz_ref[...], preferred_element_type=F32)

    def reduce_scores(kc, slot, carry):
        rmax, rmin = carry
        kc = jnp.minimum(kc, last)
        s = jnp.zeros(tile, F32)
        for h in range(IDX_HEADS):
            d = prod_ref[slot, :, h * BLOCK_Q:(h + 1) * BLOCK_Q]
            s = s + jnp.maximum(d, 0.0) * w_rows[h:h + 1, :]
        causal = (kc * KEY_CHUNK + key_row) <= q_pos
        sc_ref[kc] = jnp.where(causal, s, -jnp.inf)
        rmax = jnp.maximum(rmax, jnp.max(jnp.where(causal, s, -jnp.inf), axis=0, keepdims=True))
        rmin = jnp.minimum(rmin, jnp.min(jnp.where(causal, s, jnp.inf), axis=0, keepdims=True))
        return rmax, rmin

    def score_pair(j, carry):
        idx_products(2 * j + 1, 1)
        carry = reduce_scores(2 * j, 0, carry)
        idx_products(2 * j + 2, 0)
        return reduce_scores(2 * j + 1, 1, carry)

    idx_products(0, 0)
    rmax, rmin = lax.fori_loop(0, n_pairs, score_pair,
                               (jnp.full(row1, -jnp.inf, F32), jnp.full(row1, jnp.inf, F32)))

    n_causal = (t0 + lax.broadcasted_iota(jnp.int32, row1, 1) + 1).astype(F32)
    short = n_causal <= k_sel

    def count_ge(t):
        def body(kc, c):
            hit = jnp.where(sc_ref[kc] >= t, 1.0, 0.0)
            return c + jnp.sum(hit.reshape(KEY_CHUNK // COUNT_ROWS, COUNT_ROWS, BLOCK_Q), axis=0)
        c = lax.fori_loop(0, n_chunks, body, jnp.zeros((COUNT_ROWS, BLOCK_Q), F32))
        return jnp.sum(c, axis=0, keepdims=True)

    def halve(_, carry):
        lo, hi, c_lo, _ = carry
        mid = lo + (hi - lo) * 0.5
        cnt = count_ge(mid)
        enough = cnt >= k_sel
        shrinks = jnp.where((mid > lo) & (mid < hi), 1.0, 0.0)
        return jnp.where(enough, mid, lo), jnp.where(enough, hi, mid), jnp.where(enough, cnt, c_lo), shrinks

    def pending(carry):
        rounds, todo, _, _, _ = carry
        return (rounds < MAX_BISECT_ITERS // BISECT_ROUND) & (todo > 0.0)

    def bisect_round(carry):
        rounds, _, lo, hi, c_lo = carry
        lo, hi, c_lo, shrinks = lax.fori_loop(0, BISECT_ROUND, halve, (lo, hi, c_lo, jnp.ones(row1, F32)))
        settled = short | (c_lo == k_sel) | (shrinks == 0.0)
        return rounds + 1, jnp.sum(jnp.where(settled, 0.0, 1.0)), lo, hi, c_lo

    _, _, lo, _, _ = lax.while_loop(pending, bisect_round,
                                    (jnp.int32(0), jnp.float32(1.0), rmin, rmax, n_causal))
    thr = jnp.where(short, jnp.finfo(F32).min, lo)
    thr_g = jnp.concatenate([thr] * GRP, axis=1)

    m_ref[...] = jnp.full(m_ref.shape, -1e30, F32)
    l_ref[...] = jnp.zeros(l_ref.shape, F32)
    acc_ref[...] = jnp.zeros(acc_ref.shape, F32)

    def qk_products(kc, slot):
        ks = pl.multiple_of(jnp.minimum(kc, last) * KEY_CHUNK, KEY_CHUNK)
        for g in range(N_KV_A):
            k_c = k_ref[pl.ds(ks, KEY_CHUNK), g * HEAD_DIM_A:(g + 1) * HEAD_DIM_A]
            prod_ref[slot, :, g * GRP_LANES:(g + 1) * GRP_LANES] = jnp.dot(
                k_c, qz_ref[:, g * GRP_LANES:(g + 1) * GRP_LANES], preferred_element_type=F32)

    def softmax_pv(kc, slot, near):
        live_thr = jnp.where(kc <= last, thr_g, jnp.inf) if near else thr_g
        kc = jnp.minimum(kc, last) if near else kc
        keep = jnp.concatenate([sc_ref[kc]] * GRP, axis=1) >= live_thr
        for g in range(N_KV_A):
            lg = prod_ref[slot, :, g * GRP_LANES:(g + 1) * GRP_LANES]
            if near:
                lg = lg + bias_ref[jnp.minimum(i - 2 * kc, N_BIAS_TILES - 1), g]
            lg = jnp.where(keep, lg, -jnp.inf)
            m_old = m_ref[g]
            m_new = jnp.maximum(m_old, jnp.max(lg, axis=0, keepdims=True))
            alpha = jnp.exp2(m_old - m_new)
            p = jnp.exp2(lg - m_new)
            l_ref[g] = alpha * l_ref[g] + jnp.sum(p, axis=0, keepdims=True)
            pv = jnp.dot(vt_ref[kc, g * HEAD_DIM_A:(g + 1) * HEAD_DIM_A, :], p.astype(BF16),
                         preferred_element_type=F32)
            acc_ref[g] = alpha * acc_ref[g] + pv
            m_ref[g] = m_new

    def attend(near):
        def body(j, carry):
            qk_products(2 * j + 1, 1)
            softmax_pv(2 * j, 0, near)
            qk_products(2 * j + 2, 0)
            softmax_pv(2 * j + 1, 1, near)
            return carry
        return body

    n_far = jnp.maximum(n_pairs - 2, 0)
    qk_products(0, 0)
    lax.fori_loop(0, n_far, attend(False), 0)
    lax.fori_loop(n_far, n_pairs, attend(True), 0)

    for g in range(N_KV_A):
        out_t = acc_ref[g] / l_ref[g]
        for j in range(GRP):
            h = g * GRP + j
            o_ref[:, h * HEAD_DIM_A:(h + 1) * HEAD_DIM_A] = (
                out_t[:, j * BLOCK_Q:(j + 1) * BLOCK_Q].T.astype(o_ref.dtype))


def _sparse_attention(bias_tiles, q_t, v_t, wi_t, k_n, batch, seq):
    nb = seq // BLOCK_Q
    n_seq_chunks = seq // KEY_CHUNK
    kw_blk = KV_WIDTH // V7X_LANES
    return pl.pallas_call(
        _sparse_attn_kernel,
        grid=(batch, nb),
        in_specs=[pl.BlockSpec(bias_tiles.shape, lambda b, i: (0, 0, 0, 0)),
                  pl.BlockSpec((ATTN_WIDTH, BLOCK_Q), lambda b, i: (0, b * nb + i)),
                  pl.BlockSpec((IDX_HEADS * IDX_DIM, BLOCK_Q), lambda b, i: (1, b * nb + i)),
                  pl.BlockSpec((IDX_HEADS, BLOCK_Q), lambda b, i: (0, b * nb + i)),
                  pl.BlockSpec((seq, V7X_LANES), lambda b, i: (b, kw_blk)),
                  pl.BlockSpec((seq, KV_WIDTH), lambda b, i: (b, 0)),
                  pl.BlockSpec((n_seq_chunks, KV_WIDTH, KEY_CHUNK), lambda b, i: (b, 0, 0))],
        out_specs=pl.BlockSpec((BLOCK_Q, ATTN_WIDTH), lambda b, i: (b * nb + i, 0)),
        out_shape=jax.ShapeDtypeStruct((batch * seq, ATTN_WIDTH), BF16),
        scratch_shapes=[pltpu.VMEM((HEAD_DIM_A, N_HEADS_A * BLOCK_Q), BF16),
                        pltpu.VMEM((V7X_LANES, IDX_HEADS * BLOCK_Q), BF16),
                        pltpu.VMEM((n_seq_chunks, KEY_CHUNK, BLOCK_Q), F32),
                        pltpu.VMEM((2, KEY_CHUNK, IDX_HEADS * BLOCK_Q), F32),
                        pltpu.VMEM((N_KV_A, 1, GRP_LANES), F32),
                        pltpu.VMEM((N_KV_A, 1, GRP_LANES), F32),
                        pltpu.VMEM((N_KV_A, HEAD_DIM_A, GRP_LANES), F32)],
        compiler_params=_params(2),
        name="sparse_attention",
    )(bias_tiles, q_t, q_t, wi_t, k_n, k_n, v_t)


def _s5_input_kernel(u_ref, lr_ref, li_ref, ldt_ref, bre_ref, bim_ref,
                     bur_ref, bui_ref, ar_ref, ai_ref, bbr_ref, bbi_ref):
    @pl.when(pl.program_id(0) == 0)
    def _():
        dt = jnp.exp(ldt_ref[...])
        lr = lr_ref[...]
        li = li_ref[...]
        mag = jnp.exp(lr * dt)
        ar = mag * jnp.cos(li * dt)
        ai = mag * jnp.sin(li * dt)
        den = lr * lr + li * li
        nr = ar - 1.0
        cr = (nr * lr + ai * li) / den
        ci = (ai * lr - nr * li) / den
        bre = bre_ref[...]
        bim = bim_ref[...]
        bbr_ref[...] = (cr * bre - ci * bim).astype(BF16)
        bbi_ref[...] = (cr * bim + ci * bre).astype(BF16)
        ar_ref[...] = ar
        ai_ref[...] = ai

    u = u_ref[...].astype(BF16)
    bur_ref[...] = jnp.dot(u, bbr_ref[...], preferred_element_type=F32)
    bui_ref[...] = jnp.dot(u, bbi_ref[...], preferred_element_type=F32)


def _s5_input(us, lam_re, lam_im, log_dt, bre_bd, bim_bd):
    m = us.shape[0]
    tm = SMALL_ROW_TILE
    row = pl.BlockSpec((1, SSM_STATES), lambda i: (0, 0))
    mat = pl.BlockSpec((SSM_WIDTH, SSM_STATES), lambda i: (0, 0))
    out = pl.BlockSpec((tm, SSM_STATES), lambda i: (i, 0))
    return pl.pallas_call(
        _s5_input_kernel,
        grid=(m // tm,),
        in_specs=[pl.BlockSpec((tm, SSM_WIDTH), lambda i: (i, 0)), row, row, row, mat, mat],
        out_specs=[out, out, row, row],
        out_shape=[jax.ShapeDtypeStruct((m, SSM_STATES), F32)] * 2
        + [jax.ShapeDtypeStruct((1, SSM_STATES), F32)] * 2,
        scratch_shapes=[pltpu.VMEM((SSM_WIDTH, SSM_STATES), BF16)] * 2,
        compiler_params=_params(1),
        name="s5_input",
    )(us, lam_re, lam_im, log_dt, bre_bd, bim_bd)


def _s5_scan_kernel(ar_ref, ai_ref, br_ref, bi_ref, xr_ref, xi_ref, sr_ref, si_ref):
    @pl.when(pl.program_id(0) == 0)
    def _():
        sr_ref[...] = jnp.zeros(sr_ref.shape, F32)
        si_ref[...] = jnp.zeros(si_ref.shape, F32)

    ar = ar_ref[...]
    ai = ai_ref[...]
    steps = br_ref.shape[1]

    def step(t, carry):
        xr, xi = carry
        nr = ar * xr - ai * xi + br_ref[:, t]
        ni = ar * xi + ai * xr + bi_ref[:, t]
        xr_ref[:, t] = nr.astype(xr_ref.dtype)
        xi_ref[:, t] = ni.astype(xi_ref.dtype)
        return nr, ni

    xr, xi = lax.fori_loop(0, steps, step, (sr_ref[...], si_ref[...]), unroll=8)
    sr_ref[...] = xr
    si_ref[...] = xi


def _s5_scan(a_re, a_im, bu_re, bu_im):
    batch, seq, rows, lanes = bu_re.shape
    blk = pl.BlockSpec((batch, SCAN_CHUNK, rows, lanes), lambda c: (0, c, 0, 0))
    par = pl.BlockSpec((rows, lanes), lambda c: (0, 0))
    return pl.pallas_call(
        _s5_scan_kernel,
        grid=(seq // SCAN_CHUNK,),
        in_specs=[par, par, blk, blk],
        out_specs=[blk, blk],
        out_shape=[jax.ShapeDtypeStruct(bu_re.shape, BF16)] * 2,
        scratch_shapes=[pltpu.VMEM((batch, rows, lanes), F32)] * 2,
        compiler_params=_params(1),
        name="s5_scan",
    )(a_re, a_im, bu_re, bu_im)


def _s5_output_kernel(xr_ref, xi_ref, u_ref, cr_ref, ci_ref, d_ref, wg_ref, bg_ref, o_ref):
    y = (jnp.dot(xr_ref[...], cr_ref[...], preferred_element_type=F32)
         - jnp.dot(xi_ref[...], ci_ref[...], preferred_element_type=F32)
         + d_ref[...] * u_ref[...])
    g = jax.nn.gelu(y)
    gate = jax.nn.sigmoid(jnp.dot(g.astype(BF16), wg_ref[...], preferred_element_type=F32) + bg_ref[...])
    o_ref[...] = (g * gate).astype(o_ref.dtype)


def _s5_output(xr, xi, us, cre_bd, cim_bd, d_row, w_glu, b_glu):
    m = us.shape[0]
    tm = SMALL_ROW_TILE
    state = pl.BlockSpec((tm, SSM_STATES), lambda i: (i, 0))
    chan = pl.BlockSpec((tm, SSM_WIDTH), lambda i: (i, 0))
    cmat = pl.BlockSpec((SSM_STATES, SSM_WIDTH), lambda i: (0, 0))
    row = pl.BlockSpec((1, SSM_WIDTH), lambda i: (0, 0))
    return pl.pallas_call(
        _s5_output_kernel,
        grid=(m // tm,),
        in_specs=[state, state, chan, cmat, cmat, row,
                  pl.BlockSpec((SSM_WIDTH, SSM_WIDTH), lambda i: (0, 0)), row],
        out_specs=chan,
        out_shape=jax.ShapeDtypeStruct((m, SSM_WIDTH), BF16),
        compiler_params=_params(1),
        name="s5_output",
    )(xr, xi, us, cre_bd, cim_bd, d_row, w_glu, b_glu)


def _block_diag(blocks):
    g, r, c = blocks.shape
    eye = jnp.eye(g, dtype=blocks.dtype)
    return (eye[:, None, :, None] * blocks[:, :, None, :]).reshape(g * r, g * c)


def _cross_attn_kernel(q_ref, k_ref, v_ref, o_ref):
    scale = X_HEAD_DIM ** -0.5
    for h in range(X_HEADS):
        cols = slice(h * X_HEAD_DIM, (h + 1) * X_HEAD_DIM)
        lg = lax.dot_general(q_ref[:, cols], k_ref[:, cols], _NT, preferred_element_type=F32) * scale
        p = jnp.exp(lg - jnp.max(lg, axis=-1, keepdims=True))
        o = jnp.dot(p.astype(BF16), v_ref[:, cols], preferred_element_type=F32)
        o_ref[:, cols] = (o / jnp.sum(p, axis=-1, keepdims=True)).astype(o_ref.dtype)


def _cross_attn(q, kv, batch, seq, mem_len):
    width = X_HEADS * X_HEAD_DIM
    tq = SMALL_ROW_TILE
    nt = seq // tq
    return pl.pallas_call(
        _cross_attn_kernel,
        grid=(batch, nt),
        in_specs=[pl.BlockSpec((tq, width), lambda b, i: (b * nt + i, 0)),
                  pl.BlockSpec((mem_len, width), lambda b, i: (b, 0)),
                  pl.BlockSpec((mem_len, width), lambda b, i: (b, 1))],
        out_specs=pl.BlockSpec((tq, width), lambda b, i: (b * nt + i, 0)),
        out_shape=jax.ShapeDtypeStruct((batch * seq, width), BF16),
        compiler_params=_params(2),
        name="cross_attention",
    )(q, kv, kv)


def _conv_ffn_kernel(x_ref, g_ref, wa_ref, wb_ref, cwa_ref, cwb_ref, cba_ref, cbb_ref, wd_ref, gf_ref,
                     o_ref, xn_ref, ha_ref, hb_ref, ta_ref, tb_ref, *, tiles_per_seq):
    i = pl.program_id(0)
    j = pl.program_id(1)
    tm = x_ref.shape[0]
    halo = V7X_SUBLANES

    @pl.when(j == 0)
    def _():
        xn_ref[...] = _rms(x_ref[...], g_ref[...], EPS).astype(BF16)
        o_ref[...] = jnp.zeros(o_ref.shape, F32)

    @pl.when(i % tiles_per_seq == 0)
    def _():
        ha_ref[0:halo] = jnp.zeros((halo, ha_ref.shape[1]), F32)
        hb_ref[0:halo] = jnp.zeros((halo, hb_ref.shape[1]), F32)

    @pl.when(i % tiles_per_seq != 0)
    def _():
        ha_ref[0:halo] = ta_ref[j]
        hb_ref[0:halo] = tb_ref[j]

    xn = xn_ref[...]
    ha_ref[halo:halo + tm] = jnp.dot(xn, wa_ref[...], preferred_element_type=F32)
    hb_ref[halo:halo + tm] = jnp.dot(xn, wb_ref[...], preferred_element_type=F32)
    ta_ref[j] = ha_ref[tm:tm + halo]
    tb_ref[j] = hb_ref[tm:tm + halo]

    def conv(h_ref, cw_ref, cb_ref):
        out = cb_ref[...]
        for tap in range(CONV_WIDTH):
            start = halo - (CONV_WIDTH - 1) + tap
            out = out + h_ref[start:start + tm] * cw_ref[tap:tap + 1]
        return out

    a = conv(ha_ref, cwa_ref, cba_ref)
    b = conv(hb_ref, cwb_ref, cbb_ref)
    act = (a * jax.nn.sigmoid(a) * b).astype(BF16)
    o_ref[...] += jnp.dot(act, wd_ref[...], preferred_element_type=F32)

    @pl.when(j == pl.num_programs(1) - 1)
    def _():
        o_ref[...] = _rms(x_ref[...] + o_ref[...], gf_ref[...], EPS)


def _conv_ffn(x, gain, w_up, conv_w, conv_b, w_down, gain_final, seq):
    m, d = x.shape
    d_ff = w_down.shape[0]
    tm, tf = ROW_TILE, FFN_COL_TILE
    nj = d_ff // tf
    halo = V7X_SUBLANES
    return pl.pallas_call(
        functools.partial(_conv_ffn_kernel, tiles_per_seq=seq // tm),
        grid=(m // tm, nj),
        in_specs=[pl.BlockSpec((tm, d), lambda i, j: (i, 0)),
                  pl.BlockSpec((1, d), lambda i, j: (0, 0)),
                  pl.BlockSpec((d, tf), lambda i, j: (0, j)),
                  pl.BlockSpec((d, tf), lambda i, j: (0, j + nj)),
                  pl.BlockSpec((CONV_WIDTH, tf), lambda i, j: (0, j)),
                  pl.BlockSpec((CONV_WIDTH, tf), lambda i, j: (0, j + nj)),
                  pl.BlockSpec((1, tf), lambda i, j: (0, j)),
                  pl.BlockSpec((1, tf), lambda i, j: (0, j + nj)),
                  pl.BlockSpec((tf, d), lambda i, j: (j, 0)),
                  pl.BlockSpec((1, d), lambda i, j: (0, 0))],
        out_specs=pl.BlockSpec((tm, d), lambda i, j: (i, 0)),
        out_shape=jax.ShapeDtypeStruct((m, d), F32),
        scratch_shapes=[pltpu.VMEM((tm, d), BF16),
                        pltpu.VMEM((tm + halo, tf), F32),
                        pltpu.VMEM((tm + halo, tf), F32),
                        pltpu.VMEM((nj, halo, tf), F32),
                        pltpu.VMEM((nj, halo, tf), F32)],
        compiler_params=_params(2),
        name="conv_ffn",
    )(x, gain.reshape(1, d), w_up, w_up, conv_w, conv_w, conv_b.reshape(1, -1), conv_b.reshape(1, -1),
      w_down, gain_final.reshape(1, d))


def kernel(x, mem, rel_bias, norm_mix, w_in, ssm_lambda_re, ssm_lambda_im, ssm_log_dt, ssm_b_re, ssm_b_im, ssm_c_re, ssm_c_im, ssm_d, ssm_w_glu, ssm_b_glu, w_branch_a, w_branch_b, w_out, norm_cross, norm_mem, w_cross_q, w_cross_kv, w_cross_o, norm_ffn, w_up, ffn_conv_w, ffn_conv_b, w_down, norm_final):
    batch, seq, d_model = x.shape
    mem_len = mem.shape[1]
    depth = w_in.shape[0]
    assert depth == 1, "the final rmsnorm is fused into the last layer's ConvFFN kernel"
    assert seq % ROW_TILE == 0 and seq % KEY_CHUNK == 0 and seq >= 4 * TOPK_MAX
    m = batch * seq
    xf = x.reshape(m, d_model)
    memf = mem.reshape(batch * mem_len, d_model)
    bias_tiles = _bias_tiles(rel_bias)

    splits = (ATTN_WIDTH, KV_WIDTH, KV_WIDTH, IDX_HEADS * IDX_DIM, IDX_DIM, IDX_HEADS, SSM_WIDTH, d_model, d_model)
    offs = [int(o) for o in np.cumsum((0,) + splits)]
    for l in range(depth):
        w = w_in[l]
        wq, wk, wv, wqi, wki, wwi, wss, wga, wgb = [w[:, offs[n]:offs[n + 1]] for n in range(len(splits))]
        pad = jnp.zeros((d_model, V7X_LANES - IDX_DIM - IDX_HEADS), w.dtype)
        w_t = jnp.concatenate([wq, wqi, wv, wwi], axis=1).T.astype(BF16)
        w_n = jnp.concatenate([wk, wki, wwi, pad], axis=1).astype(BF16)
        w_gate = jnp.concatenate([wga, wgb], axis=1).astype(BF16)

        q_t, v_t, wi_t, k_n = _proj_attn(xf, norm_mix[l], w_t, w_n)
        us = _norm_matmul(xf, norm_mix[l], wss.astype(BF16), F32, COL_TILE, "proj_ssm")
        gates = _norm_matmul(xf, norm_mix[l], w_gate, BF16, COL_TILE, "proj_gates", sigmoid=True)

        y_a = _sparse_attention(bias_tiles, q_t, v_t, wi_t, k_n, batch, seq)

        bre_bd = _block_diag(jnp.transpose(ssm_b_re[l], (0, 2, 1)))
        bim_bd = _block_diag(jnp.transpose(ssm_b_im[l], (0, 2, 1)))
        cre_bd = _block_diag(jnp.transpose(ssm_c_re[l], (0, 2, 1))).astype(BF16)
        cim_bd = _block_diag(jnp.transpose(ssm_c_im[l], (0, 2, 1))).astype(BF16)
        log_dt = jnp.broadcast_to(ssm_log_dt[l][:, None], (SSM_GROUPS, SSM_STATE)).reshape(1, SSM_STATES)
        bu_re, bu_im, a_re, a_im = _s5_input(us, ssm_lambda_re[l].reshape(1, SSM_STATES),
                                             ssm_lambda_im[l].reshape(1, SSM_STATES), log_dt, bre_bd, bim_bd)
        state_rows = SSM_STATES // V7X_LANES
        sshape = (batch, seq, state_rows, V7X_LANES)
        xs_re, xs_im = _s5_scan(a_re.reshape(state_rows, V7X_LANES), a_im.reshape(state_rows, V7X_LANES),
                                bu_re.reshape(sshape), bu_im.reshape(sshape))
        y_b = _s5_output(xs_re.reshape(m, SSM_STATES), xs_im.reshape(m, SSM_STATES), us, cre_bd, cim_bd,
                         ssm_d[l].reshape(1, SSM_WIDTH), ssm_w_glu[l].astype(BF16),
                         ssm_b_glu[l].reshape(1, SSM_WIDTH))

        z = _gated_merge(y_a, y_b, w_branch_a[l].astype(BF16), w_branch_b[l].astype(BF16), gates)
        xf = _matmul_residual(z, w_out[l].astype(BF16), xf, "out_proj")

        qx = _norm_matmul(xf, norm_cross[l], w_cross_q[l].astype(BF16), BF16, COL_TILE, "cross_q")
        kvx = _norm_matmul(memf, norm_mem[l], w_cross_kv[l].astype(BF16), BF16, COL_TILE, "cross_kv")
        ox = _cross_attn(qx, kvx, batch, seq, mem_len)
        xf = _matmul_residual(ox, w_cross_o[l].astype(BF16), xf, "cross_out")

        xf = _conv_ffn(xf, norm_ffn[l], w_up[l].astype(BF16), ffn_conv_w[l], ffn_conv_b[l],
                       w_down[l].astype(BF16), norm_final, seq)
    return xf.reshape(batch, seq, d_model)
```

```python
import functools
import math

import jax
import jax.numpy as jnp
import numpy as np
from jax import lax
from jax.experimental import pallas as pl
from jax.experimental.pallas import tpu as pltpu

N_HEADS_A = 8
N_KV_A = 2
HEAD_DIM_A = 128
ATTN_WIDTH = N_HEADS_A * HEAD_DIM_A
KV_WIDTH = N_KV_A * HEAD_DIM_A
IDX_HEADS = 16
IDX_DIM = 64
TOPK_MAX = 256
BLOCK_Q = 128
REL_BUCKETS = 32
REL_MAX_DIST = 128
SSM_GROUP = 16
SSM_GROUPS = 32
SSM_STATE = 64
SSM_WIDTH = SSM_GROUP * SSM_GROUPS
SSM_STATES = SSM_GROUPS * SSM_STATE
X_HEADS = 4
X_HEAD_DIM = 128
CONV_WIDTH = 3
EPS = 1e-6

V7X_LANES = 128
V7X_SUBLANES = 8
V7X_VMEM_BYTES = 64 * 1024 * 1024
VMEM_LIMIT_BYTES = V7X_VMEM_BYTES - 8 * 1024 * 1024

GRP = N_HEADS_A // N_KV_A
GRP_LANES = GRP * BLOCK_Q
V_ROWS = HEAD_DIM_A + 16
KEY_CHUNK = 2 * BLOCK_Q
N_BIAS_TILES = 4
MAX_BISECT_ITERS = 64
BISECT_ROUND = 4
COUNT_ROWS = 64
LOG2_E = math.log2(math.e)
ROW_TILE = 1024
SMALL_ROW_TILE = 512
COL_TILE = 512
FFN_ROW_TILE = 512
FFN_ROW_SLABS = 2
SCAN_CHUNK = 128

F32 = jnp.float32
BF16 = jnp.bfloat16
_NT = (((1,), (1,)), ((), ()))


def _params(n_axes):
    return pltpu.CompilerParams(dimension_semantics=("arbitrary",) * n_axes,
                                vmem_limit_bytes=VMEM_LIMIT_BYTES)


def _rms(x, gain, eps):
    return x * lax.rsqrt(jnp.mean(x * x, axis=-1, keepdims=True) + eps) * gain


def _norm_matmul_kernel(x_ref, g_ref, w_ref, o_ref, xn_ref, *, sigmoid):
    @pl.when(pl.program_id(1) == 0)
    def _():
        xn_ref[...] = _rms(x_ref[...], g_ref[...], EPS).astype(BF16)

    tm = o_ref.shape[0]
    rows = min(tm, SMALL_ROW_TILE) if sigmoid else tm
    for r in range(0, tm, rows):
        y = jnp.dot(xn_ref[r:r + rows], w_ref[...], preferred_element_type=F32)
        if sigmoid:
            y = jax.nn.sigmoid(y)
        o_ref[r:r + rows] = y.astype(o_ref.dtype)


def _norm_matmul(x, gain, w, out_dtype, tn, name, sigmoid=False):
    m, k = x.shape
    n = w.shape[1]
    tm = min(ROW_TILE, m)
    return pl.pallas_call(
        functools.partial(_norm_matmul_kernel, sigmoid=sigmoid),
        grid=(m // tm, n // tn),
        in_specs=[pl.BlockSpec((tm, k), lambda i, j: (i, 0)),
                  pl.BlockSpec((1, k), lambda i, j: (0, 0)),
                  pl.BlockSpec((k, tn), lambda i, j: (0, j))],
        out_specs=pl.BlockSpec((tm, tn), lambda i, j: (i, j)),
        out_shape=jax.ShapeDtypeStruct((m, n), out_dtype),
        scratch_shapes=[pltpu.VMEM((tm, k), BF16)],
        compiler_params=_params(2),
        name=name,
    )(x, gain.reshape(1, k), w)


def _proj_attn_kernel(x_ref, g_ref, wt_ref, wn_ref, qt_ref, vt_ref, wit_ref, kn_ref):
    xn = _rms(x_ref[...], g_ref[...], EPS).astype(BF16)
    res = lax.dot_general(wt_ref[...], xn, _NT, preferred_element_type=F32)
    nq = qt_ref.shape[0]
    qt_ref[0:ATTN_WIDTH] = (res[0:ATTN_WIDTH] * (HEAD_DIM_A ** -0.5 * LOG2_E)).astype(BF16)
    qt_ref[ATTN_WIDTH:nq] = res[ATTN_WIDTH:nq].astype(BF16)
    pad_rows = V_ROWS - HEAD_DIM_A
    ones_row = (lax.broadcasted_iota(jnp.int32, (pad_rows, KEY_CHUNK), 0) == 0).astype(BF16)
    for c in range(vt_ref.shape[0]):
        for g in range(N_KV_A):
            rows = res[nq + g * HEAD_DIM_A:nq + (g + 1) * HEAD_DIM_A, c * KEY_CHUNK:(c + 1) * KEY_CHUNK]
            vt_ref[c, g * V_ROWS:g * V_ROWS + HEAD_DIM_A] = rows.astype(BF16)
            vt_ref[c, g * V_ROWS + HEAD_DIM_A:(g + 1) * V_ROWS] = ones_row
    wit_ref[...] = res[nq + KV_WIDTH:nq + KV_WIDTH + IDX_HEADS]
    kn_ref[...] = jnp.dot(xn, wn_ref[...], preferred_element_type=F32).astype(BF16)


def _proj_attn(x, gain, w_t, w_n):
    m, k = x.shape
    tm = SMALL_ROW_TILE
    nq = ATTN_WIDTH + IDX_HEADS * IDX_DIM
    return pl.pallas_call(
        _proj_attn_kernel,
        grid=(m // tm,),
        in_specs=[pl.BlockSpec((tm, k), lambda i: (i, 0)),
                  pl.BlockSpec((1, k), lambda i: (0, 0)),
                  pl.BlockSpec(w_t.shape, lambda i: (0, 0)),
                  pl.BlockSpec(w_n.shape, lambda i: (0, 0))],
        out_specs=[pl.BlockSpec((nq, tm), lambda i: (0, i)),
                   pl.BlockSpec((tm // KEY_CHUNK, N_KV_A * V_ROWS, KEY_CHUNK), lambda i: (i, 0, 0)),
                   pl.BlockSpec((IDX_HEADS, tm), lambda i: (0, i)),
                   pl.BlockSpec((tm, w_n.shape[1]), lambda i: (i, 0))],
        out_shape=[jax.ShapeDtypeStruct((nq, m), BF16),
                   jax.ShapeDtypeStruct((m // KEY_CHUNK, N_KV_A * V_ROWS, KEY_CHUNK), BF16),
                   jax.ShapeDtypeStruct((IDX_HEADS, m), F32),
                   jax.ShapeDtypeStruct((m, w_n.shape[1]), BF16)],
        compiler_params=_params(1),
        name="proj_attn",
    )(x, gain.reshape(1, k), w_t, w_n)


def _matmul_residual_kernel(a_ref, w_ref, r_ref, o_ref):
    o_ref[...] = r_ref[...] + jnp.dot(a_ref[...], w_ref[...], preferred_element_type=F32)


def _matmul_residual(a, w, res, name):
    m, k = a.shape
    n = w.shape[1]
    tm, tn = ROW_TILE, COL_TILE
    return pl.pallas_call(
        _matmul_residual_kernel,
        grid=(m // tm, n // tn),
        in_specs=[pl.BlockSpec((tm, k), lambda i, j: (i, 0)),
                  pl.BlockSpec((k, tn), lambda i, j: (0, j)),
                  pl.BlockSpec((tm, tn), lambda i, j: (i, j))],
        out_specs=pl.BlockSpec((tm, tn), lambda i, j: (i, j)),
        out_shape=jax.ShapeDtypeStruct((m, n), F32),
        compiler_params=_params(2),
        name=name,
    )(a, w, res)


def _gated_merge_kernel(ya_ref, yb_ref, wa_ref, wb_ref, ga_ref, gb_ref, o_ref):
    za = jnp.dot(ya_ref[...], wa_ref[...], preferred_element_type=F32)
    zb = jnp.dot(yb_ref[...], wb_ref[...], preferred_element_type=F32)
    z = ga_ref[...].astype(F32) * za + gb_ref[...].astype(F32) * zb
    o_ref[...] = z.astype(o_ref.dtype)


def _gated_merge(ya, yb, wa, wb, gates):
    m = ya.shape[0]
    n = wa.shape[1]
    tm, tn = ROW_TILE, COL_TILE
    nj = n // tn
    return pl.pallas_call(
        _gated_merge_kernel,
        grid=(m // tm, nj),
        in_specs=[pl.BlockSpec((tm, ya.shape[1]), lambda i, j: (i, 0)),
                  pl.BlockSpec((tm, yb.shape[1]), lambda i, j: (i, 0)),
                  pl.BlockSpec((wa.shape[0], tn), lambda i, j: (0, j)),
                  pl.BlockSpec((wb.shape[0], tn), lambda i, j: (0, j)),
                  pl.BlockSpec((tm, tn), lambda i, j: (i, j)),
                  pl.BlockSpec((tm, tn), lambda i, j: (i, j + nj))],
        out_specs=pl.BlockSpec((tm, tn), lambda i, j: (i, j)),
        out_shape=jax.ShapeDtypeStruct((m, n), BF16),
        compiler_params=_params(2),
        name="gated_merge",
    )(ya, yb, wa, wb, gates, gates)


def _t5_bucket(n):
    max_exact = REL_BUCKETS // 2
    n = jnp.maximum(n, 0)
    nf = jnp.maximum(n, 1).astype(F32)
    large = max_exact + (jnp.log(nf / max_exact) / math.log(REL_MAX_DIST / max_exact)
                         * (REL_BUCKETS - max_exact)).astype(jnp.int32)
    large = jnp.minimum(large, REL_BUCKETS - 1)
    return jnp.where(n < max_exact, n, large)


def _bias_tiles_kernel(rb_ref, o_ref):
    shape = (KEY_CHUNK, BLOCK_Q)
    key = lax.broadcasted_iota(jnp.int32, shape, 0)
    qry = lax.broadcasted_iota(jnp.int32, shape, 1)
    for o in range(N_BIAS_TILES):
        bucket = _t5_bucket(o * BLOCK_Q + qry - key)
        for h in range(N_HEADS_A):
            def pick(b, t, bucket=bucket, h=h):
                return jnp.where(bucket == b, rb_ref[b, h], t)
            tile = lax.fori_loop(0, REL_BUCKETS, pick, jnp.zeros(shape, F32))
            j = h % GRP
            o_ref[o, h // GRP, :, j * BLOCK_Q:(j + 1) * BLOCK_Q] = (tile - rb_ref[REL_BUCKETS - 1, h]) * LOG2_E


def _bias_tiles(rel_bias):
    return pl.pallas_call(
        _bias_tiles_kernel,
        in_specs=[pl.BlockSpec(memory_space=pltpu.SMEM)],
        out_specs=pl.BlockSpec(memory_space=pltpu.VMEM),
        out_shape=jax.ShapeDtypeStruct((N_BIAS_TILES, N_KV_A, KEY_CHUNK, GRP_LANES), F32),
        name="rel_bias_tiles",
    )(rel_bias)


def _sparse_attn_kernel(bias_ref, qt_ref, qit_ref, wit_ref, kw_ref, k_ref, vt_ref, o_ref,
                        qz_ref, qiz_ref, sc_ref, prod_ref, m_ref, l_ref, acc_ref):
    i = pl.program_id(1)
    t0 = i * BLOCK_Q
    n_chunks = (i + 2) // 2
    tile = (KEY_CHUNK, BLOCK_Q)
    row1 = (1, BLOCK_Q)
    k_sel = float(TOPK_MAX)

    @pl.when((pl.program_id(0) == 0) & (i == 0))
    def _():
        qiz_ref[...] = jnp.zeros(qiz_ref.shape, BF16)

    for h in range(IDX_HEADS):
        qiz_ref[0:IDX_DIM, h * BLOCK_Q:(h + 1) * BLOCK_Q] = qit_ref[h * IDX_DIM:(h + 1) * IDX_DIM, :]
    for h in range(N_HEADS_A):
        qz_ref[:, h * BLOCK_Q:(h + 1) * BLOCK_Q] = qt_ref[h * HEAD_DIM_A:(h + 1) * HEAD_DIM_A, :]
    w_rows = wit_ref[...] * ((IDX_HEADS * IDX_DIM) ** -0.5)

    key_row = lax.broadcasted_iota(jnp.int32, tile, 0)
    q_pos = t0 + lax.broadcasted_iota(jnp.int32, tile, 1)

    last = n_chunks - 1
    n_pairs = (n_chunks + 1) // 2

    def idx_products(kc, slot):
        ks = pl.multiple_of(jnp.minimum(kc, last) * KEY_CHUNK, KEY_CHUNK)
        prod_ref[slot] = jnp.dot(kw_ref[pl.ds(ks, KEY_CHUNK), :], qiz_ref[...], preferred_element_type=F32)

    def reduce_scores(kc, slot, carry):
        rmax, rmin = carry
        kc = jnp.minimum(kc, last)
        s = jnp.zeros(tile, F32)
        for h in range(IDX_HEADS):
            d = prod_ref[slot, :, h * BLOCK_Q:(h + 1) * BLOCK_Q]
            s = s + jnp.maximum(d, 0.0) * w_rows[h:h + 1, :]
        causal = (kc * KEY_CHUNK + key_row) <= q_pos
        sc_ref[kc] = jnp.where(causal, s, -jnp.inf)
        rmax = jnp.maximum(rmax, jnp.max(jnp.where(causal, s, -jnp.inf), axis=0, keepdims=True))
        rmin = jnp.minimum(rmin, jnp.min(jnp.where(causal, s, jnp.inf), axis=0, keepdims=True))
        return rmax, rmin

    def score_pair(j, carry):
        idx_products(2 * j + 1, 1)
        carry = reduce_scores(2 * j, 0, carry)
        idx_products(2 * j + 2, 0)
        return reduce_scores(2 * j + 1, 1, carry)

    idx_products(0, 0)
    rmax, rmin = lax.fori_loop(0, n_pairs, score_pair,
                               (jnp.full(row1, -jnp.inf, F32), jnp.full(row1, jnp.inf, F32)))

    n_causal = (t0 + lax.broadcasted_iota(jnp.int32, row1, 1) + 1).astype(F32)
    short = n_causal <= k_sel

    def count_ge(t):
        def body(kc, c):
            hit = jnp.where(sc_ref[kc] >= t, 1.0, 0.0)
            return c + jnp.sum(hit.reshape(KEY_CHUNK // COUNT_ROWS, COUNT_ROWS, BLOCK_Q), axis=0)
        c = lax.fori_loop(0, n_chunks, body, jnp.zeros((COUNT_ROWS, BLOCK_Q), F32))
        return jnp.sum(c, axis=0, keepdims=True)

    def halve(_, carry):
        lo, hi, c_lo, _ = carry
        mid = lo + (hi - lo) * 0.5
        cnt = count_ge(mid)
        enough = cnt >= k_sel
        shrinks = jnp.where((mid > lo) & (mid < hi), 1.0, 0.0)
        return jnp.where(enough, mid, lo), jnp.where(enough, hi, mid), jnp.where(enough, cnt, c_lo), shrinks

    def pending(carry):
        rounds, todo, _, _, _ = carry
        return (rounds < MAX_BISECT_ITERS // BISECT_ROUND) & (todo > 0.0)

    def bisect_round(carry):
        rounds, _, lo, hi, c_lo = carry
        lo, hi, c_lo, shrinks = lax.fori_loop(0, BISECT_ROUND, halve, (lo, hi, c_lo, jnp.ones(row1, F32)))
        settled = short | (c_lo == k_sel) | (shrinks == 0.0)
        return rounds + 1, jnp.sum(jnp.where(settled, 0.0, 1.0)), lo, hi, c_lo

    _, _, lo, _, _ = lax.while_loop(pending, bisect_round,
                                    (jnp.int32(0), jnp.float32(1.0), rmin, rmax, n_causal))
    thr = jnp.where(short, jnp.finfo(F32).min, lo)

    m_ref[...] = jnp.full(m_ref.shape, -1e30, F32)
    l_ref[...] = jnp.zeros(l_ref.shape, F32)
    acc_ref[...] = jnp.zeros(acc_ref.shape, F32)

    def qk_products(kc, slot):
        ks = pl.multiple_of(jnp.minimum(kc, last) * KEY_CHUNK, KEY_CHUNK)
        for g in range(N_KV_A):
            k_c = k_ref[pl.ds(ks, KEY_CHUNK), g * HEAD_DIM_A:(g + 1) * HEAD_DIM_A]
            prod_ref[slot, :, g * GRP_LANES:(g + 1) * GRP_LANES] = jnp.dot(
                k_c, qz_ref[:, g * GRP_LANES:(g + 1) * GRP_LANES], preferred_element_type=F32)

    def softmax_pv(kc, slot, near):
        live_thr = jnp.where(kc <= last, thr, jnp.inf) if near else thr
        kc = jnp.minimum(kc, last) if near else kc
        mask = jnp.where(sc_ref[kc] >= live_thr, 0.0, -jnp.inf)
        mask = jnp.concatenate([mask] * GRP, axis=1)
        for g in range(N_KV_A):
            lg = prod_ref[slot, :, g * GRP_LANES:(g + 1) * GRP_LANES]
            if near:
                lg = lg + bias_ref[jnp.minimum(i - 2 * kc, N_BIAS_TILES - 1), g]
            lg = lg + mask
            m_old = m_ref[g]
            m_new = jnp.maximum(m_old, jnp.max(lg, axis=0, keepdims=True))
            alpha = jnp.exp2(m_old - m_new)
            p = jnp.exp2(lg - m_new)
            pv = jnp.dot(vt_ref[kc, g * V_ROWS:(g + 1) * V_ROWS, :], p.astype(BF16),
                         preferred_element_type=F32)
            l_ref[g] = alpha * l_ref[g] + pv[HEAD_DIM_A:HEAD_DIM_A + 1]
            acc_ref[g] = alpha * acc_ref[g] + pv[0:HEAD_DIM_A]
            m_ref[g] = m_new

    def attend(near):
        def body(j, carry):
            qk_products(2 * j + 1, 1)
            softmax_pv(2 * j, 0, near)
            qk_products(2 * j + 2, 0)
            softmax_pv(2 * j + 1, 1, near)
            return carry
        return body

    n_far = jnp.maximum(n_pairs - 2, 0)
    qk_products(0, 0)
    lax.fori_loop(0, n_far, attend(False), 0)
    lax.fori_loop(n_far, n_pairs, attend(True), 0)

    for g in range(N_KV_A):
        out_t = acc_ref[g] / l_ref[g]
        for j in range(GRP):
            h = g * GRP + j
            o_ref[:, h * HEAD_DIM_A:(h + 1) * HEAD_DIM_A] = (
                out_t[:, j * BLOCK_Q:(j + 1) * BLOCK_Q].T.astype(o_ref.dtype))


def _sparse_attention(bias_tiles, q_t, v_t, wi_t, k_n, batch, seq):
    nb = seq // BLOCK_Q
    n_seq_chunks = seq // KEY_CHUNK
    kw_blk = KV_WIDTH // V7X_LANES
    return pl.pallas_call(
        _sparse_attn_kernel,
        grid=(batch, nb),
        in_specs=[pl.BlockSpec(bias_tiles.shape, lambda b, i: (0, 0, 0, 0)),
                  pl.BlockSpec((ATTN_WIDTH, BLOCK_Q), lambda b, i: (0, b * nb + i)),
                  pl.BlockSpec((IDX_HEADS * IDX_DIM, BLOCK_Q), lambda b, i: (1, b * nb + i)),
                  pl.BlockSpec((IDX_HEADS, BLOCK_Q), lambda b, i: (0, b * nb + i)),
                  pl.BlockSpec((seq, V7X_LANES), lambda b, i: (b, kw_blk)),
                  pl.BlockSpec((seq, KV_WIDTH), lambda b, i: (b, 0)),
                  pl.BlockSpec((n_seq_chunks, N_KV_A * V_ROWS, KEY_CHUNK), lambda b, i: (b, 0, 0))],
        out_specs=pl.BlockSpec((BLOCK_Q, ATTN_WIDTH), lambda b, i: (b * nb + i, 0)),
        out_shape=jax.ShapeDtypeStruct((batch * seq, ATTN_WIDTH), BF16),
        scratch_shapes=[pltpu.VMEM((HEAD_DIM_A, N_HEADS_A * BLOCK_Q), BF16),
                        pltpu.VMEM((V7X_LANES, IDX_HEADS * BLOCK_Q), BF16),
                        pltpu.VMEM((n_seq_chunks, KEY_CHUNK, BLOCK_Q), F32),
                        pltpu.VMEM((2, KEY_CHUNK, IDX_HEADS * BLOCK_Q), F32),
                        pltpu.VMEM((N_KV_A, 1, GRP_LANES), F32),
                        pltpu.VMEM((N_KV_A, 1, GRP_LANES), F32),
                        pltpu.VMEM((N_KV_A, HEAD_DIM_A, GRP_LANES), F32)],
        compiler_params=_params(2),
        name="sparse_attention",
    )(bias_tiles, q_t, q_t, wi_t, k_n, k_n, v_t)


def _s5_input_kernel(u_ref, lr_ref, li_ref, ldt_ref, bre_ref, bim_ref,
                     bur_ref, bui_ref, ar_ref, ai_ref, bbr_ref, bbi_ref):
    @pl.when(pl.program_id(0) == 0)
    def _():
        dt = jnp.exp(ldt_ref[...])
        lr = lr_ref[...]
        li = li_ref[...]
        mag = jnp.exp(lr * dt)
        ar = mag * jnp.cos(li * dt)
        ai = mag * jnp.sin(li * dt)
        den = lr * lr + li * li
        nr = ar - 1.0
        cr = (nr * lr + ai * li) / den
        ci = (ai * lr - nr * li) / den
        bre = bre_ref[...]
        bim = bim_ref[...]
        bbr_ref[...] = (cr * bre - ci * bim).astype(BF16)
        bbi_ref[...] = (cr * bim + ci * bre).astype(BF16)
        ar_ref[...] = ar
        ai_ref[...] = ai

    u = u_ref[...].astype(BF16)
    bur_ref[...] = jnp.dot(u, bbr_ref[...], preferred_element_type=F32)
    bui_ref[...] = jnp.dot(u, bbi_ref[...], preferred_element_type=F32)


def _s5_input(us, lam_re, lam_im, log_dt, bre_bd, bim_bd):
    m = us.shape[0]
    tm = SMALL_ROW_TILE
    row = pl.BlockSpec((1, SSM_STATES), lambda i: (0, 0))
    mat = pl.BlockSpec((SSM_WIDTH, SSM_STATES), lambda i: (0, 0))
    out = pl.BlockSpec((tm, SSM_STATES), lambda i: (i, 0))
    return pl.pallas_call(
        _s5_input_kernel,
        grid=(m // tm,),
        in_specs=[pl.BlockSpec((tm, SSM_WIDTH), lambda i: (i, 0)), row, row, row, mat, mat],
        out_specs=[out, out, row, row],
        out_shape=[jax.ShapeDtypeStruct((m, SSM_STATES), F32)] * 2
        + [jax.ShapeDtypeStruct((1, SSM_STATES), F32)] * 2,
        scratch_shapes=[pltpu.VMEM((SSM_WIDTH, SSM_STATES), BF16)] * 2,
        compiler_params=_params(1),
        name="s5_input",
    )(us, lam_re, lam_im, log_dt, bre_bd, bim_bd)


def _s5_scan_kernel(ar_ref, ai_ref, br_ref, bi_ref, xr_ref, xi_ref, sr_ref, si_ref):
    @pl.when(pl.program_id(0) == 0)
    def _():
        sr_ref[...] = jnp.zeros(sr_ref.shape, F32)
        si_ref[...] = jnp.zeros(si_ref.shape, F32)

    ar = ar_ref[...]
    ai = ai_ref[...]
    steps = br_ref.shape[1]

    def step(t, carry):
        xr, xi = carry
        nr = ar * xr - ai * xi + br_ref[:, t]
        ni = ar * xi + ai * xr + bi_ref[:, t]
        xr_ref[:, t] = nr.astype(xr_ref.dtype)
        xi_ref[:, t] = ni.astype(xi_ref.dtype)
        return nr, ni

    xr, xi = lax.fori_loop(0, steps, step, (sr_ref[...], si_ref[...]), unroll=8)
    sr_ref[...] = xr
    si_ref[...] = xi


def _s5_scan(a_re, a_im, bu_re, bu_im):
    batch, seq, rows, lanes = bu_re.shape
    blk = pl.BlockSpec((batch, SCAN_CHUNK, rows, lanes), lambda c: (0, c, 0, 0))
    par = pl.BlockSpec((rows, lanes), lambda c: (0, 0))
    return pl.pallas_call(
        _s5_scan_kernel,
        grid=(seq // SCAN_CHUNK,),
        in_specs=[par, par, blk, blk],
        out_specs=[blk, blk],
        out_shape=[jax.ShapeDtypeStruct(bu_re.shape, BF16)] * 2,
        scratch_shapes=[pltpu.VMEM((batch, rows, lanes), F32)] * 2,
        compiler_params=_params(1),
        name="s5_scan",
    )(a_re, a_im, bu_re, bu_im)


def _s5_output_kernel(xr_ref, xi_ref, u_ref, cr_ref, ci_ref, d_ref, wg_ref, bg_ref, o_ref):
    y = (jnp.dot(xr_ref[...], cr_ref[...], preferred_element_type=F32)
         - jnp.dot(xi_ref[...], ci_ref[...], preferred_element_type=F32)
         + d_ref[...] * u_ref[...])
    g = jax.nn.gelu(y)
    gate = jax.nn.sigmoid(jnp.dot(g.astype(BF16), wg_ref[...], preferred_element_type=F32) + bg_ref[...])
    o_ref[...] = (g * gate).astype(o_ref.dtype)


def _s5_output(xr, xi, us, cre_bd, cim_bd, d_row, w_glu, b_glu):
    m = us.shape[0]
    tm = SMALL_ROW_TILE
    state = pl.BlockSpec((tm, SSM_STATES), lambda i: (i, 0))
    chan = pl.BlockSpec((tm, SSM_WIDTH), lambda i: (i, 0))
    cmat = pl.BlockSpec((SSM_STATES, SSM_WIDTH), lambda i: (0, 0))
    row = pl.BlockSpec((1, SSM_WIDTH), lambda i: (0, 0))
    return pl.pallas_call(
        _s5_output_kernel,
        grid=(m // tm,),
        in_specs=[state, state, chan, cmat, cmat, row,
                  pl.BlockSpec((SSM_WIDTH, SSM_WIDTH), lambda i: (0, 0)), row],
        out_specs=chan,
        out_shape=jax.ShapeDtypeStruct((m, SSM_WIDTH), BF16),
        compiler_params=_params(1),
        name="s5_output",
    )(xr, xi, us, cre_bd, cim_bd, d_row, w_glu, b_glu)


def _block_diag(blocks):
    g, r, c = blocks.shape
    eye = jnp.eye(g, dtype=blocks.dtype)
    return (eye[:, None, :, None] * blocks[:, :, None, :]).reshape(g * r, g * c)


def _cross_attn_kernel(q_ref, k_ref, v_ref, o_ref):
    scale = X_HEAD_DIM ** -0.5
    for h in range(X_HEADS):
        cols = slice(h * X_HEAD_DIM, (h + 1) * X_HEAD_DIM)
        lg = lax.dot_general(q_ref[:, cols], k_ref[:, cols], _NT, preferred_element_type=F32) * scale
        p = jnp.exp(lg - jnp.max(lg, axis=-1, keepdims=True))
        o = jnp.dot(p.astype(BF16), v_ref[:, cols], preferred_element_type=F32)
        o_ref[:, cols] = (o / jnp.sum(p, axis=-1, keepdims=True)).astype(o_ref.dtype)


def _cross_attn(q, kv, batch, seq, mem_len):
    width = X_HEADS * X_HEAD_DIM
    tq = SMALL_ROW_TILE
    nt = seq // tq
    return pl.pallas_call(
        _cross_attn_kernel,
        grid=(batch, nt),
        in_specs=[pl.BlockSpec((tq, width), lambda b, i: (b * nt + i, 0)),
                  pl.BlockSpec((mem_len, width), lambda b, i: (b, 0)),
                  pl.BlockSpec((mem_len, width), lambda b, i: (b, 1))],
        out_specs=pl.BlockSpec((tq, width), lambda b, i: (b * nt + i, 0)),
        out_shape=jax.ShapeDtypeStruct((batch * seq, width), BF16),
        compiler_params=_params(2),
        name="cross_attention",
    )(q, kv, kv)


def _conv_ffn_kernel(x_ref, g_ref, wa_ref, wb_ref, cwa_ref, cwb_ref, cba_ref, cbb_ref, wd_ref, gf_ref,
                     o_ref, xn_ref, ha_ref, hb_ref, ta_ref, tb_ref, *, tiles_per_seq):
    i = pl.program_id(0)
    j = pl.program_id(1)
    tm = x_ref.shape[0]
    halo = V7X_SUBLANES

    @pl.when(j == 0)
    def _():
        xn_ref[...] = _rms(x_ref[...], g_ref[...], EPS).astype(BF16)
        o_ref[...] = jnp.zeros(o_ref.shape, F32)

    @pl.when(i % tiles_per_seq == 0)
    def _():
        ha_ref[0:halo] = jnp.zeros((halo, ha_ref.shape[1]), F32)
        hb_ref[0:halo] = jnp.zeros((halo, hb_ref.shape[1]), F32)

    @pl.when(i % tiles_per_seq != 0)
    def _():
        ha_ref[0:halo] = ta_ref[j]
        hb_ref[0:halo] = tb_ref[j]

    rows = tm // FFN_ROW_SLABS
    for s in range(FFN_ROW_SLABS):
        xn = xn_ref[s * rows:(s + 1) * rows]
        ha_ref[halo + s * rows:halo + (s + 1) * rows] = jnp.dot(xn, wa_ref[...], preferred_element_type=F32)
        hb_ref[halo + s * rows:halo + (s + 1) * rows] = jnp.dot(xn, wb_ref[...], preferred_element_type=F32)
    ta_ref[j] = ha_ref[tm:tm + halo]
    tb_ref[j] = hb_ref[tm:tm + halo]

    def conv(h_ref, cw_ref, cb_ref, r0):
        out = cb_ref[...]
        for tap in range(CONV_WIDTH):
            start = r0 + halo - (CONV_WIDTH - 1) + tap
            out = out + h_ref[start:start + rows] * cw_ref[tap:tap + 1]
        return out

    for s in range(FFN_ROW_SLABS):
        a = conv(ha_ref, cwa_ref, cba_ref, s * rows)
        b = conv(hb_ref, cwb_ref, cbb_ref, s * rows)
        act = (a * jax.nn.sigmoid(a) * b).astype(BF16)
        o_ref[s * rows:(s + 1) * rows] += jnp.dot(act, wd_ref[...], preferred_element_type=F32)

    @pl.when(j == pl.num_programs(1) - 1)
    def _():
        o_ref[...] = _rms(x_ref[...] + o_ref[...], gf_ref[...], EPS)


def _conv_ffn(x, gain, w_up, conv_w, conv_b, w_down, gain_final, seq):
    m, d = x.shape
    d_ff = w_down.shape[0]
    tm, tf = FFN_ROW_TILE, COL_TILE
    nj = d_ff // tf
    halo = V7X_SUBLANES
    return pl.pallas_call(
        functools.partial(_conv_ffn_kernel, tiles_per_seq=seq // tm),
        grid=(m // tm, nj),
        in_specs=[pl.BlockSpec((tm, d), lambda i, j: (i, 0)),
                  pl.BlockSpec((1, d), lambda i, j: (0, 0)),
                  pl.BlockSpec((d, tf), lambda i, j: (0, j)),
                  pl.BlockSpec((d, tf), lambda i, j: (0, j + nj)),
                  pl.BlockSpec((CONV_WIDTH, tf), lambda i, j: (0, j)),
                  pl.BlockSpec((CONV_WIDTH, tf), lambda i, j: (0, j + nj)),
                  pl.BlockSpec((1, tf), lambda i, j: (0, j)),
                  pl.BlockSpec((1, tf), lambda i, j: (0, j + nj)),
                  pl.BlockSpec((tf, d), lambda i, j: (j, 0)),
                  pl.BlockSpec((1, d), lambda i, j: (0, 0))],
        out_specs=pl.BlockSpec((tm, d), lambda i, j: (i, 0)),
        out_shape=jax.ShapeDtypeStruct((m, d), F32),
        scratch_shapes=[pltpu.VMEM((tm, d), BF16),
                        pltpu.VMEM((tm + halo, tf), F32),
                        pltpu.VMEM((tm + halo, tf), F32),
                        pltpu.VMEM((nj, halo, tf), F32),
                        pltpu.VMEM((nj, halo, tf), F32)],
        compiler_params=_params(2),
        name="conv_ffn",
    )(x, gain.reshape(1, d), w_up, w_up, conv_w, conv_w, conv_b.reshape(1, -1), conv_b.reshape(1, -1),
      w_down, gain_final.reshape(1, d))


def kernel(x, mem, rel_bias, norm_mix, w_in, ssm_lambda_re, ssm_lambda_im, ssm_log_dt, ssm_b_re, ssm_b_im, ssm_c_re, ssm_c_im, ssm_d, ssm_w_glu, ssm_b_glu, w_branch_a, w_branch_b, w_out, norm_cross, norm_mem, w_cross_q, w_cross_kv, w_cross_o, norm_ffn, w_up, ffn_conv_w, ffn_conv_b, w_down, norm_final):
    batch, seq, d_model = x.shape
    mem_len = mem.shape[1]
    depth = w_in.shape[0]
    assert depth == 1, "the final rmsnorm is fused into the last layer's ConvFFN kernel"
    assert seq % ROW_TILE == 0 and seq % KEY_CHUNK == 0 and seq >= 4 * TOPK_MAX
    m = batch * seq
    xf = x.reshape(m, d_model)
    memf = mem.reshape(batch * mem_len, d_model)
    bias_tiles = _bias_tiles(rel_bias)

    splits = (ATTN_WIDTH, KV_WIDTH, KV_WIDTH, IDX_HEADS * IDX_DIM, IDX_DIM, IDX_HEADS, SSM_WIDTH, d_model, d_model)
    offs = [int(o) for o in np.cumsum((0,) + splits)]
    for l in range(depth):
        w = w_in[l]
        wq, wk, wv, wqi, wki, wwi, wss, wga, wgb = [w[:, offs[n]:offs[n + 1]] for n in range(len(splits))]
        pad = jnp.zeros((d_model, V7X_LANES - IDX_DIM - IDX_HEADS), w.dtype)
        w_t = jnp.concatenate([wq, wqi, wv, wwi], axis=1).T.astype(BF16)
        w_n = jnp.concatenate([wk, wki, wwi, pad], axis=1).astype(BF16)
        w_gate = jnp.concatenate([wga, wgb], axis=1).astype(BF16)

        q_t, v_t, wi_t, k_n = _proj_attn(xf, norm_mix[l], w_t, w_n)
        us = _norm_matmul(xf, norm_mix[l], wss.astype(BF16), F32, COL_TILE, "proj_ssm")
        gates = _norm_matmul(xf, norm_mix[l], w_gate, BF16, COL_TILE, "proj_gates", sigmoid=True)

        y_a = _sparse_attention(bias_tiles, q_t, v_t, wi_t, k_n, batch, seq)

        bre_bd = _block_diag(jnp.transpose(ssm_b_re[l], (0, 2, 1)))
        bim_bd = _block_diag(jnp.transpose(ssm_b_im[l], (0, 2, 1)))
        cre_bd = _block_diag(jnp.transpose(ssm_c_re[l], (0, 2, 1))).astype(BF16)
        cim_bd = _block_diag(jnp.transpose(ssm_c_im[l], (0, 2, 1))).astype(BF16)
        log_dt = jnp.broadcast_to(ssm_log_dt[l][:, None], (SSM_GROUPS, SSM_STATE)).reshape(1, SSM_STATES)
        bu_re, bu_im, a_re, a_im = _s5_input(us, ssm_lambda_re[l].reshape(1, SSM_STATES),
                                             ssm_lambda_im[l].reshape(1, SSM_STATES), log_dt, bre_bd, bim_bd)
        state_rows = SSM_STATES // V7X_LANES
        sshape = (batch, seq, state_rows, V7X_LANES)
        xs_re, xs_im = _s5_scan(a_re.reshape(state_rows, V7X_LANES), a_im.reshape(state_rows, V7X_LANES),
                                bu_re.reshape(sshape), bu_im.reshape(sshape))
        y_b = _s5_output(xs_re.reshape(m, SSM_STATES), xs_im.reshape(m, SSM_STATES), us, cre_bd, cim_bd,
                         ssm_d[l].reshape(1, SSM_WIDTH), ssm_w_glu[l].astype(BF16),
                         ssm_b_glu[l].reshape(1, SSM_WIDTH))

        z = _gated_merge(y_a, y_b, w_branch_a[l].astype(BF16), w_branch_b[l].astype(BF16), gates)
        xf = _matmul_residual(z, w_out[l].astype(BF16), xf, "out_proj")

        qx = _norm_matmul(xf, norm_cross[l], w_cross_q[l].astype(BF16), BF16, COL_TILE, "cross_q")
        kvx = _norm_matmul(memf, norm_mem[l], w_cross_kv[l].astype(BF16), BF16, COL_TILE, "cross_kv")
        ox = _cross_attn(qx, kvx, batch, seq, mem_len)
        xf = _matmul_residual(ox, w_cross_o[l].astype(BF16), xf, "cross_out")

        xf = _conv_ffn(xf, norm_ffn[l], w_up[l].astype(BF16), ffn_conv_w[l], ffn_conv_b[l],
                       w_down[l].astype(BF16), norm_final, seq)
    return xf.reshape(batch, seq, d_model)
```

```python
import functools
import math

import jax
import jax.numpy as jnp
import numpy as np
from jax import lax
from jax.experimental import pallas as pl
from jax.experimental.pallas import tpu as pltpu

N_HEADS_A = 8
N_KV_A = 2
HEAD_DIM_A = 128
ATTN_WIDTH = N_HEADS_A * HEAD_DIM_A
KV_WIDTH = N_KV_A * HEAD_DIM_A
IDX_HEADS = 16
IDX_DIM = 64
TOPK_MAX = 256
BLOCK_Q = 128
REL_BUCKETS = 32
REL_MAX_DIST = 128
SSM_GROUP = 16
SSM_GROUPS = 32
SSM_STATE = 64
SSM_WIDTH = SSM_GROUP * SSM_GROUPS
SSM_STATES = SSM_GROUPS * SSM_STATE
X_HEADS = 4
X_HEAD_DIM = 128
CONV_WIDTH = 3
EPS = 1e-6

V7X_LANES = 128
V7X_SUBLANES = 8
V7X_VMEM_BYTES = 64 * 1024 * 1024
VMEM_LIMIT_BYTES = V7X_VMEM_BYTES - 8 * 1024 * 1024

GRP = N_HEADS_A // N_KV_A
GRP_LANES = GRP * BLOCK_Q
V_ROWS = HEAD_DIM_A + 16
KEY_CHUNK = 2 * BLOCK_Q
N_BIAS_TILES = 4
MAX_BISECT_ITERS = 64
BISECT_ROUND = 4
COUNT_ROWS = 64
LOG2_E = math.log2(math.e)
ROW_TILE = 1024
SMALL_ROW_TILE = 512
COL_TILE = 512
FFN_ROW_TILE = 512
FFN_ROW_SLABS = 2
SCAN_CHUNK = 128

F32 = jnp.float32
BF16 = jnp.bfloat16
_NT = (((1,), (1,)), ((), ()))


def _params(n_axes):
    return pltpu.CompilerParams(dimension_semantics=("arbitrary",) * n_axes,
                                vmem_limit_bytes=VMEM_LIMIT_BYTES)


def _rms(x, gain, eps):
    return x * lax.rsqrt(jnp.mean(x * x, axis=-1, keepdims=True) + eps) * gain


def _norm_matmul_kernel(x_ref, g_ref, w_ref, o_ref, xn_ref, *, sigmoid):
    @pl.when(pl.program_id(1) == 0)
    def _():
        xn_ref[...] = _rms(x_ref[...], g_ref[...], EPS).astype(BF16)

    tm = o_ref.shape[0]
    rows = min(tm, SMALL_ROW_TILE) if sigmoid else tm
    for r in range(0, tm, rows):
        y = jnp.dot(xn_ref[r:r + rows], w_ref[...], preferred_element_type=F32)
        if sigmoid:
            y = 0.5 * jnp.tanh(0.5 * y) + 0.5
        o_ref[r:r + rows] = y.astype(o_ref.dtype)


def _norm_matmul(x, gain, w, out_dtype, tn, name, sigmoid=False):
    m, k = x.shape
    n = w.shape[1]
    tm = min(ROW_TILE, m)
    return pl.pallas_call(
        functools.partial(_norm_matmul_kernel, sigmoid=sigmoid),
        grid=(m // tm, n // tn),
        in_specs=[pl.BlockSpec((tm, k), lambda i, j: (i, 0)),
                  pl.BlockSpec((1, k), lambda i, j: (0, 0)),
                  pl.BlockSpec((k, tn), lambda i, j: (0, j))],
        out_specs=pl.BlockSpec((tm, tn), lambda i, j: (i, j)),
        out_shape=jax.ShapeDtypeStruct((m, n), out_dtype),
        scratch_shapes=[pltpu.VMEM((tm, k), BF16)],
        compiler_params=_params(2),
        name=name,
    )(x, gain.reshape(1, k), w)


def _proj_attn_kernel(x_ref, g_ref, wt_ref, wn_ref, qt_ref, vt_ref, wit_ref, kn_ref):
    xn = _rms(x_ref[...], g_ref[...], EPS).astype(BF16)
    res = lax.dot_general(wt_ref[...], xn, _NT, preferred_element_type=F32)
    nq = qt_ref.shape[0]
    qt_ref[0:ATTN_WIDTH] = (res[0:ATTN_WIDTH] * (HEAD_DIM_A ** -0.5 * LOG2_E)).astype(BF16)
    qt_ref[ATTN_WIDTH:nq] = res[ATTN_WIDTH:nq].astype(BF16)
    pad_rows = V_ROWS - HEAD_DIM_A
    ones_row = (lax.broadcasted_iota(jnp.int32, (pad_rows, KEY_CHUNK), 0) == 0).astype(BF16)
    for c in range(vt_ref.shape[0]):
        for g in range(N_KV_A):
            rows = res[nq + g * HEAD_DIM_A:nq + (g + 1) * HEAD_DIM_A, c * KEY_CHUNK:(c + 1) * KEY_CHUNK]
            vt_ref[c, g * V_ROWS:g * V_ROWS + HEAD_DIM_A] = rows.astype(BF16)
            vt_ref[c, g * V_ROWS + HEAD_DIM_A:(g + 1) * V_ROWS] = ones_row
    wit_ref[...] = res[nq + KV_WIDTH:nq + KV_WIDTH + IDX_HEADS]
    kn_ref[...] = jnp.dot(xn, wn_ref[...], preferred_element_type=F32).astype(BF16)


def _proj_attn(x, gain, w_t, w_n):
    m, k = x.shape
    tm = SMALL_ROW_TILE
    nq = ATTN_WIDTH + IDX_HEADS * IDX_DIM
    return pl.pallas_call(
        _proj_attn_kernel,
        grid=(m // tm,),
        in_specs=[pl.BlockSpec((tm, k), lambda i: (i, 0)),
                  pl.BlockSpec((1, k), lambda i: (0, 0)),
                  pl.BlockSpec(w_t.shape, lambda i: (0, 0)),
                  pl.BlockSpec(w_n.shape, lambda i: (0, 0))],
        out_specs=[pl.BlockSpec((nq, tm), lambda i: (0, i)),
                   pl.BlockSpec((tm // KEY_CHUNK, N_KV_A * V_ROWS, KEY_CHUNK), lambda i: (i, 0, 0)),
                   pl.BlockSpec((IDX_HEADS, tm), lambda i: (0, i)),
                   pl.BlockSpec((tm, w_n.shape[1]), lambda i: (i, 0))],
        out_shape=[jax.ShapeDtypeStruct((nq, m), BF16),
                   jax.ShapeDtypeStruct((m // KEY_CHUNK, N_KV_A * V_ROWS, KEY_CHUNK), BF16),
                   jax.ShapeDtypeStruct((IDX_HEADS, m), F32),
                   jax.ShapeDtypeStruct((m, w_n.shape[1]), BF16)],
        compiler_params=_params(1),
        name="proj_attn",
    )(x, gain.reshape(1, k), w_t, w_n)


def _t5_bucket(n):
    max_exact = REL_BUCKETS // 2
    n = jnp.maximum(n, 0)
    nf = jnp.maximum(n, 1).astype(F32)
    large = max_exact + (jnp.log(nf / max_exact) / math.log(REL_MAX_DIST / max_exact)
                         * (REL_BUCKETS - max_exact)).astype(jnp.int32)
    large = jnp.minimum(large, REL_BUCKETS - 1)
    return jnp.where(n < max_exact, n, large)


def _bias_tiles_kernel(rb_ref, o_ref):
    shape = (KEY_CHUNK, BLOCK_Q)
    key = lax.broadcasted_iota(jnp.int32, shape, 0)
    qry = lax.broadcasted_iota(jnp.int32, shape, 1)
    for o in range(N_BIAS_TILES):
        bucket = _t5_bucket(o * BLOCK_Q + qry - key)
        for h in range(N_HEADS_A):
            def pick(b, t, bucket=bucket, h=h):
                return jnp.where(bucket == b, rb_ref[b, h], t)
            tile = lax.fori_loop(0, REL_BUCKETS, pick, jnp.zeros(shape, F32))
            j = h % GRP
            o_ref[o, h // GRP, :, j * BLOCK_Q:(j + 1) * BLOCK_Q] = (tile - rb_ref[REL_BUCKETS - 1, h]) * LOG2_E


def _bias_tiles(rel_bias):
    return pl.pallas_call(
        _bias_tiles_kernel,
        in_specs=[pl.BlockSpec(memory_space=pltpu.SMEM)],
        out_specs=pl.BlockSpec(memory_space=pltpu.VMEM),
        out_shape=jax.ShapeDtypeStruct((N_BIAS_TILES, N_KV_A, KEY_CHUNK, GRP_LANES), F32),
        name="rel_bias_tiles",
    )(rel_bias)


def _sparse_attn_kernel(bias_ref, qt_ref, qit_ref, wit_ref, kw_ref, k_ref, vt_ref, o_ref,
                        qz_ref, qiz_ref, sc_ref, prod_ref, m_ref, l_ref, acc_ref):
    i = pl.program_id(1)
    t0 = i * BLOCK_Q
    n_chunks = (i + 2) // 2
    tile = (KEY_CHUNK, BLOCK_Q)
    row1 = (1, BLOCK_Q)
    k_sel = float(TOPK_MAX)

    @pl.when((pl.program_id(0) == 0) & (i == 0))
    def _():
        qiz_ref[...] = jnp.zeros(qiz_ref.shape, BF16)

    for h in range(IDX_HEADS):
        qiz_ref[0:IDX_DIM, h * BLOCK_Q:(h + 1) * BLOCK_Q] = qit_ref[h * IDX_DIM:(h + 1) * IDX_DIM, :]
    for h in range(N_HEADS_A):
        qz_ref[:, h * BLOCK_Q:(h + 1) * BLOCK_Q] = qt_ref[h * HEAD_DIM_A:(h + 1) * HEAD_DIM_A, :]
    w_rows = wit_ref[...] * ((IDX_HEADS * IDX_DIM) ** -0.5)

    key_row = lax.broadcasted_iota(jnp.int32, tile, 0)
    q_pos = t0 + lax.broadcasted_iota(jnp.int32, tile, 1)

    last = n_chunks - 1
    n_pairs = (n_chunks + 1) // 2

    def idx_products(kc, slot):
        ks = pl.multiple_of(jnp.minimum(kc, last) * KEY_CHUNK, KEY_CHUNK)
        prod_ref[slot] = jnp.dot(kw_ref[pl.ds(ks, KEY_CHUNK), :], qiz_ref[...], preferred_element_type=F32)

    def reduce_scores(kc, slot, carry):
        rmax, rmin = carry
        kc = jnp.minimum(kc, last)
        s = jnp.zeros(tile, F32)
        for h in range(IDX_HEADS):
            d = prod_ref[slot, :, h * BLOCK_Q:(h + 1) * BLOCK_Q]
            s = s + jnp.maximum(d, 0.0) * w_rows[h:h + 1, :]
        causal = (kc * KEY_CHUNK + key_row) <= q_pos
        sc_ref[kc] = jnp.where(causal, s, -jnp.inf)
        rmax = jnp.maximum(rmax, jnp.max(jnp.where(causal, s, -jnp.inf), axis=0, keepdims=True))
        rmin = jnp.minimum(rmin, jnp.min(jnp.where(causal, s, jnp.inf), axis=0, keepdims=True))
        return rmax, rmin

    def score_pair(j, carry):
        idx_products(2 * j + 1, 1)
        carry = reduce_scores(2 * j, 0, carry)
        idx_products(2 * j + 2, 0)
        return reduce_scores(2 * j + 1, 1, carry)

    idx_products(0, 0)
    rmax, rmin = lax.fori_loop(0, n_pairs, score_pair,
                               (jnp.full(row1, -jnp.inf, F32), jnp.full(row1, jnp.inf, F32)))

    n_causal = (t0 + lax.broadcasted_iota(jnp.int32, row1, 1) + 1).astype(F32)
    short = n_causal <= k_sel

    def count_ge(t):
        def body(kc, c):
            hit = jnp.where(sc_ref[kc] >= t, 1.0, 0.0)
            return c + jnp.sum(hit.reshape(KEY_CHUNK // COUNT_ROWS, COUNT_ROWS, BLOCK_Q), axis=0)
        c = lax.fori_loop(0, n_chunks, body, jnp.zeros((COUNT_ROWS, BLOCK_Q), F32))
        return jnp.sum(c, axis=0, keepdims=True)

    def halve(_, carry):
        lo, hi, c_lo, _ = carry
        mid = lo + (hi - lo) * 0.5
        cnt = count_ge(mid)
        enough = cnt >= k_sel
        shrinks = jnp.where((mid > lo) & (mid < hi), 1.0, 0.0)
        return jnp.where(enough, mid, lo), jnp.where(enough, hi, mid), jnp.where(enough, cnt, c_lo), shrinks

    def pending(carry):
        rounds, todo, _, _, _ = carry
        return (rounds < MAX_BISECT_ITERS // BISECT_ROUND) & (todo > 0.0)

    def bisect_round(carry):
        rounds, _, lo, hi, c_lo = carry
        lo, hi, c_lo, shrinks = lax.fori_loop(0, BISECT_ROUND, halve, (lo, hi, c_lo, jnp.ones(row1, F32)))
        settled = short | (c_lo == k_sel) | (shrinks == 0.0)
        return rounds + 1, jnp.sum(jnp.where(settled, 0.0, 1.0)), lo, hi, c_lo

    _, _, lo, _, _ = lax.while_loop(pending, bisect_round,
                                    (jnp.int32(0), jnp.float32(1.0), rmin, rmax, n_causal))
    thr = jnp.where(short, jnp.finfo(F32).min, lo)

    m_ref[...] = jnp.full(m_ref.shape, -1e30, F32)
    l_ref[...] = jnp.zeros(l_ref.shape, F32)
    acc_ref[...] = jnp.zeros(acc_ref.shape, F32)

    def qk_products(kc, slot):
        ks = pl.multiple_of(jnp.minimum(kc, last) * KEY_CHUNK, KEY_CHUNK)
        for g in range(N_KV_A):
            k_c = k_ref[pl.ds(ks, KEY_CHUNK), g * HEAD_DIM_A:(g + 1) * HEAD_DIM_A]
            prod_ref[slot, :, g * GRP_LANES:(g + 1) * GRP_LANES] = jnp.dot(
                k_c, qz_ref[:, g * GRP_LANES:(g + 1) * GRP_LANES], preferred_element_type=F32)

    def softmax_pv(kc, slot, near):
        live_thr = jnp.where(kc <= last, thr, jnp.inf) if near else thr
        kc = jnp.minimum(kc, last) if near else kc
        mask = jnp.where(sc_ref[kc] >= live_thr, 0.0, -jnp.inf)
        mask = jnp.concatenate([mask] * GRP, axis=1)
        for g in range(N_KV_A):
            lg = prod_ref[slot, :, g * GRP_LANES:(g + 1) * GRP_LANES]
            if near:
                lg = lg + bias_ref[jnp.minimum(i - 2 * kc, N_BIAS_TILES - 1), g]
            lg = lg + mask
            m_old = m_ref[g]
            m_new = jnp.maximum(m_old, jnp.max(lg, axis=0, keepdims=True))
            alpha = jnp.exp2(m_old - m_new)
            p = jnp.exp2(lg - m_new)
            pv = jnp.dot(vt_ref[kc, g * V_ROWS:(g + 1) * V_ROWS, :], p.astype(BF16),
                         preferred_element_type=F32)
            l_ref[g] = alpha * l_ref[g] + pv[HEAD_DIM_A:HEAD_DIM_A + 1]
            acc_ref[g] = alpha * acc_ref[g] + pv[0:HEAD_DIM_A]
            m_ref[g] = m_new

    def attend(near):
        def body(j, carry):
            qk_products(2 * j + 1, 1)
            softmax_pv(2 * j, 0, near)
            qk_products(2 * j + 2, 0)
            softmax_pv(2 * j + 1, 1, near)
            return carry
        return body

    n_far = jnp.maximum(n_pairs - 2, 0)
    qk_products(0, 0)
    lax.fori_loop(0, n_far, attend(False), 0)
    lax.fori_loop(n_far, n_pairs, attend(True), 0)

    for g in range(N_KV_A):
        out_t = acc_ref[g] / l_ref[g]
        for j in range(GRP):
            h = g * GRP + j
            o_ref[:, h * HEAD_DIM_A:(h + 1) * HEAD_DIM_A] = (
                out_t[:, j * BLOCK_Q:(j + 1) * BLOCK_Q].T.astype(o_ref.dtype))


def _sparse_attention(bias_tiles, q_t, v_t, wi_t, k_n, batch, seq):
    nb = seq // BLOCK_Q
    n_seq_chunks = seq // KEY_CHUNK
    kw_blk = KV_WIDTH // V7X_LANES
    return pl.pallas_call(
        _sparse_attn_kernel,
        grid=(batch, nb),
        in_specs=[pl.BlockSpec(bias_tiles.shape, lambda b, i: (0, 0, 0, 0)),
                  pl.BlockSpec((ATTN_WIDTH, BLOCK_Q), lambda b, i: (0, b * nb + i)),
                  pl.BlockSpec((IDX_HEADS * IDX_DIM, BLOCK_Q), lambda b, i: (1, b * nb + i)),
                  pl.BlockSpec((IDX_HEADS, BLOCK_Q), lambda b, i: (0, b * nb + i)),
                  pl.BlockSpec((seq, V7X_LANES), lambda b, i: (b, kw_blk)),
                  pl.BlockSpec((seq, KV_WIDTH), lambda b, i: (b, 0)),
                  pl.BlockSpec((n_seq_chunks, N_KV_A * V_ROWS, KEY_CHUNK), lambda b, i: (b, 0, 0))],
        out_specs=pl.BlockSpec((BLOCK_Q, ATTN_WIDTH), lambda b, i: (b * nb + i, 0)),
        out_shape=jax.ShapeDtypeStruct((batch * seq, ATTN_WIDTH), BF16),
        scratch_shapes=[pltpu.VMEM((HEAD_DIM_A, N_HEADS_A * BLOCK_Q), BF16),
                        pltpu.VMEM((V7X_LANES, IDX_HEADS * BLOCK_Q), BF16),
                        pltpu.VMEM((n_seq_chunks, KEY_CHUNK, BLOCK_Q), F32),
                        pltpu.VMEM((2, KEY_CHUNK, IDX_HEADS * BLOCK_Q), F32),
                        pltpu.VMEM((N_KV_A, 1, GRP_LANES), F32),
                        pltpu.VMEM((N_KV_A, 1, GRP_LANES), F32),
                        pltpu.VMEM((N_KV_A, HEAD_DIM_A, GRP_LANES), F32)],
        compiler_params=_params(2),
        name="sparse_attention",
    )(bias_tiles, q_t, q_t, wi_t, k_n, k_n, v_t)


def _s5_input_kernel(u_ref, lr_ref, li_ref, ldt_ref, bre_ref, bim_ref,
                     bur_ref, bui_ref, ar_ref, ai_ref, bbr_ref, bbi_ref):
    @pl.when(pl.program_id(0) == 0)
    def _():
        dt = jnp.exp(ldt_ref[...])
        lr = lr_ref[...]
        li = li_ref[...]
        mag = jnp.exp(lr * dt)
        ar = mag * jnp.cos(li * dt)
        ai = mag * jnp.sin(li * dt)
        den = lr * lr + li * li
        nr = ar - 1.0
        cr = (nr * lr + ai * li) / den
        ci = (ai * lr - nr * li) / den
        bre = bre_ref[...]
        bim = bim_ref[...]
        bbr_ref[...] = (cr * bre - ci * bim).astype(BF16)
        bbi_ref[...] = (cr * bim + ci * bre).astype(BF16)
        ar_ref[...] = ar
        ai_ref[...] = ai

    u = u_ref[...].astype(BF16)
    bur_ref[...] = jnp.dot(u, bbr_ref[...], preferred_element_type=F32)
    bui_ref[...] = jnp.dot(u, bbi_ref[...], preferred_element_type=F32)


def _s5_input(us, lam_re, lam_im, log_dt, bre_bd, bim_bd):
    m = us.shape[0]
    tm = SMALL_ROW_TILE
    row = pl.BlockSpec((1, SSM_STATES), lambda i: (0, 0))
    mat = pl.BlockSpec((SSM_WIDTH, SSM_STATES), lambda i: (0, 0))
    out = pl.BlockSpec((tm, SSM_STATES), lambda i: (i, 0))
    return pl.pallas_call(
        _s5_input_kernel,
        grid=(m // tm,),
        in_specs=[pl.BlockSpec((tm, SSM_WIDTH), lambda i: (i, 0)), row, row, row, mat, mat],
        out_specs=[out, out, row, row],
        out_shape=[jax.ShapeDtypeStruct((m, SSM_STATES), F32)] * 2
        + [jax.ShapeDtypeStruct((1, SSM_STATES), F32)] * 2,
        scratch_shapes=[pltpu.VMEM((SSM_WIDTH, SSM_STATES), BF16)] * 2,
        compiler_params=_params(1),
        name="s5_input",
    )(us, lam_re, lam_im, log_dt, bre_bd, bim_bd)


def _s5_scan_kernel(ar_ref, ai_ref, br_ref, bi_ref, xr_ref, xi_ref, sr_ref, si_ref):
    @pl.when(pl.program_id(0) == 0)
    def _():
        sr_ref[...] = jnp.zeros(sr_ref.shape, F32)
        si_ref[...] = jnp.zeros(si_ref.shape, F32)

    ar = ar_ref[...]
    ai = ai_ref[...]
    steps = br_ref.shape[1]

    def step(t, carry):
        xr, xi = carry
        nr = ar * xr - ai * xi + br_ref[:, t]
        ni = ar * xi + ai * xr + bi_ref[:, t]
        xr_ref[:, t] = nr.astype(xr_ref.dtype)
        xi_ref[:, t] = ni.astype(xi_ref.dtype)
        return nr, ni

    xr, xi = lax.fori_loop(0, steps, step, (sr_ref[...], si_ref[...]), unroll=8)
    sr_ref[...] = xr
    si_ref[...] = xi


def _s5_scan(a_re, a_im, bu_re, bu_im):
    batch, seq, rows, lanes = bu_re.shape
    blk = pl.BlockSpec((batch, SCAN_CHUNK, rows, lanes), lambda c: (0, c, 0, 0))
    par = pl.BlockSpec((rows, lanes), lambda c: (0, 0))
    return pl.pallas_call(
        _s5_scan_kernel,
        grid=(seq // SCAN_CHUNK,),
        in_specs=[par, par, blk, blk],
        out_specs=[blk, blk],
        out_shape=[jax.ShapeDtypeStruct(bu_re.shape, BF16)] * 2,
        scratch_shapes=[pltpu.VMEM((batch, rows, lanes), F32)] * 2,
        compiler_params=_params(1),
        name="s5_scan",
    )(a_re, a_im, bu_re, bu_im)


def _s5_output_kernel(xr_ref, xi_ref, u_ref, cr_ref, ci_ref, d_ref, wg_ref, bg_ref, o_ref):
    y = (jnp.dot(xr_ref[...], cr_ref[...], preferred_element_type=F32)
         - jnp.dot(xi_ref[...], ci_ref[...], preferred_element_type=F32)
         + d_ref[...] * u_ref[...])
    g = jax.nn.gelu(y)
    gate = jax.nn.sigmoid(jnp.dot(g.astype(BF16), wg_ref[...], preferred_element_type=F32) + bg_ref[...])
    o_ref[...] = (g * gate).astype(o_ref.dtype)


def _s5_output(xr, xi, us, cre_bd, cim_bd, d_row, w_glu, b_glu):
    m = us.shape[0]
    tm = SMALL_ROW_TILE
    state = pl.BlockSpec((tm, SSM_STATES), lambda i: (i, 0))
    chan = pl.BlockSpec((tm, SSM_WIDTH), lambda i: (i, 0))
    cmat = pl.BlockSpec((SSM_STATES, SSM_WIDTH), lambda i: (0, 0))
    row = pl.BlockSpec((1, SSM_WIDTH), lambda i: (0, 0))
    return pl.pallas_call(
        _s5_output_kernel,
        grid=(m // tm,),
        in_specs=[state, state, chan, cmat, cmat, row,
                  pl.BlockSpec((SSM_WIDTH, SSM_WIDTH), lambda i: (0, 0)), row],
        out_specs=chan,
        out_shape=jax.ShapeDtypeStruct((m, SSM_WIDTH), BF16),
        compiler_params=_params(1),
        name="s5_output",
    )(xr, xi, us, cre_bd, cim_bd, d_row, w_glu, b_glu)


def _block_diag(blocks):
    g, r, c = blocks.shape
    eye = jnp.eye(g, dtype=blocks.dtype)
    return (eye[:, None, :, None] * blocks[:, :, None, :]).reshape(g * r, g * c)


def _post_mixer_kernel(x_ref, ya_ref, yb_ref, ga_ref, gb_ref, wa_ref, wb_ref, wo_ref, gc_ref, wq_ref,
                       kv_ref, wco_ref, o_ref):
    width = X_HEADS * X_HEAD_DIM
    scale = X_HEAD_DIM ** -0.5
    za = jnp.dot(ya_ref[...], wa_ref[...], preferred_element_type=F32)
    zb = jnp.dot(yb_ref[...], wb_ref[...], preferred_element_type=F32)
    z = (ga_ref[...].astype(F32) * za + gb_ref[...].astype(F32) * zb).astype(BF16)
    x1 = x_ref[...] + jnp.dot(z, wo_ref[...], preferred_element_type=F32)
    xn = _rms(x1, gc_ref[...], EPS).astype(BF16)
    q = jnp.dot(xn, wq_ref[...], preferred_element_type=F32).astype(BF16)
    heads = []
    for h in range(X_HEADS):
        cols = slice(h * X_HEAD_DIM, (h + 1) * X_HEAD_DIM)
        vcols = slice(width + h * X_HEAD_DIM, width + (h + 1) * X_HEAD_DIM)
        lg = lax.dot_general(q[:, cols], kv_ref[:, cols], _NT, preferred_element_type=F32) * scale
        p = jnp.exp(lg - jnp.max(lg, axis=-1, keepdims=True))
        o = jnp.dot(p.astype(BF16), kv_ref[:, vcols], preferred_element_type=F32)
        heads.append((o / jnp.sum(p, axis=-1, keepdims=True)).astype(BF16))
    attn = jnp.concatenate(heads, axis=1)
    o_ref[...] = x1 + jnp.dot(attn, wco_ref[...], preferred_element_type=F32)


def _post_mixer(x, ya, yb, gates, w_a, w_b, w_out, gain_cross, w_q, kv, w_co, batch, seq, mem_len):
    d = x.shape[1]
    width = X_HEADS * X_HEAD_DIM
    tm = SMALL_ROW_TILE
    nt = seq // tm
    row = pl.BlockSpec((tm, d), lambda b, i: (b * nt + i, 0))
    return pl.pallas_call(
        _post_mixer_kernel,
        grid=(batch, nt),
        in_specs=[row,
                  pl.BlockSpec((tm, ya.shape[1]), lambda b, i: (b * nt + i, 0)),
                  pl.BlockSpec((tm, yb.shape[1]), lambda b, i: (b * nt + i, 0)),
                  row,
                  pl.BlockSpec((tm, d), lambda b, i: (b * nt + i, 1)),
                  pl.BlockSpec(w_a.shape, lambda b, i: (0, 0)),
                  pl.BlockSpec(w_b.shape, lambda b, i: (0, 0)),
                  pl.BlockSpec((d, d), lambda b, i: (0, 0)),
                  pl.BlockSpec((1, d), lambda b, i: (0, 0)),
                  pl.BlockSpec((d, width), lambda b, i: (0, 0)),
                  pl.BlockSpec((mem_len, 2 * width), lambda b, i: (b, 0)),
                  pl.BlockSpec((width, d), lambda b, i: (0, 0))],
        out_specs=row,
        out_shape=jax.ShapeDtypeStruct(x.shape, F32),
        compiler_params=_params(2),
        name="post_mixer",
    )(x, ya, yb, gates, gates, w_a, w_b, w_out, gain_cross.reshape(1, d), w_q, kv, w_co)


def _conv_ffn_kernel(x_ref, g_ref, wa_ref, wb_ref, cwa_ref, cwb_ref, cba_ref, cbb_ref, wd_ref, gf_ref,
                     o_ref, xn_ref, ha_ref, hb_ref, ta_ref, tb_ref, *, tiles_per_seq):
    i = pl.program_id(0)
    j = pl.program_id(1)
    tm = x_ref.shape[0]
    halo = V7X_SUBLANES

    @pl.when(j == 0)
    def _():
        xn_ref[...] = _rms(x_ref[...], g_ref[...], EPS).astype(BF16)
        o_ref[...] = jnp.zeros(o_ref.shape, F32)

    @pl.when(i % tiles_per_seq == 0)
    def _():
        ha_ref[0:halo] = jnp.zeros((halo, ha_ref.shape[1]), F32)
        hb_ref[0:halo] = jnp.zeros((halo, hb_ref.shape[1]), F32)

    @pl.when(i % tiles_per_seq != 0)
    def _():
        ha_ref[0:halo] = ta_ref[j]
        hb_ref[0:halo] = tb_ref[j]

    rows = tm // FFN_ROW_SLABS
    for s in range(FFN_ROW_SLABS):
        xn = xn_ref[s * rows:(s + 1) * rows]
        ha_ref[halo + s * rows:halo + (s + 1) * rows] = jnp.dot(xn, wa_ref[...], preferred_element_type=F32)
        hb_ref[halo + s * rows:halo + (s + 1) * rows] = jnp.dot(xn, wb_ref[...], preferred_element_type=F32)
    ta_ref[j] = ha_ref[tm:tm + halo]
    tb_ref[j] = hb_ref[tm:tm + halo]

    def conv(h_ref, cw_ref, cb_ref, r0):
        out = cb_ref[...]
        for tap in range(CONV_WIDTH):
            start = r0 + halo - (CONV_WIDTH - 1) + tap
            out = out + h_ref[start:start + rows] * cw_ref[tap:tap + 1]
        return out

    for s in range(FFN_ROW_SLABS):
        a = conv(ha_ref, cwa_ref, cba_ref, s * rows)
        b = conv(hb_ref, cwb_ref, cbb_ref, s * rows)
        act = (a * jax.nn.sigmoid(a) * b).astype(BF16)
        o_ref[s * rows:(s + 1) * rows] += jnp.dot(act, wd_ref[...], preferred_element_type=F32)

    @pl.when(j == pl.num_programs(1) - 1)
    def _():
        o_ref[...] = _rms(x_ref[...] + o_ref[...], gf_ref[...], EPS)


def _conv_ffn(x, gain, w_up, conv_w, conv_b, w_down, gain_final, seq):
    m, d = x.shape
    d_ff = w_down.shape[0]
    tm, tf = FFN_ROW_TILE, COL_TILE
    nj = d_ff // tf
    halo = V7X_SUBLANES
    return pl.pallas_call(
        functools.partial(_conv_ffn_kernel, tiles_per_seq=seq // tm),
        grid=(m // tm, nj),
        in_specs=[pl.BlockSpec((tm, d), lambda i, j: (i, 0)),
                  pl.BlockSpec((1, d), lambda i, j: (0, 0)),
                  pl.BlockSpec((d, tf), lambda i, j: (0, j)),
                  pl.BlockSpec((d, tf), lambda i, j: (0, j + nj)),
                  pl.BlockSpec((CONV_WIDTH, tf), lambda i, j: (0, j)),
                  pl.BlockSpec((CONV_WIDTH, tf), lambda i, j: (0, j + nj)),
                  pl.BlockSpec((1, tf), lambda i, j: (0, j)),
                  pl.BlockSpec((1, tf), lambda i, j: (0, j + nj)),
                  pl.BlockSpec((tf, d), lambda i, j: (j, 0)),
                  pl.BlockSpec((1, d), lambda i, j: (0, 0))],
        out_specs=pl.BlockSpec((tm, d), lambda i, j: (i, 0)),
        out_shape=jax.ShapeDtypeStruct((m, d), F32),
        scratch_shapes=[pltpu.VMEM((tm, d), BF16),
                        pltpu.VMEM((tm + halo, tf), F32),
                        pltpu.VMEM((tm + halo, tf), F32),
                        pltpu.VMEM((nj, halo, tf), F32),
                        pltpu.VMEM((nj, halo, tf), F32)],
        compiler_params=_params(2),
        name="conv_ffn",
    )(x, gain.reshape(1, d), w_up, w_up, conv_w, conv_w, conv_b.reshape(1, -1), conv_b.reshape(1, -1),
      w_down, gain_final.reshape(1, d))


def kernel(x, mem, rel_bias, norm_mix, w_in, ssm_lambda_re, ssm_lambda_im, ssm_log_dt, ssm_b_re, ssm_b_im, ssm_c_re, ssm_c_im, ssm_d, ssm_w_glu, ssm_b_glu, w_branch_a, w_branch_b, w_out, norm_cross, norm_mem, w_cross_q, w_cross_kv, w_cross_o, norm_ffn, w_up, ffn_conv_w, ffn_conv_b, w_down, norm_final):
    batch, seq, d_model = x.shape
    mem_len = mem.shape[1]
    depth = w_in.shape[0]
    assert depth == 1, "the final rmsnorm is fused into the last layer's ConvFFN kernel"
    assert seq % ROW_TILE == 0 and seq % KEY_CHUNK == 0 and seq >= 4 * TOPK_MAX
    m = batch * seq
    xf = x.reshape(m, d_model)
    memf = mem.reshape(batch * mem_len, d_model)
    bias_tiles = _bias_tiles(rel_bias)

    splits = (ATTN_WIDTH, KV_WIDTH, KV_WIDTH, IDX_HEADS * IDX_DIM, IDX_DIM, IDX_HEADS, SSM_WIDTH, d_model, d_model)
    offs = [int(o) for o in np.cumsum((0,) + splits)]
    for l in range(depth):
        w = w_in[l]
        wq, wk, wv, wqi, wki, wwi, wss, wga, wgb = [w[:, offs[n]:offs[n + 1]] for n in range(len(splits))]
        pad = jnp.zeros((d_model, V7X_LANES - IDX_DIM - IDX_HEADS), w.dtype)
        w_t = jnp.concatenate([wq, wqi, wv, wwi], axis=1).T.astype(BF16)
        w_n = jnp.concatenate([wk, wki, wwi, pad], axis=1).astype(BF16)
        w_gate = jnp.concatenate([wga, wgb], axis=1).astype(BF16)

        q_t, v_t, wi_t, k_n = _proj_attn(xf, norm_mix[l], w_t, w_n)
        us = _norm_matmul(xf, norm_mix[l], wss.astype(BF16), F32, COL_TILE, "proj_ssm")
        gates = _norm_matmul(xf, norm_mix[l], w_gate, BF16, COL_TILE, "proj_gates", sigmoid=True)

        y_a = _sparse_attention(bias_tiles, q_t, v_t, wi_t, k_n, batch, seq)

        bre_bd = _block_diag(jnp.transpose(ssm_b_re[l], (0, 2, 1)))
        bim_bd = _block_diag(jnp.transpose(ssm_b_im[l], (0, 2, 1)))
        cre_bd = _block_diag(jnp.transpose(ssm_c_re[l], (0, 2, 1))).astype(BF16)
        cim_bd = _block_diag(jnp.transpose(ssm_c_im[l], (0, 2, 1))).astype(BF16)
        log_dt = jnp.broadcast_to(ssm_log_dt[l][:, None], (SSM_GROUPS, SSM_STATE)).reshape(1, SSM_STATES)
        bu_re, bu_im, a_re, a_im = _s5_input(us, ssm_lambda_re[l].reshape(1, SSM_STATES),
                                             ssm_lambda_im[l].reshape(1, SSM_STATES), log_dt, bre_bd, bim_bd)
        state_rows = SSM_STATES // V7X_LANES
        sshape = (batch, seq, state_rows, V7X_LANES)
        xs_re, xs_im = _s5_scan(a_re.reshape(state_rows, V7X_LANES), a_im.reshape(state_rows, V7X_LANES),
                                bu_re.reshape(sshape), bu_im.reshape(sshape))
        y_b = _s5_output(xs_re.reshape(m, SSM_STATES), xs_im.reshape(m, SSM_STATES), us, cre_bd, cim_bd,
                         ssm_d[l].reshape(1, SSM_WIDTH), ssm_w_glu[l].astype(BF16),
                         ssm_b_glu[l].reshape(1, SSM_WIDTH))

        kvx = _norm_matmul(memf, norm_mem[l], w_cross_kv[l].astype(BF16), BF16, COL_TILE, "cross_kv")
        xf = _post_mixer(xf, y_a, y_b, gates, w_branch_a[l].astype(BF16), w_branch_b[l].astype(BF16),
                         w_out[l].astype(BF16), norm_cross[l], w_cross_q[l].astype(BF16), kvx,
                         w_cross_o[l].astype(BF16), batch, seq, mem_len)

        xf = _conv_ffn(xf, norm_ffn[l], w_up[l].astype(BF16), ffn_conv_w[l], ffn_conv_b[l],
                       w_down[l].astype(BF16), norm_final, seq)
    return xf.reshape(batch, seq, d_model)
```

```python
import functools
import math

import jax
import jax.numpy as jnp
import numpy as np
from jax import lax
from jax.experimental import pallas as pl
from jax.experimental.pallas import tpu as pltpu

N_HEADS_A = 8
N_KV_A = 2
HEAD_DIM_A = 128
ATTN_WIDTH = N_HEADS_A * HEAD_DIM_A
KV_WIDTH = N_KV_A * HEAD_DIM_A
IDX_HEADS = 16
IDX_DIM = 64
TOPK_MAX = 256
BLOCK_Q = 128
REL_BUCKETS = 32
REL_MAX_DIST = 128
SSM_GROUP = 16
SSM_GROUPS = 32
SSM_STATE = 64
SSM_WIDTH = SSM_GROUP * SSM_GROUPS
SSM_STATES = SSM_GROUPS * SSM_STATE
X_HEADS = 4
X_HEAD_DIM = 128
CONV_WIDTH = 3
EPS = 1e-6

V7X_LANES = 128
V7X_SUBLANES = 8
V7X_VMEM_BYTES = 64 * 1024 * 1024
VMEM_LIMIT_BYTES = V7X_VMEM_BYTES - 8 * 1024 * 1024

GRP = N_HEADS_A // N_KV_A
GRP_LANES = GRP * BLOCK_Q
V_ROWS = HEAD_DIM_A + 16
KEY_CHUNK = 2 * BLOCK_Q
N_BIAS_TILES = 4
MAX_BISECT_ITERS = 64
BISECT_ROUND = 4
COUNT_ROWS = 64
LOG2_E = math.log2(math.e)
ROW_TILE = 1024
SMALL_ROW_TILE = 512
COL_TILE = 512
FFN_ROW_TILE = 512
FFN_ROW_SLABS = 2
SCAN_CHUNK = 128

F32 = jnp.float32
BF16 = jnp.bfloat16
_NT = (((1,), (1,)), ((), ()))


def _params(n_axes):
    return pltpu.CompilerParams(dimension_semantics=("arbitrary",) * n_axes,
                                vmem_limit_bytes=VMEM_LIMIT_BYTES)


def _rms(x, gain, eps):
    return x * lax.rsqrt(jnp.mean(x * x, axis=-1, keepdims=True) + eps) * gain


def _norm_matmul_kernel(x_ref, g_ref, w_ref, o_ref, xn_ref, *, sigmoid):
    @pl.when(pl.program_id(1) == 0)
    def _():
        xn_ref[...] = _rms(x_ref[...], g_ref[...], EPS).astype(BF16)

    tm = o_ref.shape[0]
    rows = min(tm, SMALL_ROW_TILE) if sigmoid else tm
    for r in range(0, tm, rows):
        y = jnp.dot(xn_ref[r:r + rows], w_ref[...], preferred_element_type=F32)
        if sigmoid:
            y = 0.5 * jnp.tanh(0.5 * y) + 0.5
        o_ref[r:r + rows] = y.astype(o_ref.dtype)


def _norm_matmul(x, gain, w, out_dtype, tn, name, sigmoid=False):
    m, k = x.shape
    n = w.shape[1]
    tm = min(ROW_TILE, m)
    return pl.pallas_call(
        functools.partial(_norm_matmul_kernel, sigmoid=sigmoid),
        grid=(m // tm, n // tn),
        in_specs=[pl.BlockSpec((tm, k), lambda i, j: (i, 0)),
                  pl.BlockSpec((1, k), lambda i, j: (0, 0)),
                  pl.BlockSpec((k, tn), lambda i, j: (0, j))],
        out_specs=pl.BlockSpec((tm, tn), lambda i, j: (i, j)),
        out_shape=jax.ShapeDtypeStruct((m, n), out_dtype),
        scratch_shapes=[pltpu.VMEM((tm, k), BF16)],
        compiler_params=_params(2),
        name=name,
    )(x, gain.reshape(1, k), w)


def _proj_attn_kernel(x_ref, g_ref, wt_ref, wn_ref, qt_ref, vt_ref, wit_ref, kn_ref):
    xn = _rms(x_ref[...], g_ref[...], EPS).astype(BF16)
    res = lax.dot_general(wt_ref[...], xn, _NT, preferred_element_type=F32)
    nq = qt_ref.shape[0]
    qt_ref[0:ATTN_WIDTH] = (res[0:ATTN_WIDTH] * (HEAD_DIM_A ** -0.5 * LOG2_E)).astype(BF16)
    qt_ref[ATTN_WIDTH:nq] = res[ATTN_WIDTH:nq].astype(BF16)
    pad_rows = V_ROWS - HEAD_DIM_A
    ones_row = (lax.broadcasted_iota(jnp.int32, (pad_rows, KEY_CHUNK), 0) == 0).astype(BF16)
    for c in range(vt_ref.shape[0]):
        for g in range(N_KV_A):
            rows = res[nq + g * HEAD_DIM_A:nq + (g + 1) * HEAD_DIM_A, c * KEY_CHUNK:(c + 1) * KEY_CHUNK]
            vt_ref[c, g * V_ROWS:g * V_ROWS + HEAD_DIM_A] = rows.astype(BF16)
            vt_ref[c, g * V_ROWS + HEAD_DIM_A:(g + 1) * V_ROWS] = ones_row
    wit_ref[...] = res[nq + KV_WIDTH:nq + KV_WIDTH + IDX_HEADS]
    kn_ref[...] = jnp.dot(xn, wn_ref[...], preferred_element_type=F32).astype(BF16)


def _proj_attn(x, gain, w_t, w_n):
    m, k = x.shape
    tm = SMALL_ROW_TILE
    nq = ATTN_WIDTH + IDX_HEADS * IDX_DIM
    return pl.pallas_call(
        _proj_attn_kernel,
        grid=(m // tm,),
        in_specs=[pl.BlockSpec((tm, k), lambda i: (i, 0)),
                  pl.BlockSpec((1, k), lambda i: (0, 0)),
                  pl.BlockSpec(w_t.shape, lambda i: (0, 0)),
                  pl.BlockSpec(w_n.shape, lambda i: (0, 0))],
        out_specs=[pl.BlockSpec((nq, tm), lambda i: (0, i)),
                   pl.BlockSpec((tm // KEY_CHUNK, N_KV_A * V_ROWS, KEY_CHUNK), lambda i: (i, 0, 0)),
                   pl.BlockSpec((IDX_HEADS, tm), lambda i: (0, i)),
                   pl.BlockSpec((tm, w_n.shape[1]), lambda i: (i, 0))],
        out_shape=[jax.ShapeDtypeStruct((nq, m), BF16),
                   jax.ShapeDtypeStruct((m // KEY_CHUNK, N_KV_A * V_ROWS, KEY_CHUNK), BF16),
                   jax.ShapeDtypeStruct((IDX_HEADS, m), F32),
                   jax.ShapeDtypeStruct((m, w_n.shape[1]), BF16)],
        compiler_params=_params(1),
        name="proj_attn",
    )(x, gain.reshape(1, k), w_t, w_n)


def _t5_bucket(n):
    max_exact = REL_BUCKETS // 2
    n = jnp.maximum(n, 0)
    nf = jnp.maximum(n, 1).astype(F32)
    large = max_exact + (jnp.log(nf / max_exact) / math.log(REL_MAX_DIST / max_exact)
                         * (REL_BUCKETS - max_exact)).astype(jnp.int32)
    large = jnp.minimum(large, REL_BUCKETS - 1)
    return jnp.where(n < max_exact, n, large)


def _bias_tiles_kernel(rb_ref, o_ref):
    shape = (KEY_CHUNK, BLOCK_Q)
    key = lax.broadcasted_iota(jnp.int32, shape, 0)
    qry = lax.broadcasted_iota(jnp.int32, shape, 1)
    for o in range(N_BIAS_TILES):
        bucket = _t5_bucket(o * BLOCK_Q + qry - key)
        for h in range(N_HEADS_A):
            def pick(b, t, bucket=bucket, h=h):
                return jnp.where(bucket == b, rb_ref[b, h], t)
            tile = lax.fori_loop(0, REL_BUCKETS, pick, jnp.zeros(shape, F32))
            j = h % GRP
            o_ref[o, h // GRP, :, j * BLOCK_Q:(j + 1) * BLOCK_Q] = (tile - rb_ref[REL_BUCKETS - 1, h]) * LOG2_E


def _bias_tiles(rel_bias):
    return pl.pallas_call(
        _bias_tiles_kernel,
        in_specs=[pl.BlockSpec(memory_space=pltpu.SMEM)],
        out_specs=pl.BlockSpec(memory_space=pltpu.VMEM),
        out_shape=jax.ShapeDtypeStruct((N_BIAS_TILES, N_KV_A, KEY_CHUNK, GRP_LANES), F32),
        name="rel_bias_tiles",
    )(rel_bias)


def _sparse_attn_kernel(bias_ref, qt_ref, qit_ref, wit_ref, kw_ref, k_ref, vt_ref, o_ref,
                        qz_ref, qiz_ref, sc_ref, prod_ref, m_ref, l_ref, acc_ref, tri_ref, ties_ref):
    i = pl.program_id(1)
    t0 = i * BLOCK_Q
    n_chunks = (i + 2) // 2
    tile = (KEY_CHUNK, BLOCK_Q)
    row1 = (1, BLOCK_Q)
    k_sel = float(TOPK_MAX)

    @pl.when((pl.program_id(0) == 0) & (i == 0))
    def _():
        qiz_ref[...] = jnp.zeros(qiz_ref.shape, BF16)

    for h in range(IDX_HEADS):
        qiz_ref[0:IDX_DIM, h * BLOCK_Q:(h + 1) * BLOCK_Q] = qit_ref[h * IDX_DIM:(h + 1) * IDX_DIM, :]
    for h in range(N_HEADS_A):
        qz_ref[:, h * BLOCK_Q:(h + 1) * BLOCK_Q] = qt_ref[h * HEAD_DIM_A:(h + 1) * HEAD_DIM_A, :]
    w_rows = wit_ref[...] * ((IDX_HEADS * IDX_DIM) ** -0.5)

    key_row = lax.broadcasted_iota(jnp.int32, tile, 0)
    q_pos = t0 + lax.broadcasted_iota(jnp.int32, tile, 1)

    last = n_chunks - 1
    n_pairs = (n_chunks + 1) // 2

    def idx_products(kc, slot):
        ks = pl.multiple_of(jnp.minimum(kc, last) * KEY_CHUNK, KEY_CHUNK)
        prod_ref[slot] = jnp.dot(kw_ref[pl.ds(ks, KEY_CHUNK), :], qiz_ref[...], preferred_element_type=F32)

    def reduce_scores(kc, slot, carry):
        rmax, rmin = carry
        kc = jnp.minimum(kc, last)
        s = jnp.zeros(tile, F32)
        for h in range(IDX_HEADS):
            d = prod_ref[slot, :, h * BLOCK_Q:(h + 1) * BLOCK_Q]
            s = s + jnp.maximum(d, 0.0) * w_rows[h:h + 1, :]
        causal = (kc * KEY_CHUNK + key_row) <= q_pos
        sc_ref[kc] = jnp.where(causal, s, -jnp.inf)
        rmax = jnp.maximum(rmax, jnp.max(jnp.where(causal, s, -jnp.inf), axis=0, keepdims=True))
        rmin = jnp.minimum(rmin, jnp.min(jnp.where(causal, s, jnp.inf), axis=0, keepdims=True))
        return rmax, rmin

    def score_pair(j, carry):
        idx_products(2 * j + 1, 1)
        carry = reduce_scores(2 * j, 0, carry)
        idx_products(2 * j + 2, 0)
        return reduce_scores(2 * j + 1, 1, carry)

    idx_products(0, 0)
    rmax, rmin = lax.fori_loop(0, n_pairs, score_pair,
                               (jnp.full(row1, -jnp.inf, F32), jnp.full(row1, jnp.inf, F32)))

    n_causal = (t0 + lax.broadcasted_iota(jnp.int32, row1, 1) + 1).astype(F32)
    short = n_causal <= k_sel

    def count_ge(t):
        def body(kc, c):
            hit = jnp.where(sc_ref[kc] >= t, 1.0, 0.0)
            return c + jnp.sum(hit.reshape(KEY_CHUNK // COUNT_ROWS, COUNT_ROWS, BLOCK_Q), axis=0)
        c = lax.fori_loop(0, n_chunks, body, jnp.zeros((COUNT_ROWS, BLOCK_Q), F32))
        return jnp.sum(c, axis=0, keepdims=True)

    def halve(_, carry):
        lo, hi, c_lo, _ = carry
        mid = lo + (hi - lo) * 0.5
        cnt = count_ge(mid)
        enough = cnt >= k_sel
        shrinks = jnp.where((mid > lo) & (mid < hi), 1.0, 0.0)
        return jnp.where(enough, mid, lo), jnp.where(enough, hi, mid), jnp.where(enough, cnt, c_lo), shrinks

    def pending(carry):
        rounds, todo, _, _, _ = carry
        return (rounds < MAX_BISECT_ITERS // BISECT_ROUND) & (todo > 0.0)

    def bisect_round(carry):
        rounds, _, lo, hi, c_lo = carry
        lo, hi, c_lo, shrinks = lax.fori_loop(0, BISECT_ROUND, halve, (lo, hi, c_lo, jnp.ones(row1, F32)))
        settled = short | (c_lo == k_sel) | (shrinks == 0.0)
        return rounds + 1, jnp.sum(jnp.where(settled, 0.0, 1.0)), lo, hi, c_lo

    hi0 = rmax + jnp.maximum(jnp.abs(rmax) * 2.0 ** -22, jnp.finfo(F32).tiny)
    _, _, lo, hi, c_lo = lax.while_loop(pending, bisect_round,
                                        (jnp.int32(0), jnp.float32(1.0), rmin, hi0, n_causal))
    thr = jnp.where(short, jnp.finfo(F32).min, lo)
    tied = jnp.logical_not(short) & (c_lo > k_sel)
    n_tied = jnp.sum(jnp.where(tied, 1.0, 0.0))

    m_ref[...] = jnp.full(m_ref.shape, -1e30, F32)
    l_ref[...] = jnp.zeros(l_ref.shape, F32)
    acc_ref[...] = jnp.zeros(acc_ref.shape, F32)

    def qk_products(kc, slot):
        ks = pl.multiple_of(jnp.minimum(kc, last) * KEY_CHUNK, KEY_CHUNK)
        for g in range(N_KV_A):
            k_c = k_ref[pl.ds(ks, KEY_CHUNK), g * HEAD_DIM_A:(g + 1) * HEAD_DIM_A]
            prod_ref[slot, :, g * GRP_LANES:(g + 1) * GRP_LANES] = jnp.dot(
                k_c, qz_ref[:, g * GRP_LANES:(g + 1) * GRP_LANES], preferred_element_type=F32)

    def softmax_pv(kc, slot, near, ties, upper=None, quota=None):
        live = (kc <= last) if near else True
        kc = jnp.minimum(kc, last) if near else kc
        sc = sc_ref[kc]
        if ties:
            above = sc >= upper
            tie = (sc >= thr) & jnp.logical_not(above)
            tie_f = jnp.where(tie, 1.0, 0.0)
            upto = jnp.dot(tri_ref[...], tie_f.astype(BF16), preferred_element_type=F32)
            before = ties_ref[...]
            keep = (above | (tie & (before + upto - tie_f < quota))) & live
            ties_ref[...] = before + jnp.where(live, upto[KEY_CHUNK - 1:KEY_CHUNK], 0.0)
        else:
            keep = (sc >= thr) & live
        mask = jnp.where(keep, 0.0, -jnp.inf)
        mask = jnp.concatenate([mask] * GRP, axis=1)
        for g in range(N_KV_A):
            lg = prod_ref[slot, :, g * GRP_LANES:(g + 1) * GRP_LANES]
            if near:
                lg = lg + bias_ref[jnp.minimum(i - 2 * kc, N_BIAS_TILES - 1), g]
            lg = lg + mask
            m_old = m_ref[g]
            m_new = jnp.maximum(m_old, jnp.max(lg, axis=0, keepdims=True))
            alpha = jnp.exp2(m_old - m_new)
            p = jnp.exp2(lg - m_new)
            pv = jnp.dot(vt_ref[kc, g * V_ROWS:(g + 1) * V_ROWS, :], p.astype(BF16),
                         preferred_element_type=F32)
            l_ref[g] = alpha * l_ref[g] + pv[HEAD_DIM_A:HEAD_DIM_A + 1]
            acc_ref[g] = alpha * acc_ref[g] + pv[0:HEAD_DIM_A]
            m_ref[g] = m_new

    def attend(near, ties, **tie_args):
        def body(j, carry):
            qk_products(2 * j + 1, 1)
            softmax_pv(2 * j, 0, near, ties, **tie_args)
            qk_products(2 * j + 2, 0)
            softmax_pv(2 * j + 1, 1, near, ties, **tie_args)
            return carry
        return body

    n_far = jnp.maximum(n_pairs - 2, 0)
    qk_products(0, 0)

    @pl.when(n_tied == 0.0)
    def _():
        lax.fori_loop(0, n_far, attend(False, False), 0)
        lax.fori_loop(n_far, n_pairs, attend(True, False), 0)

    @pl.when(n_tied > 0.0)
    def _():
        row = lax.broadcasted_iota(jnp.int32, (KEY_CHUNK, KEY_CHUNK), 0)
        col = lax.broadcasted_iota(jnp.int32, (KEY_CHUNK, KEY_CHUNK), 1)
        tri_ref[...] = (col <= row).astype(BF16)
        ties_ref[...] = jnp.zeros(ties_ref.shape, F32)
        tie_args = dict(upper=jnp.where(tied, hi, thr),
                        quota=jnp.where(tied, k_sel - count_ge(hi), 0.0))
        lax.fori_loop(0, n_far, attend(False, True, **tie_args), 0)
        lax.fori_loop(n_far, n_pairs, attend(True, True, **tie_args), 0)

    for g in range(N_KV_A):
        out_t = acc_ref[g] / l_ref[g]
        for j in range(GRP):
            h = g * GRP + j
            o_ref[:, h * HEAD_DIM_A:(h + 1) * HEAD_DIM_A] = (
                out_t[:, j * BLOCK_Q:(j + 1) * BLOCK_Q].T.astype(o_ref.dtype))


def _sparse_attention(bias_tiles, q_t, v_t, wi_t, k_n, batch, seq):
    nb = seq // BLOCK_Q
    n_seq_chunks = seq // KEY_CHUNK
    kw_blk = KV_WIDTH // V7X_LANES
    return pl.pallas_call(
        _sparse_attn_kernel,
        grid=(batch, nb),
        in_specs=[pl.BlockSpec(bias_tiles.shape, lambda b, i: (0, 0, 0, 0)),
                  pl.BlockSpec((ATTN_WIDTH, BLOCK_Q), lambda b, i: (0, b * nb + i)),
                  pl.BlockSpec((IDX_HEADS * IDX_DIM, BLOCK_Q), lambda b, i: (1, b * nb + i)),
                  pl.BlockSpec((IDX_HEADS, BLOCK_Q), lambda b, i: (0, b * nb + i)),
                  pl.BlockSpec((seq, V7X_LANES), lambda b, i: (b, kw_blk)),
                  pl.BlockSpec((seq, KV_WIDTH), lambda b, i: (b, 0)),
                  pl.BlockSpec((n_seq_chunks, N_KV_A * V_ROWS, KEY_CHUNK), lambda b, i: (b, 0, 0))],
        out_specs=pl.BlockSpec((BLOCK_Q, ATTN_WIDTH), lambda b, i: (b * nb + i, 0)),
        out_shape=jax.ShapeDtypeStruct((batch * seq, ATTN_WIDTH), BF16),
        scratch_shapes=[pltpu.VMEM((HEAD_DIM_A, N_HEADS_A * BLOCK_Q), BF16),
                        pltpu.VMEM((V7X_LANES, IDX_HEADS * BLOCK_Q), BF16),
                        pltpu.VMEM((n_seq_chunks, KEY_CHUNK, BLOCK_Q), F32),
                        pltpu.VMEM((2, KEY_CHUNK, IDX_HEADS * BLOCK_Q), F32),
                        pltpu.VMEM((N_KV_A, 1, GRP_LANES), F32),
                        pltpu.VMEM((N_KV_A, 1, GRP_LANES), F32),
                        pltpu.VMEM((N_KV_A, HEAD_DIM_A, GRP_LANES), F32),
                        pltpu.VMEM((KEY_CHUNK, KEY_CHUNK), BF16),
                        pltpu.VMEM((1, BLOCK_Q), F32)],
        compiler_params=_params(2),
        name="sparse_attention",
    )(bias_tiles, q_t, q_t, wi_t, k_n, k_n, v_t)


def _s5_input_kernel(u_ref, lr_ref, li_ref, ldt_ref, bre_ref, bim_ref,
                     bur_ref, bui_ref, ar_ref, ai_ref, bbr_ref, bbi_ref):
    @pl.when(pl.program_id(0) == 0)
    def _():
        dt = jnp.exp(ldt_ref[...])
        lr = lr_ref[...]
        li = li_ref[...]
        mag = jnp.exp(lr * dt)
        ar = mag * jnp.cos(li * dt)
        ai = mag * jnp.sin(li * dt)
        den = lr * lr + li * li
        nr = ar - 1.0
        cr = (nr * lr + ai * li) / den
        ci = (ai * lr - nr * li) / den
        bre = bre_ref[...]
        bim = bim_ref[...]
        bbr_ref[...] = (cr * bre - ci * bim).astype(BF16)
        bbi_ref[...] = (cr * bim + ci * bre).astype(BF16)
        ar_ref[...] = ar
        ai_ref[...] = ai

    u = u_ref[...].astype(BF16)
    bur_ref[...] = jnp.dot(u, bbr_ref[...], preferred_element_type=F32)
    bui_ref[...] = jnp.dot(u, bbi_ref[...], preferred_element_type=F32)


def _s5_input(us, lam_re, lam_im, log_dt, bre_bd, bim_bd):
    m = us.shape[0]
    tm = SMALL_ROW_TILE
    row = pl.BlockSpec((1, SSM_STATES), lambda i: (0, 0))
    mat = pl.BlockSpec((SSM_WIDTH, SSM_STATES), lambda i: (0, 0))
    out = pl.BlockSpec((tm, SSM_STATES), lambda i: (i, 0))
    return pl.pallas_call(
        _s5_input_kernel,
        grid=(m // tm,),
        in_specs=[pl.BlockSpec((tm, SSM_WIDTH), lambda i: (i, 0)), row, row, row, mat, mat],
        out_specs=[out, out, row, row],
        out_shape=[jax.ShapeDtypeStruct((m, SSM_STATES), F32)] * 2
        + [jax.ShapeDtypeStruct((1, SSM_STATES), F32)] * 2,
        scratch_shapes=[pltpu.VMEM((SSM_WIDTH, SSM_STATES), BF16)] * 2,
        compiler_params=_params(1),
        name="s5_input",
    )(us, lam_re, lam_im, log_dt, bre_bd, bim_bd)


def _s5_scan_kernel(ar_ref, ai_ref, br_ref, bi_ref, xr_ref, xi_ref, sr_ref, si_ref):
    @pl.when(pl.program_id(0) == 0)
    def _():
        sr_ref[...] = jnp.zeros(sr_ref.shape, F32)
        si_ref[...] = jnp.zeros(si_ref.shape, F32)

    ar = ar_ref[...]
    ai = ai_ref[...]
    steps = br_ref.shape[1]

    def step(t, carry):
        xr, xi = carry
        nr = ar * xr - ai * xi + br_ref[:, t]
        ni = ar * xi + ai * xr + bi_ref[:, t]
        xr_ref[:, t] = nr.astype(xr_ref.dtype)
        xi_ref[:, t] = ni.astype(xi_ref.dtype)
        return nr, ni

    xr, xi = lax.fori_loop(0, steps, step, (sr_ref[...], si_ref[...]), unroll=8)
    sr_ref[...] = xr
    si_ref[...] = xi


def _s5_scan(a_re, a_im, bu_re, bu_im):
    batch, seq, rows, lanes = bu_re.shape
    blk = pl.BlockSpec((batch, SCAN_CHUNK, rows, lanes), lambda c: (0, c, 0, 0))
    par = pl.BlockSpec((rows, lanes), lambda c: (0, 0))
    return pl.pallas_call(
        _s5_scan_kernel,
        grid=(seq // SCAN_CHUNK,),
        in_specs=[par, par, blk, blk],
        out_specs=[blk, blk],
        out_shape=[jax.ShapeDtypeStruct(bu_re.shape, BF16)] * 2,
        scratch_shapes=[pltpu.VMEM((batch, rows, lanes), F32)] * 2,
        compiler_params=_params(1),
        name="s5_scan",
    )(a_re, a_im, bu_re, bu_im)


def _s5_output_kernel(xr_ref, xi_ref, u_ref, cr_ref, ci_ref, d_ref, wg_ref, bg_ref, o_ref):
    y = (jnp.dot(xr_ref[...], cr_ref[...], preferred_element_type=F32)
         - jnp.dot(xi_ref[...], ci_ref[...], preferred_element_type=F32)
         + d_ref[...] * u_ref[...])
    g = jax.nn.gelu(y)
    gate = jax.nn.sigmoid(jnp.dot(g.astype(BF16), wg_ref[...], preferred_element_type=F32) + bg_ref[...])
    o_ref[...] = (g * gate).astype(o_ref.dtype)


def _s5_output(xr, xi, us, cre_bd, cim_bd, d_row, w_glu, b_glu):
    m = us.shape[0]
    tm = SMALL_ROW_TILE
    state = pl.BlockSpec((tm, SSM_STATES), lambda i: (i, 0))
    chan = pl.BlockSpec((tm, SSM_WIDTH), lambda i: (i, 0))
    cmat = pl.BlockSpec((SSM_STATES, SSM_WIDTH), lambda i: (0, 0))
    row = pl.BlockSpec((1, SSM_WIDTH), lambda i: (0, 0))
    return pl.pallas_call(
        _s5_output_kernel,
        grid=(m // tm,),
        in_specs=[state, state, chan, cmat, cmat, row,
                  pl.BlockSpec((SSM_WIDTH, SSM_WIDTH), lambda i: (0, 0)), row],
        out_specs=chan,
        out_shape=jax.ShapeDtypeStruct((m, SSM_WIDTH), BF16),
        compiler_params=_params(1),
        name="s5_output",
    )(xr, xi, us, cre_bd, cim_bd, d_row, w_glu, b_glu)


def _block_diag(blocks):
    g, r, c = blocks.shape
    eye = jnp.eye(g, dtype=blocks.dtype)
    return (eye[:, None, :, None] * blocks[:, :, None, :]).reshape(g * r, g * c)


def _post_mixer_kernel(x_ref, ya_ref, yb_ref, ga_ref, gb_ref, wa_ref, wb_ref, wo_ref, gc_ref, wq_ref,
                       kv_ref, wco_ref, o_ref):
    width = X_HEADS * X_HEAD_DIM
    scale = X_HEAD_DIM ** -0.5
    za = jnp.dot(ya_ref[...], wa_ref[...], preferred_element_type=F32)
    zb = jnp.dot(yb_ref[...], wb_ref[...], preferred_element_type=F32)
    z = (ga_ref[...].astype(F32) * za + gb_ref[...].astype(F32) * zb).astype(BF16)
    x1 = x_ref[...] + jnp.dot(z, wo_ref[...], preferred_element_type=F32)
    xn = _rms(x1, gc_ref[...], EPS).astype(BF16)
    q = jnp.dot(xn, wq_ref[...], preferred_element_type=F32).astype(BF16)
    heads = []
    for h in range(X_HEADS):
        cols = slice(h * X_HEAD_DIM, (h + 1) * X_HEAD_DIM)
        vcols = slice(width + h * X_HEAD_DIM, width + (h + 1) * X_HEAD_DIM)
        lg = lax.dot_general(q[:, cols], kv_ref[:, cols], _NT, preferred_element_type=F32) * scale
        p = jnp.exp(lg - jnp.max(lg, axis=-1, keepdims=True))
        o = jnp.dot(p.astype(BF16), kv_ref[:, vcols], preferred_element_type=F32)
        heads.append((o / jnp.sum(p, axis=-1, keepdims=True)).astype(BF16))
    attn = jnp.concatenate(heads, axis=1)
    o_ref[...] = x1 + jnp.dot(attn, wco_ref[...], preferred_element_type=F32)


def _post_mixer(x, ya, yb, gates, w_a, w_b, w_out, gain_cross, w_q, kv, w_co, batch, seq, mem_len):
    d = x.shape[1]
    width = X_HEADS * X_HEAD_DIM
    tm = SMALL_ROW_TILE
    nt = seq // tm
    row = pl.BlockSpec((tm, d), lambda b, i: (b * nt + i, 0))
    return pl.pallas_call(
        _post_mixer_kernel,
        grid=(batch, nt),
        in_specs=[row,
                  pl.BlockSpec((tm, ya.shape[1]), lambda b, i: (b * nt + i, 0)),
                  pl.BlockSpec((tm, yb.shape[1]), lambda b, i: (b * nt + i, 0)),
                  row,
                  pl.BlockSpec((tm, d), lambda b, i: (b * nt + i, 1)),
                  pl.BlockSpec(w_a.shape, lambda b, i: (0, 0)),
                  pl.BlockSpec(w_b.shape, lambda b, i: (0, 0)),
                  pl.BlockSpec((d, d), lambda b, i: (0, 0)),
                  pl.BlockSpec((1, d), lambda b, i: (0, 0)),
                  pl.BlockSpec((d, width), lambda b, i: (0, 0)),
                  pl.BlockSpec((mem_len, 2 * width), lambda b, i: (b, 0)),
                  pl.BlockSpec((width, d), lambda b, i: (0, 0))],
        out_specs=row,
        out_shape=jax.ShapeDtypeStruct(x.shape, F32),
        compiler_params=_params(2),
        name="post_mixer",
    )(x, ya, yb, gates, gates, w_a, w_b, w_out, gain_cross.reshape(1, d), w_q, kv, w_co)


def _conv_ffn_kernel(x_ref, g_ref, wa_ref, wb_ref, cwa_ref, cwb_ref, cba_ref, cbb_ref, wd_ref, gf_ref,
                     o_ref, xn_ref, ha_ref, hb_ref, ta_ref, tb_ref, *, tiles_per_seq):
    i = pl.program_id(0)
    j = pl.program_id(1)
    tm = x_ref.shape[0]
    halo = V7X_SUBLANES

    @pl.when(j == 0)
    def _():
        xn_ref[...] = _rms(x_ref[...], g_ref[...], EPS).astype(BF16)
        o_ref[...] = jnp.zeros(o_ref.shape, F32)

    @pl.when(i % tiles_per_seq == 0)
    def _():
        ha_ref[0:halo] = jnp.zeros((halo, ha_ref.shape[1]), F32)
        hb_ref[0:halo] = jnp.zeros((halo, hb_ref.shape[1]), F32)

    @pl.when(i % tiles_per_seq != 0)
    def _():
        ha_ref[0:halo] = ta_ref[j]
        hb_ref[0:halo] = tb_ref[j]

    rows = tm // FFN_ROW_SLABS
    for s in range(FFN_ROW_SLABS):
        xn = xn_ref[s * rows:(s + 1) * rows]
        ha_ref[halo + s * rows:halo + (s + 1) * rows] = jnp.dot(xn, wa_ref[...], preferred_element_type=F32)
        hb_ref[halo + s * rows:halo + (s + 1) * rows] = jnp.dot(xn, wb_ref[...], preferred_element_type=F32)
    ta_ref[j] = ha_ref[tm:tm + halo]
    tb_ref[j] = hb_ref[tm:tm + halo]

    def conv(h_ref, cw_ref, cb_ref, r0):
        out = cb_ref[...]
        for tap in range(CONV_WIDTH):
            start = r0 + halo - (CONV_WIDTH - 1) + tap
            out = out + h_ref[start:start + rows] * cw_ref[tap:tap + 1]
        return out

    for s in range(FFN_ROW_SLABS):
        a = conv(ha_ref, cwa_ref, cba_ref, s * rows)
        b = conv(hb_ref, cwb_ref, cbb_ref, s * rows)
        act = (a * jax.nn.sigmoid(a) * b).astype(BF16)
        o_ref[s * rows:(s + 1) * rows] += jnp.dot(act, wd_ref[...], preferred_element_type=F32)

    @pl.when(j == pl.num_programs(1) - 1)
    def _():
        o_ref[...] = _rms(x_ref[...] + o_ref[...], gf_ref[...], EPS)


def _conv_ffn(x, gain, w_up, conv_w, conv_b, w_down, gain_final, seq):
    m, d = x.shape
    d_ff = w_down.shape[0]
    tm, tf = FFN_ROW_TILE, COL_TILE
    nj = d_ff // tf
    halo = V7X_SUBLANES
    return pl.pallas_call(
        functools.partial(_conv_ffn_kernel, tiles_per_seq=seq // tm),
        grid=(m // tm, nj),
        in_specs=[pl.BlockSpec((tm, d), lambda i, j: (i, 0)),
                  pl.BlockSpec((1, d), lambda i, j: (0, 0)),
                  pl.BlockSpec((d, tf), lambda i, j: (0, j)),
                  pl.BlockSpec((d, tf), lambda i, j: (0, j + nj)),
                  pl.BlockSpec((CONV_WIDTH, tf), lambda i, j: (0, j)),
                  pl.BlockSpec((CONV_WIDTH, tf), lambda i, j: (0, j + nj)),
                  pl.BlockSpec((1, tf), lambda i, j: (0, j)),
                  pl.BlockSpec((1, tf), lambda i, j: (0, j + nj)),
                  pl.BlockSpec((tf, d), lambda i, j: (j, 0)),
                  pl.BlockSpec((1, d), lambda i, j: (0, 0))],
        out_specs=pl.BlockSpec((tm, d), lambda i, j: (i, 0)),
        out_shape=jax.ShapeDtypeStruct((m, d), F32),
        scratch_shapes=[pltpu.VMEM((tm, d), BF16),
                        pltpu.VMEM((tm + halo, tf), F32),
                        pltpu.VMEM((tm + halo, tf), F32),
                        pltpu.VMEM((nj, halo, tf), F32),
                        pltpu.VMEM((nj, halo, tf), F32)],
        compiler_params=_params(2),
        name="conv_ffn",
    )(x, gain.reshape(1, d), w_up, w_up, conv_w, conv_w, conv_b.reshape(1, -1), conv_b.reshape(1, -1),
      w_down, gain_final.reshape(1, d))


def kernel(x, mem, rel_bias, norm_mix, w_in, ssm_lambda_re, ssm_lambda_im, ssm_log_dt, ssm_b_re, ssm_b_im, ssm_c_re, ssm_c_im, ssm_d, ssm_w_glu, ssm_b_glu, w_branch_a, w_branch_b, w_out, norm_cross, norm_mem, w_cross_q, w_cross_kv, w_cross_o, norm_ffn, w_up, ffn_conv_w, ffn_conv_b, w_down, norm_final):
    batch, seq, d_model = x.shape
    mem_len = mem.shape[1]
    depth = w_in.shape[0]
    assert depth == 1, "the final rmsnorm is fused into the last layer's ConvFFN kernel"
    assert seq % ROW_TILE == 0 and seq % KEY_CHUNK == 0 and seq >= 4 * TOPK_MAX
    m = batch * seq
    xf = x.reshape(m, d_model)
    memf = mem.reshape(batch * mem_len, d_model)
    bias_tiles = _bias_tiles(rel_bias)

    splits = (ATTN_WIDTH, KV_WIDTH, KV_WIDTH, IDX_HEADS * IDX_DIM, IDX_DIM, IDX_HEADS, SSM_WIDTH, d_model, d_model)
    offs = [int(o) for o in np.cumsum((0,) + splits)]
    for l in range(depth):
        w = w_in[l]
        wq, wk, wv, wqi, wki, wwi, wss, wga, wgb = [w[:, offs[n]:offs[n + 1]] for n in range(len(splits))]
        pad = jnp.zeros((d_model, V7X_LANES - IDX_DIM - IDX_HEADS), w.dtype)
        w_t = jnp.concatenate([wq, wqi, wv, wwi], axis=1).T.astype(BF16)
        w_n = jnp.concatenate([wk, wki, wwi, pad], axis=1).astype(BF16)
        w_gate = jnp.concatenate([wga, wgb], axis=1).astype(BF16)

        q_t, v_t, wi_t, k_n = _proj_attn(xf, norm_mix[l], w_t, w_n)
        us = _norm_matmul(xf, norm_mix[l], wss.astype(BF16), F32, COL_TILE, "proj_ssm")
        gates = _norm_matmul(xf, norm_mix[l], w_gate, BF16, COL_TILE, "proj_gates", sigmoid=True)

        y_a = _sparse_attention(bias_tiles, q_t, v_t, wi_t, k_n, batch, seq)

        bre_bd = _block_diag(jnp.transpose(ssm_b_re[l], (0, 2, 1)))
        bim_bd = _block_diag(jnp.transpose(ssm_b_im[l], (0, 2, 1)))
        cre_bd = _block_diag(jnp.transpose(ssm_c_re[l], (0, 2, 1))).astype(BF16)
        cim_bd = _block_diag(jnp.transpose(ssm_c_im[l], (0, 2, 1))).astype(BF16)
        log_dt = jnp.broadcast_to(ssm_log_dt[l][:, None], (SSM_GROUPS, SSM_STATE)).reshape(1, SSM_STATES)
        bu_re, bu_im, a_re, a_im = _s5_input(us, ssm_lambda_re[l].reshape(1, SSM_STATES),
                                             ssm_lambda_im[l].reshape(1, SSM_STATES), log_dt, bre_bd, bim_bd)
        state_rows = SSM_STATES // V7X_LANES
        sshape = (batch, seq, state_rows, V7X_LANES)
        xs_re, xs_im = _s5_scan(a_re.reshape(state_rows, V7X_LANES), a_im.reshape(state_rows, V7X_LANES),
                                bu_re.reshape(sshape), bu_im.reshape(sshape))
        y_b = _s5_output(xs_re.reshape(m, SSM_STATES), xs_im.reshape(m, SSM_STATES), us, cre_bd, cim_bd,
                         ssm_d[l].reshape(1, SSM_WIDTH), ssm_w_glu[l].astype(BF16),
                         ssm_b_glu[l].reshape(1, SSM_WIDTH))

        kvx = _norm_matmul(memf, norm_mem[l], w_cross_kv[l].astype(BF16), BF16, COL_TILE, "cross_kv")
        xf = _post_mixer(xf, y_a, y_b, gates, w_branch_a[l].astype(BF16), w_branch_b[l].astype(BF16),
                         w_out[l].astype(BF16), norm_cross[l], w_cross_q[l].astype(BF16), kvx,
                         w_cross_o[l].astype(BF16), batch, seq, mem_len)

        xf = _conv_ffn(xf, norm_ffn[l], w_up[l].astype(BF16), ffn_conv_w[l], ffn_conv_b[l],
                       w_down[l].astype(BF16), norm_final, seq)
    return xf.reshape(batch, seq, d_model)
```

```python
import functools
import math

import jax
import jax.numpy as jnp
import numpy as np
from jax import lax
from jax.experimental import pallas as pl
from jax.experimental.pallas import tpu as pltpu

N_HEADS_A = 8
N_KV_A = 2
HEAD_DIM_A = 128
ATTN_WIDTH = N_HEADS_A * HEAD_DIM_A
KV_WIDTH = N_KV_A * HEAD_DIM_A
IDX_HEADS = 16
IDX_DIM = 64
TOPK_MAX = 256
BLOCK_Q = 128
REL_BUCKETS = 32
REL_MAX_DIST = 128
SSM_GROUP = 16
SSM_GROUPS = 32
SSM_STATE = 64
SSM_WIDTH = SSM_GROUP * SSM_GROUPS
SSM_STATES = SSM_GROUPS * SSM_STATE
X_HEADS = 4
X_HEAD_DIM = 128
CONV_WIDTH = 3
EPS = 1e-6

V7X_LANES = 128
V7X_SUBLANES = 8
V7X_VMEM_BYTES = 64 * 1024 * 1024
VMEM_LIMIT_BYTES = V7X_VMEM_BYTES - 8 * 1024 * 1024

GRP = N_HEADS_A // N_KV_A
GRP_LANES = GRP * BLOCK_Q
V_ROWS = HEAD_DIM_A + 16
KEY_CHUNK = 2 * BLOCK_Q
N_BIAS_TILES = 4
MAX_BISECT_ITERS = 64
BISECT_FIRST = 12
BISECT_ROUND = 2
COUNT_ROWS = 64
LOG2_E = math.log2(math.e)
ROW_TILE = 1024
SMALL_ROW_TILE = 512
COL_TILE = 512
GATE_COL_TILE = 1024
FFN_ROW_TILE = 512
FFN_ROW_SLABS = 2
SCAN_CHUNK = 128

F32 = jnp.float32
BF16 = jnp.bfloat16
_NT = (((1,), (1,)), ((), ()))


def _params(n_axes):
    return pltpu.CompilerParams(dimension_semantics=("arbitrary",) * n_axes,
                                vmem_limit_bytes=VMEM_LIMIT_BYTES)


def _rms(x, gain, eps):
    return x * lax.rsqrt(jnp.mean(x * x, axis=-1, keepdims=True) + eps) * gain


def _norm_matmul_kernel(x_ref, g_ref, w_ref, o_ref, xn_ref, *, sigmoid):
    @pl.when(pl.program_id(1) == 0)
    def _():
        xn_ref[...] = _rms(x_ref[...], g_ref[...], EPS).astype(BF16)

    tm = o_ref.shape[0]
    rows = min(tm, SMALL_ROW_TILE) if sigmoid else tm
    for r in range(0, tm, rows):
        y = jnp.dot(xn_ref[r:r + rows], w_ref[...], preferred_element_type=F32)
        if sigmoid:
            y = 0.5 * jnp.tanh(0.5 * y) + 0.5
        o_ref[r:r + rows] = y.astype(o_ref.dtype)


def _norm_matmul(x, gain, w, out_dtype, tn, name, sigmoid=False):
    m, k = x.shape
    n = w.shape[1]
    tm = min(ROW_TILE, m)
    return pl.pallas_call(
        functools.partial(_norm_matmul_kernel, sigmoid=sigmoid),
        grid=(m // tm, n // tn),
        in_specs=[pl.BlockSpec((tm, k), lambda i, j: (i, 0)),
                  pl.BlockSpec((1, k), lambda i, j: (0, 0)),
                  pl.BlockSpec((k, tn), lambda i, j: (0, j))],
        out_specs=pl.BlockSpec((tm, tn), lambda i, j: (i, j)),
        out_shape=jax.ShapeDtypeStruct((m, n), out_dtype),
        scratch_shapes=[pltpu.VMEM((tm, k), BF16)],
        compiler_params=_params(2),
        name=name,
    )(x, gain.reshape(1, k), w)


def _proj_attn_kernel(x_ref, g_ref, wt_ref, wn_ref, qt_ref, vt_ref, wit_ref, kn_ref):
    xn = _rms(x_ref[...], g_ref[...], EPS).astype(BF16)
    res = lax.dot_general(wt_ref[...], xn, _NT, preferred_element_type=F32)
    nq = qt_ref.shape[0]
    qt_ref[0:ATTN_WIDTH] = (res[0:ATTN_WIDTH] * (HEAD_DIM_A ** -0.5 * LOG2_E)).astype(BF16)
    qt_ref[ATTN_WIDTH:nq] = res[ATTN_WIDTH:nq].astype(BF16)
    pad_rows = V_ROWS - HEAD_DIM_A
    ones_row = (lax.broadcasted_iota(jnp.int32, (pad_rows, KEY_CHUNK), 0) == 0).astype(BF16)
    for c in range(vt_ref.shape[0]):
        for g in range(N_KV_A):
            rows = res[nq + g * HEAD_DIM_A:nq + (g + 1) * HEAD_DIM_A, c * KEY_CHUNK:(c + 1) * KEY_CHUNK]
            vt_ref[c, g * V_ROWS:g * V_ROWS + HEAD_DIM_A] = rows.astype(BF16)
            vt_ref[c, g * V_ROWS + HEAD_DIM_A:(g + 1) * V_ROWS] = ones_row
    wit_ref[...] = res[nq + KV_WIDTH:nq + KV_WIDTH + IDX_HEADS]
    kn_ref[...] = jnp.dot(xn, wn_ref[...], preferred_element_type=F32).astype(BF16)


def _proj_attn(x, gain, w_t, w_n):
    m, k = x.shape
    tm = SMALL_ROW_TILE
    nq = ATTN_WIDTH + IDX_HEADS * IDX_DIM
    return pl.pallas_call(
        _proj_attn_kernel,
        grid=(m // tm,),
        in_specs=[pl.BlockSpec((tm, k), lambda i: (i, 0)),
                  pl.BlockSpec((1, k), lambda i: (0, 0)),
                  pl.BlockSpec(w_t.shape, lambda i: (0, 0)),
                  pl.BlockSpec(w_n.shape, lambda i: (0, 0))],
        out_specs=[pl.BlockSpec((nq, tm), lambda i: (0, i)),
                   pl.BlockSpec((tm // KEY_CHUNK, N_KV_A * V_ROWS, KEY_CHUNK), lambda i: (i, 0, 0)),
                   pl.BlockSpec((IDX_HEADS, tm), lambda i: (0, i)),
                   pl.BlockSpec((tm, w_n.shape[1]), lambda i: (i, 0))],
        out_shape=[jax.ShapeDtypeStruct((nq, m), BF16),
                   jax.ShapeDtypeStruct((m // KEY_CHUNK, N_KV_A * V_ROWS, KEY_CHUNK), BF16),
                   jax.ShapeDtypeStruct((IDX_HEADS, m), F32),
                   jax.ShapeDtypeStruct((m, w_n.shape[1]), BF16)],
        compiler_params=_params(1),
        name="proj_attn",
    )(x, gain.reshape(1, k), w_t, w_n)


def _t5_bucket(n):
    max_exact = REL_BUCKETS // 2
    n = jnp.maximum(n, 0)
    nf = jnp.maximum(n, 1).astype(F32)
    large = max_exact + (jnp.log(nf / max_exact) / math.log(REL_MAX_DIST / max_exact)
                         * (REL_BUCKETS - max_exact)).astype(jnp.int32)
    large = jnp.minimum(large, REL_BUCKETS - 1)
    return jnp.where(n < max_exact, n, large)


def _bias_tiles_kernel(rb_ref, o_ref):
    shape = (KEY_CHUNK, BLOCK_Q)
    key = lax.broadcasted_iota(jnp.int32, shape, 0)
    qry = lax.broadcasted_iota(jnp.int32, shape, 1)
    for o in range(N_BIAS_TILES):
        bucket = _t5_bucket(o * BLOCK_Q + qry - key)
        for h in range(N_HEADS_A):
            def pick(b, t, bucket=bucket, h=h):
                return jnp.where(bucket == b, rb_ref[b, h], t)
            tile = lax.fori_loop(0, REL_BUCKETS, pick, jnp.zeros(shape, F32))
            j = h % GRP
            o_ref[o, h // GRP, :, j * BLOCK_Q:(j + 1) * BLOCK_Q] = (tile - rb_ref[REL_BUCKETS - 1, h]) * LOG2_E


def _bias_tiles(rel_bias):
    return pl.pallas_call(
        _bias_tiles_kernel,
        in_specs=[pl.BlockSpec(memory_space=pltpu.SMEM)],
        out_specs=pl.BlockSpec(memory_space=pltpu.VMEM),
        out_shape=jax.ShapeDtypeStruct((N_BIAS_TILES, N_KV_A, KEY_CHUNK, GRP_LANES), F32),
        name="rel_bias_tiles",
    )(rel_bias)


def _sparse_attn_kernel(bias_ref, qt_ref, qit_ref, wit_ref, kw_ref, k_ref, vt_ref, o_ref,
                        qz_ref, qiz_ref, sc_ref, prod_ref, m_ref, l_ref, acc_ref, tri_ref, ties_ref):
    i = pl.program_id(1)
    t0 = i * BLOCK_Q
    n_chunks = (i + 2) // 2
    tile = (KEY_CHUNK, BLOCK_Q)
    row1 = (1, BLOCK_Q)
    k_sel = float(TOPK_MAX)

    @pl.when((pl.program_id(0) == 0) & (i == 0))
    def _():
        qiz_ref[...] = jnp.zeros(qiz_ref.shape, BF16)

    for h in range(IDX_HEADS):
        qiz_ref[0:IDX_DIM, h * BLOCK_Q:(h + 1) * BLOCK_Q] = qit_ref[h * IDX_DIM:(h + 1) * IDX_DIM, :]
    for h in range(N_HEADS_A):
        qz_ref[:, h * BLOCK_Q:(h + 1) * BLOCK_Q] = qt_ref[h * HEAD_DIM_A:(h + 1) * HEAD_DIM_A, :]
    w_rows = wit_ref[...] * ((IDX_HEADS * IDX_DIM) ** -0.5)

    key_row = lax.broadcasted_iota(jnp.int32, tile, 0)
    q_pos = t0 + lax.broadcasted_iota(jnp.int32, tile, 1)

    last = n_chunks - 1
    n_pairs = (n_chunks + 1) // 2

    def idx_products(kc, slot):
        ks = pl.multiple_of(jnp.minimum(kc, last) * KEY_CHUNK, KEY_CHUNK)
        prod_ref[slot] = jnp.dot(kw_ref[pl.ds(ks, KEY_CHUNK), :], qiz_ref[...], preferred_element_type=F32)

    def reduce_scores(kc, slot, carry):
        rmax, rmin = carry
        kc = jnp.minimum(kc, last)
        s = jnp.zeros(tile, F32)
        for h in range(IDX_HEADS):
            d = prod_ref[slot, :, h * BLOCK_Q:(h + 1) * BLOCK_Q]
            s = s + jnp.maximum(d, 0.0) * w_rows[h:h + 1, :]
        causal = (kc * KEY_CHUNK + key_row) <= q_pos
        sc_ref[kc] = jnp.where(causal, s, -jnp.inf)
        rmax = jnp.maximum(rmax, jnp.max(jnp.where(causal, s, -jnp.inf), axis=0, keepdims=True))
        rmin = jnp.minimum(rmin, jnp.min(jnp.where(causal, s, jnp.inf), axis=0, keepdims=True))
        return rmax, rmin

    def score_pair(j, carry, prefetch=True):
        idx_products(2 * j + 1, 1)
        carry = reduce_scores(2 * j, 0, carry)
        if prefetch:
            idx_products(2 * j + 2, 0)
        return reduce_scores(2 * j + 1, 1, carry)

    idx_products(0, 0)
    carry = lax.fori_loop(0, n_pairs - 1, score_pair,
                          (jnp.full(row1, -jnp.inf, F32), jnp.full(row1, jnp.inf, F32)))
    rmax, rmin = score_pair(n_pairs - 1, carry, prefetch=False)

    n_causal = (t0 + lax.broadcasted_iota(jnp.int32, row1, 1) + 1).astype(F32)
    short = n_causal <= k_sel

    def count_ge(t):
        def body(kc, c):
            hit = jnp.where(sc_ref[kc] >= t, 1.0, 0.0)
            return c + jnp.sum(hit.reshape(KEY_CHUNK // COUNT_ROWS, COUNT_ROWS, BLOCK_Q), axis=0)
        c = lax.fori_loop(0, n_chunks, body, jnp.zeros((COUNT_ROWS, BLOCK_Q), F32))
        return jnp.sum(c, axis=0, keepdims=True)

    def halve(_, carry):
        lo, hi, c_lo, _ = carry
        mid = lo + (hi - lo) * 0.5
        cnt = count_ge(mid)
        enough = cnt >= k_sel
        shrinks = jnp.where((mid > lo) & (mid < hi), 1.0, 0.0)
        return jnp.where(enough, mid, lo), jnp.where(enough, hi, mid), jnp.where(enough, cnt, c_lo), shrinks

    def halvings(count, lo, hi, c_lo):
        lo, hi, c_lo, shrinks = lax.fori_loop(0, count, halve, (lo, hi, c_lo, jnp.ones(row1, F32)))
        settled = short | (c_lo == k_sel) | (shrinks == 0.0)
        return jnp.sum(jnp.where(settled, 0.0, 1.0)), lo, hi, c_lo

    def pending(carry):
        done, todo, _, _, _ = carry
        return (done < MAX_BISECT_ITERS) & (todo > 0.0)

    def bisect_round(carry):
        done, _, lo, hi, c_lo = carry
        return (done + BISECT_ROUND,) + halvings(BISECT_ROUND, lo, hi, c_lo)

    hi0 = rmax + jnp.maximum(jnp.abs(rmax) * 2.0 ** -22, jnp.finfo(F32).tiny)
    first = (jnp.int32(BISECT_FIRST),) + halvings(BISECT_FIRST, rmin, hi0, n_causal)
    _, _, lo, hi, c_lo = lax.while_loop(pending, bisect_round, first)
    thr = jnp.where(short, jnp.finfo(F32).min, lo)
    tied = jnp.logical_not(short) & (c_lo > k_sel)
    n_tied = jnp.sum(jnp.where(tied, 1.0, 0.0))

    m_ref[...] = jnp.full(m_ref.shape, -1e30, F32)
    l_ref[...] = jnp.zeros(l_ref.shape, F32)
    acc_ref[...] = jnp.zeros(acc_ref.shape, F32)

    def qk_products(kc, slot):
        ks = pl.multiple_of(jnp.minimum(kc, last) * KEY_CHUNK, KEY_CHUNK)
        for g in range(N_KV_A):
            k_c = k_ref[pl.ds(ks, KEY_CHUNK), g * HEAD_DIM_A:(g + 1) * HEAD_DIM_A]
            prod_ref[slot, :, g * GRP_LANES:(g + 1) * GRP_LANES] = jnp.dot(
                k_c, qz_ref[:, g * GRP_LANES:(g + 1) * GRP_LANES], preferred_element_type=F32)

    def softmax_pv(kc, slot, near, ties, upper=None, quota=None):
        live = (kc <= last) if near else True
        kc = jnp.minimum(kc, last) if near else kc
        sc = sc_ref[kc]
        if ties:
            above = sc >= upper
            tie = (sc >= thr) & jnp.logical_not(above)
            tie_f = jnp.where(tie, 1.0, 0.0)
            upto = jnp.dot(tri_ref[...], tie_f.astype(BF16), preferred_element_type=F32)
            before = ties_ref[...]
            keep = (above | (tie & (before + upto - tie_f < quota))) & live
            ties_ref[...] = before + jnp.where(live, upto[KEY_CHUNK - 1:KEY_CHUNK], 0.0)
        else:
            keep = (sc >= thr) & live
        mask = jnp.where(keep, 0.0, -jnp.inf)
        mask = jnp.concatenate([mask] * GRP, axis=1)
        for g in range(N_KV_A):
            lg = prod_ref[slot, :, g * GRP_LANES:(g + 1) * GRP_LANES]
            if near:
                lg = lg + bias_ref[jnp.minimum(i - 2 * kc, N_BIAS_TILES - 1), g]
            lg = lg + mask
            m_old = m_ref[g]
            m_new = jnp.maximum(m_old, jnp.max(lg, axis=0, keepdims=True))
            alpha = jnp.exp2(m_old - m_new)
            p = jnp.exp2(lg - m_new)
            pv = jnp.dot(vt_ref[kc, g * V_ROWS:(g + 1) * V_ROWS, :], p.astype(BF16),
                         preferred_element_type=F32)
            l_ref[g] = alpha * l_ref[g] + pv[HEAD_DIM_A:HEAD_DIM_A + 1]
            acc_ref[g] = alpha * acc_ref[g] + pv[0:HEAD_DIM_A]
            m_ref[g] = m_new

    def attend(near, ties, prefetch=True, **tie_args):
        def body(j, carry):
            qk_products(2 * j + 1, 1)
            softmax_pv(2 * j, 0, near, ties, **tie_args)
            if prefetch:
                qk_products(2 * j + 2, 0)
            softmax_pv(2 * j + 1, 1, near, ties, **tie_args)
            return carry
        return body

    def attend_all(ties, **tie_args):
        lax.fori_loop(0, n_far, attend(False, ties, **tie_args), 0)
        lax.fori_loop(n_far, n_pairs - 1, attend(True, ties, **tie_args), 0)
        attend(True, ties, prefetch=False, **tie_args)(n_pairs - 1, 0)

    n_far = jnp.maximum(n_pairs - 2, 0)
    qk_products(0, 0)

    @pl.when(n_tied == 0.0)
    def _():
        attend_all(False)

    @pl.when(n_tied > 0.0)
    def _():
        row = lax.broadcasted_iota(jnp.int32, (KEY_CHUNK, KEY_CHUNK), 0)
        col = lax.broadcasted_iota(jnp.int32, (KEY_CHUNK, KEY_CHUNK), 1)
        tri_ref[...] = (col <= row).astype(BF16)
        ties_ref[...] = jnp.zeros(ties_ref.shape, F32)
        attend_all(True, upper=jnp.where(tied, hi, thr), quota=jnp.where(tied, k_sel - count_ge(hi), 0.0))

    for g in range(N_KV_A):
        out_t = acc_ref[g] / l_ref[g]
        for j in range(GRP):
            h = g * GRP + j
            o_ref[:, h * HEAD_DIM_A:(h + 1) * HEAD_DIM_A] = (
                out_t[:, j * BLOCK_Q:(j + 1) * BLOCK_Q].T.astype(o_ref.dtype))


def _sparse_attention(bias_tiles, q_t, v_t, wi_t, k_n, batch, seq):
    nb = seq // BLOCK_Q
    n_seq_chunks = seq // KEY_CHUNK
    kw_blk = KV_WIDTH // V7X_LANES
    return pl.pallas_call(
        _sparse_attn_kernel,
        grid=(batch, nb),
        in_specs=[pl.BlockSpec(bias_tiles.shape, lambda b, i: (0, 0, 0, 0)),
                  pl.BlockSpec((ATTN_WIDTH, BLOCK_Q), lambda b, i: (0, b * nb + i)),
                  pl.BlockSpec((IDX_HEADS * IDX_DIM, BLOCK_Q), lambda b, i: (1, b * nb + i)),
                  pl.BlockSpec((IDX_HEADS, BLOCK_Q), lambda b, i: (0, b * nb + i)),
                  pl.BlockSpec((seq, V7X_LANES), lambda b, i: (b, kw_blk)),
                  pl.BlockSpec((seq, KV_WIDTH), lambda b, i: (b, 0)),
                  pl.BlockSpec((n_seq_chunks, N_KV_A * V_ROWS, KEY_CHUNK), lambda b, i: (b, 0, 0))],
        out_specs=pl.BlockSpec((BLOCK_Q, ATTN_WIDTH), lambda b, i: (b * nb + i, 0)),
        out_shape=jax.ShapeDtypeStruct((batch * seq, ATTN_WIDTH), BF16),
        scratch_shapes=[pltpu.VMEM((HEAD_DIM_A, N_HEADS_A * BLOCK_Q), BF16),
                        pltpu.VMEM((V7X_LANES, IDX_HEADS * BLOCK_Q), BF16),
                        pltpu.VMEM((n_seq_chunks, KEY_CHUNK, BLOCK_Q), F32),
                        pltpu.VMEM((2, KEY_CHUNK, IDX_HEADS * BLOCK_Q), F32),
                        pltpu.VMEM((N_KV_A, 1, GRP_LANES), F32),
                        pltpu.VMEM((N_KV_A, 1, GRP_LANES), F32),
                        pltpu.VMEM((N_KV_A, HEAD_DIM_A, GRP_LANES), F32),
                        pltpu.VMEM((KEY_CHUNK, KEY_CHUNK), BF16),
                        pltpu.VMEM((1, BLOCK_Q), F32)],
        compiler_params=_params(2),
        name="sparse_attention",
    )(bias_tiles, q_t, q_t, wi_t, k_n, k_n, v_t)


def _s5_input_kernel(u_ref, lr_ref, li_ref, ldt_ref, bre_ref, bim_ref,
                     bur_ref, bui_ref, ar_ref, ai_ref, bbr_ref, bbi_ref):
    @pl.when(pl.program_id(0) == 0)
    def _():
        dt = jnp.exp(ldt_ref[...])
        lr = lr_ref[...]
        li = li_ref[...]
        mag = jnp.exp(lr * dt)
        ar = mag * jnp.cos(li * dt)
        ai = mag * jnp.sin(li * dt)
        den = lr * lr + li * li
        nr = ar - 1.0
        cr = (nr * lr + ai * li) / den
        ci = (ai * lr - nr * li) / den
        bre = bre_ref[...]
        bim = bim_ref[...]
        bbr_ref[...] = (cr * bre - ci * bim).astype(BF16)
        bbi_ref[...] = (cr * bim + ci * bre).astype(BF16)
        ar_ref[...] = ar
        ai_ref[...] = ai

    u = u_ref[...].astype(BF16)
    bur_ref[...] = jnp.dot(u, bbr_ref[...], preferred_element_type=F32)
    bui_ref[...] = jnp.dot(u, bbi_ref[...], preferred_element_type=F32)


def _s5_input(us, lam_re, lam_im, log_dt, bre_bd, bim_bd):
    m = us.shape[0]
    tm = SMALL_ROW_TILE
    row = pl.BlockSpec((1, SSM_STATES), lambda i: (0, 0))
    mat = pl.BlockSpec((SSM_WIDTH, SSM_STATES), lambda i: (0, 0))
    out = pl.BlockSpec((tm, SSM_STATES), lambda i: (i, 0))
    return pl.pallas_call(
        _s5_input_kernel,
        grid=(m // tm,),
        in_specs=[pl.BlockSpec((tm, SSM_WIDTH), lambda i: (i, 0)), row, row, row, mat, mat],
        out_specs=[out, out, row, row],
        out_shape=[jax.ShapeDtypeStruct((m, SSM_STATES), F32)] * 2
        + [jax.ShapeDtypeStruct((1, SSM_STATES), F32)] * 2,
        scratch_shapes=[pltpu.VMEM((SSM_WIDTH, SSM_STATES), BF16)] * 2,
        compiler_params=_params(1),
        name="s5_input",
    )(us, lam_re, lam_im, log_dt, bre_bd, bim_bd)


def _s5_scan_kernel(ar_ref, ai_ref, br_ref, bi_ref, xr_ref, xi_ref, sr_ref, si_ref):
    @pl.when(pl.program_id(0) == 0)
    def _():
        sr_ref[...] = jnp.zeros(sr_ref.shape, F32)
        si_ref[...] = jnp.zeros(si_ref.shape, F32)

    ar = ar_ref[...]
    ai = ai_ref[...]
    steps = br_ref.shape[1]

    def step(t, carry):
        xr, xi = carry
        nr = ar * xr - ai * xi + br_ref[:, t]
        ni = ar * xi + ai * xr + bi_ref[:, t]
        xr_ref[:, t] = nr.astype(xr_ref.dtype)
        xi_ref[:, t] = ni.astype(xi_ref.dtype)
        return nr, ni

    xr, xi = lax.fori_loop(0, steps, step, (sr_ref[...], si_ref[...]), unroll=8)
    sr_ref[...] = xr
    si_ref[...] = xi


def _s5_scan(a_re, a_im, bu_re, bu_im):
    batch, seq, rows, lanes = bu_re.shape
    blk = pl.BlockSpec((batch, SCAN_CHUNK, rows, lanes), lambda c: (0, c, 0, 0))
    par = pl.BlockSpec((rows, lanes), lambda c: (0, 0))
    return pl.pallas_call(
        _s5_scan_kernel,
        grid=(seq // SCAN_CHUNK,),
        in_specs=[par, par, blk, blk],
        out_specs=[blk, blk],
        out_shape=[jax.ShapeDtypeStruct(bu_re.shape, BF16)] * 2,
        scratch_shapes=[pltpu.VMEM((batch, rows, lanes), F32)] * 2,
        compiler_params=_params(1),
        name="s5_scan",
    )(a_re, a_im, bu_re, bu_im)


def _s5_output_kernel(xr_ref, xi_ref, u_ref, cr_ref, ci_ref, d_ref, wg_ref, bg_ref, o_ref):
    y = (jnp.dot(xr_ref[...], cr_ref[...], preferred_element_type=F32)
         - jnp.dot(xi_ref[...], ci_ref[...], preferred_element_type=F32)
         + d_ref[...] * u_ref[...])
    g = jax.nn.gelu(y)
    gate = jax.nn.sigmoid(jnp.dot(g.astype(BF16), wg_ref[...], preferred_element_type=F32) + bg_ref[...])
    o_ref[...] = (g * gate).astype(o_ref.dtype)


def _s5_output(xr, xi, us, cre_bd, cim_bd, d_row, w_glu, b_glu):
    m = us.shape[0]
    tm = SMALL_ROW_TILE
    state = pl.BlockSpec((tm, SSM_STATES), lambda i: (i, 0))
    chan = pl.BlockSpec((tm, SSM_WIDTH), lambda i: (i, 0))
    cmat = pl.BlockSpec((SSM_STATES, SSM_WIDTH), lambda i: (0, 0))
    row = pl.BlockSpec((1, SSM_WIDTH), lambda i: (0, 0))
    return pl.pallas_call(
        _s5_output_kernel,
        grid=(m // tm,),
        in_specs=[state, state, chan, cmat, cmat, row,
                  pl.BlockSpec((SSM_WIDTH, SSM_WIDTH), lambda i: (0, 0)), row],
        out_specs=chan,
        out_shape=jax.ShapeDtypeStruct((m, SSM_WIDTH), BF16),
        compiler_params=_params(1),
        name="s5_output",
    )(xr, xi, us, cre_bd, cim_bd, d_row, w_glu, b_glu)


def _block_diag(blocks):
    g, r, c = blocks.shape
    eye = jnp.eye(g, dtype=blocks.dtype)
    return (eye[:, None, :, None] * blocks[:, :, None, :]).reshape(g * r, g * c)


def _post_mixer_kernel(x_ref, ya_ref, yb_ref, ga_ref, gb_ref, wa_ref, wb_ref, wo_ref, gc_ref, wq_ref,
                       kv_ref, wco_ref, o_ref):
    width = X_HEADS * X_HEAD_DIM
    scale = X_HEAD_DIM ** -0.5
    za = jnp.dot(ya_ref[...], wa_ref[...], preferred_element_type=F32)
    zb = jnp.dot(yb_ref[...], wb_ref[...], preferred_element_type=F32)
    z = (ga_ref[...].astype(F32) * za + gb_ref[...].astype(F32) * zb).astype(BF16)
    x1 = x_ref[...] + jnp.dot(z, wo_ref[...], preferred_element_type=F32)
    xn = _rms(x1, gc_ref[...], EPS).astype(BF16)
    q = jnp.dot(xn, wq_ref[...], preferred_element_type=F32).astype(BF16)
    heads = []
    for h in range(X_HEADS):
        cols = slice(h * X_HEAD_DIM, (h + 1) * X_HEAD_DIM)
        vcols = slice(width + h * X_HEAD_DIM, width + (h + 1) * X_HEAD_DIM)
        lg = lax.dot_general(q[:, cols], kv_ref[:, cols], _NT, preferred_element_type=F32) * scale
        p = jnp.exp(lg - jnp.max(lg, axis=-1, keepdims=True))
        o = jnp.dot(p.astype(BF16), kv_ref[:, vcols], preferred_element_type=F32)
        heads.append((o / jnp.sum(p, axis=-1, keepdims=True)).astype(BF16))
    attn = jnp.concatenate(heads, axis=1)
    o_ref[...] = x1 + jnp.dot(attn, wco_ref[...], preferred_element_type=F32)


def _post_mixer(x, ya, yb, gates, w_a, w_b, w_out, gain_cross, w_q, kv, w_co, batch, seq, mem_len):
    d = x.shape[1]
    width = X_HEADS * X_HEAD_DIM
    tm = SMALL_ROW_TILE
    nt = seq // tm
    row = pl.BlockSpec((tm, d), lambda b, i: (b * nt + i, 0))
    return pl.pallas_call(
        _post_mixer_kernel,
        grid=(batch, nt),
        in_specs=[row,
                  pl.BlockSpec((tm, ya.shape[1]), lambda b, i: (b * nt + i, 0)),
                  pl.BlockSpec((tm, yb.shape[1]), lambda b, i: (b * nt + i, 0)),
                  row,
                  pl.BlockSpec((tm, d), lambda b, i: (b * nt + i, 1)),
                  pl.BlockSpec(w_a.shape, lambda b, i: (0, 0)),
                  pl.BlockSpec(w_b.shape, lambda b, i: (0, 0)),
                  pl.BlockSpec((d, d), lambda b, i: (0, 0)),
                  pl.BlockSpec((1, d), lambda b, i: (0, 0)),
                  pl.BlockSpec((d, width), lambda b, i: (0, 0)),
                  pl.BlockSpec((mem_len, 2 * width), lambda b, i: (b, 0)),
                  pl.BlockSpec((width, d), lambda b, i: (0, 0))],
        out_specs=row,
        out_shape=jax.ShapeDtypeStruct(x.shape, F32),
        compiler_params=_params(2),
        name="post_mixer",
    )(x, ya, yb, gates, gates, w_a, w_b, w_out, gain_cross.reshape(1, d), w_q, kv, w_co)


def _conv_ffn_kernel(x_ref, g_ref, wa_ref, wb_ref, cwa_ref, cwb_ref, cba_ref, cbb_ref, wd_ref, gf_ref,
                     o_ref, xn_ref, ha_ref, hb_ref, ta_ref, tb_ref, *, tiles_per_seq):
    i = pl.program_id(0)
    j = pl.program_id(1)
    tm = x_ref.shape[0]
    halo = V7X_SUBLANES

    @pl.when(j == 0)
    def _():
        xn_ref[...] = _rms(x_ref[...], g_ref[...], EPS).astype(BF16)
        o_ref[...] = jnp.zeros(o_ref.shape, F32)

    @pl.when(i % tiles_per_seq == 0)
    def _():
        ha_ref[0:halo] = jnp.zeros((halo, ha_ref.shape[1]), F32)
        hb_ref[0:halo] = jnp.zeros((halo, hb_ref.shape[1]), F32)

    @pl.when(i % tiles_per_seq != 0)
    def _():
        ha_ref[0:halo] = ta_ref[j]
        hb_ref[0:halo] = tb_ref[j]

    rows = tm // FFN_ROW_SLABS
    for s in range(FFN_ROW_SLABS):
        xn = xn_ref[s * rows:(s + 1) * rows]
        ha_ref[halo + s * rows:halo + (s + 1) * rows] = jnp.dot(xn, wa_ref[...], preferred_element_type=F32)
        hb_ref[halo + s * rows:halo + (s + 1) * rows] = jnp.dot(xn, wb_ref[...], preferred_element_type=F32)
    ta_ref[j] = ha_ref[tm:tm + halo]
    tb_ref[j] = hb_ref[tm:tm + halo]

    def conv(h_ref, cw_ref, cb_ref, r0):
        out = cb_ref[...]
        for tap in range(CONV_WIDTH):
            start = r0 + halo - (CONV_WIDTH - 1) + tap
            out = out + h_ref[start:start + rows] * cw_ref[tap:tap + 1]
        return out

    for s in range(FFN_ROW_SLABS):
        a = conv(ha_ref, cwa_ref, cba_ref, s * rows)
        b = conv(hb_ref, cwb_ref, cbb_ref, s * rows)
        act = (a * jax.nn.sigmoid(a) * b).astype(BF16)
        o_ref[s * rows:(s + 1) * rows] += jnp.dot(act, wd_ref[...], preferred_element_type=F32)

    @pl.when(j == pl.num_programs(1) - 1)
    def _():
        o_ref[...] = _rms(x_ref[...] + o_ref[...], gf_ref[...], EPS)


def _conv_ffn(x, gain, w_up, conv_w, conv_b, w_down, gain_final, seq):
    m, d = x.shape
    d_ff = w_down.shape[0]
    tm, tf = FFN_ROW_TILE, COL_TILE
    nj = d_ff // tf
    halo = V7X_SUBLANES
    return pl.pallas_call(
        functools.partial(_conv_ffn_kernel, tiles_per_seq=seq // tm),
        grid=(m // tm, nj),
        in_specs=[pl.BlockSpec((tm, d), lambda i, j: (i, 0)),
                  pl.BlockSpec((1, d), lambda i, j: (0, 0)),
                  pl.BlockSpec((d, tf), lambda i, j: (0, j)),
                  pl.BlockSpec((d, tf), lambda i, j: (0, j + nj)),
                  pl.BlockSpec((CONV_WIDTH, tf), lambda i, j: (0, j)),
                  pl.BlockSpec((CONV_WIDTH, tf), lambda i, j: (0, j + nj)),
                  pl.BlockSpec((1, tf), lambda i, j: (0, j)),
                  pl.BlockSpec((1, tf), lambda i, j: (0, j + nj)),
                  pl.BlockSpec((tf, d), lambda i, j: (j, 0)),
                  pl.BlockSpec((1, d), lambda i, j: (0, 0))],
        out_specs=pl.BlockSpec((tm, d), lambda i, j: (i, 0)),
        out_shape=jax.ShapeDtypeStruct((m, d), F32),
        scratch_shapes=[pltpu.VMEM((tm, d), BF16),
                        pltpu.VMEM((tm + halo, tf), F32),
                        pltpu.VMEM((tm + halo, tf), F32),
                        pltpu.VMEM((nj, halo, tf), F32),
                        pltpu.VMEM((nj, halo, tf), F32)],
        compiler_params=_params(2),
        name="conv_ffn",
    )(x, gain.reshape(1, d), w_up, w_up, conv_w, conv_w, conv_b.reshape(1, -1), conv_b.reshape(1, -1),
      w_down, gain_final.reshape(1, d))


def kernel(x, mem, rel_bias, norm_mix, w_in, ssm_lambda_re, ssm_lambda_im, ssm_log_dt, ssm_b_re, ssm_b_im, ssm_c_re, ssm_c_im, ssm_d, ssm_w_glu, ssm_b_glu, w_branch_a, w_branch_b, w_out, norm_cross, norm_mem, w_cross_q, w_cross_kv, w_cross_o, norm_ffn, w_up, ffn_conv_w, ffn_conv_b, w_down, norm_final):
    batch, seq, d_model = x.shape
    mem_len = mem.shape[1]
    depth = w_in.shape[0]
    assert depth == 1, "the final rmsnorm is fused into the last layer's ConvFFN kernel"
    assert seq % ROW_TILE == 0 and seq % KEY_CHUNK == 0 and seq >= 4 * TOPK_MAX
    m = batch * seq
    xf = x.reshape(m, d_model)
    memf = mem.reshape(batch * mem_len, d_model)
    bias_tiles = _bias_tiles(rel_bias)

    splits = (ATTN_WIDTH, KV_WIDTH, KV_WIDTH, IDX_HEADS * IDX_DIM, IDX_DIM, IDX_HEADS, SSM_WIDTH, d_model, d_model)
    offs = [int(o) for o in np.cumsum((0,) + splits)]
    for l in range(depth):
        w = w_in[l]
        wq, wk, wv, wqi, wki, wwi, wss, wga, wgb = [w[:, offs[n]:offs[n + 1]] for n in range(len(splits))]
        pad = jnp.zeros((d_model, V7X_LANES - IDX_DIM - IDX_HEADS), w.dtype)
        w_t = jnp.concatenate([wq, wqi, wv, wwi], axis=1).T.astype(BF16)
        w_n = jnp.concatenate([wk, wki, wwi, pad], axis=1).astype(BF16)
        w_gate = jnp.concatenate([wga, wgb], axis=1).astype(BF16)

        q_t, v_t, wi_t, k_n = _proj_attn(xf, norm_mix[l], w_t, w_n)
        us = _norm_matmul(xf, norm_mix[l], wss.astype(BF16), F32, COL_TILE, "proj_ssm")
        gates = _norm_matmul(xf, norm_mix[l], w_gate, BF16, GATE_COL_TILE, "proj_gates", sigmoid=True)

        y_a = _sparse_attention(bias_tiles, q_t, v_t, wi_t, k_n, batch, seq)

        bre_bd = _block_diag(jnp.transpose(ssm_b_re[l], (0, 2, 1)))
        bim_bd = _block_diag(jnp.transpose(ssm_b_im[l], (0, 2, 1)))
        cre_bd = _block_diag(jnp.transpose(ssm_c_re[l], (0, 2, 1))).astype(BF16)
        cim_bd = _block_diag(jnp.transpose(ssm_c_im[l], (0, 2, 1))).astype(BF16)
        log_dt = jnp.broadcast_to(ssm_log_dt[l][:, None], (SSM_GROUPS, SSM_STATE)).reshape(1, SSM_STATES)
        bu_re, bu_im, a_re, a_im = _s5_input(us, ssm_lambda_re[l].reshape(1, SSM_STATES),
                                             ssm_lambda_im[l].reshape(1, SSM_STATES), log_dt, bre_bd, bim_bd)
        state_rows = SSM_STATES // V7X_LANES
        sshape = (batch, seq, state_rows, V7X_LANES)
        xs_re, xs_im = _s5_scan(a_re.reshape(state_rows, V7X_LANES), a_im.reshape(state_rows, V7X_LANES),
                                bu_re.reshape(sshape), bu_im.reshape(sshape))
        y_b = _s5_output(xs_re.reshape(m, SSM_STATES), xs_im.reshape(m, SSM_STATES), us, cre_bd, cim_bd,
                         ssm_d[l].reshape(1, SSM_WIDTH), ssm_w_glu[l].astype(BF16),
                         ssm_b_glu[l].reshape(1, SSM_WIDTH))

        kvx = _norm_matmul(memf, norm_mem[l], w_cross_kv[l].astype(BF16), BF16, COL_TILE, "cross_kv")
        xf = _post_mixer(xf, y_a, y_b, gates, w_branch_a[l].astype(BF16), w_branch_b[l].astype(BF16),
                         w_out[l].astype(BF16), norm_cross[l], w_cross_q[l].astype(BF16), kvx,
                         w_cross_o[l].astype(BF16), batch, seq, mem_len)

        xf = _conv_ffn(xf, norm_ffn[l], w_up[l].astype(BF16), ffn_conv_w[l], ffn_conv_b[l],
                       w_down[l].astype(BF16), norm_final, seq)
    return xf.reshape(batch, seq, d_model)
```

```python
import functools
import math

import jax
import jax.numpy as jnp
import numpy as np
from jax import lax
from jax.experimental import pallas as pl
from jax.experimental.pallas import tpu as pltpu

N_HEADS_A = 8
N_KV_A = 2
HEAD_DIM_A = 128
ATTN_WIDTH = N_HEADS_A * HEAD_DIM_A
KV_WIDTH = N_KV_A * HEAD_DIM_A
IDX_HEADS = 16
IDX_DIM = 64
TOPK_MAX = 256
BLOCK_Q = 128
REL_BUCKETS = 32
REL_MAX_DIST = 128
SSM_GROUP = 16
SSM_GROUPS = 32
SSM_STATE = 64
SSM_WIDTH = SSM_GROUP * SSM_GROUPS
SSM_STATES = SSM_GROUPS * SSM_STATE
X_HEADS = 4
X_HEAD_DIM = 128
CONV_WIDTH = 3
EPS = 1e-6

V7X_LANES = 128
V7X_SUBLANES = 8
V7X_VMEM_BYTES = 64 * 1024 * 1024
VMEM_LIMIT_BYTES = V7X_VMEM_BYTES - 8 * 1024 * 1024

GRP = N_HEADS_A // N_KV_A
GRP_LANES = GRP * BLOCK_Q
V_ROWS = HEAD_DIM_A + 16
KEY_CHUNK = 2 * BLOCK_Q
N_BIAS_TILES = 4
MAX_BISECT_ITERS = 64
BISECT_FIRST = 12
BISECT_ROUND = 2
COUNT_ROWS = 64
LOG2_E = math.log2(math.e)
ROW_TILE = 1024
SMALL_ROW_TILE = 512
COL_TILE = 512
GATE_COL_TILE = 1024
FFN_ROW_TILE = 512
FFN_ROW_SLABS = 2
SCAN_CHUNK = 128

F32 = jnp.float32
BF16 = jnp.bfloat16
_NT = (((1,), (1,)), ((), ()))


def _params(n_axes):
    return pltpu.CompilerParams(dimension_semantics=("arbitrary",) * n_axes,
                                vmem_limit_bytes=VMEM_LIMIT_BYTES)


def _rms(x, gain, eps):
    return x * lax.rsqrt(jnp.mean(x * x, axis=-1, keepdims=True) + eps) * gain


def _norm_matmul_kernel(x_ref, g_ref, w_ref, o_ref, xn_ref, *, sigmoid):
    @pl.when(pl.program_id(1) == 0)
    def _():
        xn_ref[...] = _rms(x_ref[...], g_ref[...], EPS).astype(BF16)

    tm = o_ref.shape[0]
    rows = min(tm, SMALL_ROW_TILE) if sigmoid else tm
    for r in range(0, tm, rows):
        y = jnp.dot(xn_ref[r:r + rows], w_ref[...], preferred_element_type=F32)
        if sigmoid:
            y = 0.5 * jnp.tanh(0.5 * y) + 0.5
        o_ref[r:r + rows] = y.astype(o_ref.dtype)


def _norm_matmul(x, gain, w, out_dtype, tn, name, sigmoid=False):
    m, k = x.shape
    n = w.shape[1]
    tm = min(ROW_TILE, m)
    return pl.pallas_call(
        functools.partial(_norm_matmul_kernel, sigmoid=sigmoid),
        grid=(m // tm, n // tn),
        in_specs=[pl.BlockSpec((tm, k), lambda i, j: (i, 0)),
                  pl.BlockSpec((1, k), lambda i, j: (0, 0)),
                  pl.BlockSpec((k, tn), lambda i, j: (0, j))],
        out_specs=pl.BlockSpec((tm, tn), lambda i, j: (i, j)),
        out_shape=jax.ShapeDtypeStruct((m, n), out_dtype),
        scratch_shapes=[pltpu.VMEM((tm, k), BF16)],
        compiler_params=_params(2),
        name=name,
    )(x, gain.reshape(1, k), w)


def _proj_attn_kernel(x_ref, g_ref, wt_ref, wn_ref, qt_ref, vt_ref, wit_ref, kn_ref):
    xn = _rms(x_ref[...], g_ref[...], EPS).astype(BF16)
    res = lax.dot_general(wt_ref[...], xn, _NT, preferred_element_type=F32)
    nq = qt_ref.shape[0]
    qt_ref[0:ATTN_WIDTH] = (res[0:ATTN_WIDTH] * (HEAD_DIM_A ** -0.5 * LOG2_E)).astype(BF16)
    qt_ref[ATTN_WIDTH:nq] = res[ATTN_WIDTH:nq].astype(BF16)
    pad_rows = V_ROWS - HEAD_DIM_A
    ones_row = (lax.broadcasted_iota(jnp.int32, (pad_rows, KEY_CHUNK), 0) == 0).astype(BF16)
    for c in range(vt_ref.shape[0]):
        for g in range(N_KV_A):
            rows = res[nq + g * HEAD_DIM_A:nq + (g + 1) * HEAD_DIM_A, c * KEY_CHUNK:(c + 1) * KEY_CHUNK]
            vt_ref[c, g * V_ROWS:g * V_ROWS + HEAD_DIM_A] = rows.astype(BF16)
            vt_ref[c, g * V_ROWS + HEAD_DIM_A:(g + 1) * V_ROWS] = ones_row
    wit_ref[...] = res[nq + KV_WIDTH:nq + KV_WIDTH + IDX_HEADS]
    kn_ref[...] = jnp.dot(xn, wn_ref[...], preferred_element_type=F32).astype(BF16)


def _proj_attn(x, gain, w_t, w_n):
    m, k = x.shape
    tm = SMALL_ROW_TILE
    nq = ATTN_WIDTH + IDX_HEADS * IDX_DIM
    return pl.pallas_call(
        _proj_attn_kernel,
        grid=(m // tm,),
        in_specs=[pl.BlockSpec((tm, k), lambda i: (i, 0)),
                  pl.BlockSpec((1, k), lambda i: (0, 0)),
                  pl.BlockSpec(w_t.shape, lambda i: (0, 0)),
                  pl.BlockSpec(w_n.shape, lambda i: (0, 0))],
        out_specs=[pl.BlockSpec((nq, tm), lambda i: (0, i)),
                   pl.BlockSpec((tm // KEY_CHUNK, N_KV_A * V_ROWS, KEY_CHUNK), lambda i: (i, 0, 0)),
                   pl.BlockSpec((IDX_HEADS, tm), lambda i: (0, i)),
                   pl.BlockSpec((tm, w_n.shape[1]), lambda i: (i, 0))],
        out_shape=[jax.ShapeDtypeStruct((nq, m), BF16),
                   jax.ShapeDtypeStruct((m // KEY_CHUNK, N_KV_A * V_ROWS, KEY_CHUNK), BF16),
                   jax.ShapeDtypeStruct((IDX_HEADS, m), F32),
                   jax.ShapeDtypeStruct((m, w_n.shape[1]), BF16)],
        compiler_params=_params(1),
        name="proj_attn",
    )(x, gain.reshape(1, k), w_t, w_n)


def _t5_bucket(n):
    max_exact = REL_BUCKETS // 2
    n = jnp.maximum(n, 0)
    nf = jnp.maximum(n, 1).astype(F32)
    large = max_exact + (jnp.log(nf / max_exact) / math.log(REL_MAX_DIST / max_exact)
                         * (REL_BUCKETS - max_exact)).astype(jnp.int32)
    large = jnp.minimum(large, REL_BUCKETS - 1)
    return jnp.where(n < max_exact, n, large)


def _bias_tiles_kernel(rb_ref, o_ref):
    shape = (KEY_CHUNK, BLOCK_Q)
    key = lax.broadcasted_iota(jnp.int32, shape, 0)
    qry = lax.broadcasted_iota(jnp.int32, shape, 1)
    for o in range(N_BIAS_TILES):
        bucket = _t5_bucket(o * BLOCK_Q + qry - key)
        for h in range(N_HEADS_A):
            def pick(b, t, bucket=bucket, h=h):
                return jnp.where(bucket == b, rb_ref[b, h], t)
            tile = lax.fori_loop(0, REL_BUCKETS, pick, jnp.zeros(shape, F32))
            j = h % GRP
            o_ref[o, h // GRP, :, j * BLOCK_Q:(j + 1) * BLOCK_Q] = (tile - rb_ref[REL_BUCKETS - 1, h]) * LOG2_E


def _bias_tiles(rel_bias):
    return pl.pallas_call(
        _bias_tiles_kernel,
        in_specs=[pl.BlockSpec(memory_space=pltpu.SMEM)],
        out_specs=pl.BlockSpec(memory_space=pltpu.VMEM),
        out_shape=jax.ShapeDtypeStruct((N_BIAS_TILES, N_KV_A, KEY_CHUNK, GRP_LANES), F32),
        name="rel_bias_tiles",
    )(rel_bias)


def _sparse_attn_kernel(bias_ref, qt_ref, qit_ref, wit_ref, kw_ref, k_ref, vt_ref, o_ref,
                        qz_ref, qiz_ref, sc_ref, prod_ref, m_ref, l_ref, acc_ref, tri_ref, ties_ref):
    i = pl.program_id(1)
    t0 = i * BLOCK_Q
    n_chunks = (i + 2) // 2
    tile = (KEY_CHUNK, BLOCK_Q)
    row1 = (1, BLOCK_Q)
    k_sel = float(TOPK_MAX)

    @pl.when((pl.program_id(0) == 0) & (i == 0))
    def _():
        qiz_ref[...] = jnp.zeros(qiz_ref.shape, BF16)

    for h in range(IDX_HEADS):
        qiz_ref[0:IDX_DIM, h * BLOCK_Q:(h + 1) * BLOCK_Q] = qit_ref[h * IDX_DIM:(h + 1) * IDX_DIM, :]
    for h in range(N_HEADS_A):
        qz_ref[:, h * BLOCK_Q:(h + 1) * BLOCK_Q] = qt_ref[h * HEAD_DIM_A:(h + 1) * HEAD_DIM_A, :]
    w_rows = wit_ref[...] * ((IDX_HEADS * IDX_DIM) ** -0.5)

    key_row = lax.broadcasted_iota(jnp.int32, tile, 0)
    q_pos = t0 + lax.broadcasted_iota(jnp.int32, tile, 1)

    last = n_chunks - 1
    n_pairs = (n_chunks + 1) // 2

    def idx_products(kc, slot):
        ks = pl.multiple_of(jnp.minimum(kc, last) * KEY_CHUNK, KEY_CHUNK)
        prod_ref[slot] = jnp.dot(kw_ref[pl.ds(ks, KEY_CHUNK), :], qiz_ref[...], preferred_element_type=F32)

    def reduce_scores(kc, slot, carry):
        rmax, rmin = carry
        kc = jnp.minimum(kc, last)
        s = jnp.zeros(tile, F32)
        for h in range(IDX_HEADS):
            d = prod_ref[slot, :, h * BLOCK_Q:(h + 1) * BLOCK_Q]
            s = s + jnp.maximum(d, 0.0) * w_rows[h:h + 1, :]
        causal = (kc * KEY_CHUNK + key_row) <= q_pos
        sc_ref[kc] = jnp.where(causal, s, -jnp.inf)
        rmax = jnp.maximum(rmax, jnp.max(jnp.where(causal, s, -jnp.inf), axis=0, keepdims=True))
        rmin = jnp.minimum(rmin, jnp.min(jnp.where(causal, s, jnp.inf), axis=0, keepdims=True))
        return rmax, rmin

    def run_pairs(start, stop, pair_body, carry):
        n_double = (stop - start) // 2

        def double(t, c):
            j = start + 2 * t
            return pair_body(j + 1, pair_body(j, c))

        carry = lax.fori_loop(0, n_double, double, carry)
        return lax.fori_loop(start + 2 * n_double, stop, pair_body, carry)

    def score_pair(j, carry, prefetch=True):
        idx_products(2 * j + 1, 1)
        carry = reduce_scores(2 * j, 0, carry)
        if prefetch:
            idx_products(2 * j + 2, 0)
        return reduce_scores(2 * j + 1, 1, carry)

    idx_products(0, 0)
    carry = lax.fori_loop(0, n_pairs - 1, score_pair,
                          (jnp.full(row1, -jnp.inf, F32), jnp.full(row1, jnp.inf, F32)))
    rmax, rmin = score_pair(n_pairs - 1, carry, prefetch=False)

    n_causal = (t0 + lax.broadcasted_iota(jnp.int32, row1, 1) + 1).astype(F32)
    short = n_causal <= k_sel

    @pl.when(n_chunks % 2 == 1)
    def _():
        sc_ref[n_chunks] = jnp.full(tile, -jnp.inf, F32)

    def count_ge(t):
        def body(j, c):
            for kc in (2 * j, 2 * j + 1):
                hit = jnp.where(sc_ref[kc] >= t, 1.0, 0.0)
                c = c + jnp.sum(hit.reshape(KEY_CHUNK // COUNT_ROWS, COUNT_ROWS, BLOCK_Q), axis=0)
            return c
        c = lax.fori_loop(0, n_pairs, body, jnp.zeros((COUNT_ROWS, BLOCK_Q), F32))
        return jnp.sum(c, axis=0, keepdims=True)

    def halve(_, carry):
        lo, hi, c_lo, _ = carry
        mid = lo + (hi - lo) * 0.5
        cnt = count_ge(mid)
        enough = cnt >= k_sel
        shrinks = jnp.where((mid > lo) & (mid < hi), 1.0, 0.0)
        return jnp.where(enough, mid, lo), jnp.where(enough, hi, mid), jnp.where(enough, cnt, c_lo), shrinks

    def halvings(count, lo, hi, c_lo):
        lo, hi, c_lo, shrinks = lax.fori_loop(0, count, halve, (lo, hi, c_lo, jnp.ones(row1, F32)))
        settled = short | (c_lo == k_sel) | (shrinks == 0.0)
        return jnp.sum(jnp.where(settled, 0.0, 1.0)), lo, hi, c_lo

    def pending(carry):
        done, todo, _, _, _ = carry
        return (done < MAX_BISECT_ITERS) & (todo > 0.0)

    def bisect_round(carry):
        done, _, lo, hi, c_lo = carry
        return (done + BISECT_ROUND,) + halvings(BISECT_ROUND, lo, hi, c_lo)

    hi0 = rmax + jnp.maximum(jnp.abs(rmax) * 2.0 ** -22, jnp.finfo(F32).tiny)
    first = (jnp.int32(BISECT_FIRST),) + halvings(BISECT_FIRST, rmin, hi0, n_causal)
    _, _, lo, hi, c_lo = lax.while_loop(pending, bisect_round, first)
    thr = jnp.where(short, jnp.finfo(F32).min, lo)
    tied = jnp.logical_not(short) & (c_lo > k_sel)
    n_tied = jnp.sum(jnp.where(tied, 1.0, 0.0))

    m_ref[...] = jnp.full(m_ref.shape, -1e30, F32)
    l_ref[...] = jnp.zeros(l_ref.shape, F32)
    acc_ref[...] = jnp.zeros(acc_ref.shape, F32)

    def qk_products(kc, slot):
        ks = pl.multiple_of(jnp.minimum(kc, last) * KEY_CHUNK, KEY_CHUNK)
        for g in range(N_KV_A):
            k_c = k_ref[pl.ds(ks, KEY_CHUNK), g * HEAD_DIM_A:(g + 1) * HEAD_DIM_A]
            prod_ref[slot, :, g * GRP_LANES:(g + 1) * GRP_LANES] = jnp.dot(
                k_c, qz_ref[:, g * GRP_LANES:(g + 1) * GRP_LANES], preferred_element_type=F32)

    def softmax_pv(kc, slot, near, ties, upper=None, quota=None):
        live = (kc <= last) if near else True
        kc = jnp.minimum(kc, last) if near else kc
        sc = sc_ref[kc]
        if ties:
            above = sc >= upper
            tie = (sc >= thr) & jnp.logical_not(above)
            tie_f = jnp.where(tie, 1.0, 0.0)
            upto = jnp.dot(tri_ref[...], tie_f.astype(BF16), preferred_element_type=F32)
            before = ties_ref[...]
            keep = (above | (tie & (before + upto - tie_f < quota))) & live
            ties_ref[...] = before + jnp.where(live, upto[KEY_CHUNK - 1:KEY_CHUNK], 0.0)
        else:
            keep = (sc >= thr) & live
        mask = jnp.where(keep, 0.0, -jnp.inf)
        mask = jnp.concatenate([mask] * GRP, axis=1)
        for g in range(N_KV_A):
            lg = prod_ref[slot, :, g * GRP_LANES:(g + 1) * GRP_LANES]
            if near:
                lg = lg + bias_ref[jnp.minimum(i - 2 * kc, N_BIAS_TILES - 1), g]
            lg = lg + mask
            m_old = m_ref[g]
            m_new = jnp.maximum(m_old, jnp.max(lg, axis=0, keepdims=True))
            alpha = jnp.exp2(m_old - m_new)
            p = jnp.exp2(lg - m_new)
            pv = jnp.dot(vt_ref[kc, g * V_ROWS:(g + 1) * V_ROWS, :], p.astype(BF16),
                         preferred_element_type=F32)
            l_ref[g] = alpha * l_ref[g] + pv[HEAD_DIM_A:HEAD_DIM_A + 1]
            acc_ref[g] = alpha * acc_ref[g] + pv[0:HEAD_DIM_A]
            m_ref[g] = m_new

    def attend(near, ties, prefetch=True, **tie_args):
        def body(j, carry):
            qk_products(2 * j + 1, 1)
            softmax_pv(2 * j, 0, near, ties, **tie_args)
            if prefetch:
                qk_products(2 * j + 2, 0)
            softmax_pv(2 * j + 1, 1, near, ties, **tie_args)
            return carry
        return body

    def attend_all(ties, **tie_args):
        run_pairs(0, n_far, attend(False, ties, **tie_args), 0)
        lax.fori_loop(n_far, n_pairs - 1, attend(True, ties, **tie_args), 0)
        attend(True, ties, prefetch=False, **tie_args)(n_pairs - 1, 0)

    n_far = jnp.maximum(n_pairs - 2, 0)
    qk_products(0, 0)

    @pl.when(n_tied == 0.0)
    def _():
        attend_all(False)

    @pl.when(n_tied > 0.0)
    def _():
        row = lax.broadcasted_iota(jnp.int32, (KEY_CHUNK, KEY_CHUNK), 0)
        col = lax.broadcasted_iota(jnp.int32, (KEY_CHUNK, KEY_CHUNK), 1)
        tri_ref[...] = (col <= row).astype(BF16)
        ties_ref[...] = jnp.zeros(ties_ref.shape, F32)
        attend_all(True, upper=jnp.where(tied, hi, thr), quota=jnp.where(tied, k_sel - count_ge(hi), 0.0))

    for g in range(N_KV_A):
        out_t = acc_ref[g] / l_ref[g]
        for j in range(GRP):
            h = g * GRP + j
            o_ref[:, h * HEAD_DIM_A:(h + 1) * HEAD_DIM_A] = (
                out_t[:, j * BLOCK_Q:(j + 1) * BLOCK_Q].T.astype(o_ref.dtype))


def _sparse_attention(bias_tiles, q_t, v_t, wi_t, k_n, batch, seq):
    nb = seq // BLOCK_Q
    n_seq_chunks = seq // KEY_CHUNK
    kw_blk = KV_WIDTH // V7X_LANES
    return pl.pallas_call(
        _sparse_attn_kernel,
        grid=(batch, nb),
        in_specs=[pl.BlockSpec(bias_tiles.shape, lambda b, i: (0, 0, 0, 0)),
                  pl.BlockSpec((ATTN_WIDTH, BLOCK_Q), lambda b, i: (0, b * nb + i)),
                  pl.BlockSpec((IDX_HEADS * IDX_DIM, BLOCK_Q), lambda b, i: (1, b * nb + i)),
                  pl.BlockSpec((IDX_HEADS, BLOCK_Q), lambda b, i: (0, b * nb + i)),
                  pl.BlockSpec((seq, V7X_LANES), lambda b, i: (b, kw_blk)),
                  pl.BlockSpec((seq, KV_WIDTH), lambda b, i: (b, 0)),
                  pl.BlockSpec((n_seq_chunks, N_KV_A * V_ROWS, KEY_CHUNK), lambda b, i: (b, 0, 0))],
        out_specs=pl.BlockSpec((BLOCK_Q, ATTN_WIDTH), lambda b, i: (b * nb + i, 0)),
        out_shape=jax.ShapeDtypeStruct((batch * seq, ATTN_WIDTH), BF16),
        scratch_shapes=[pltpu.VMEM((HEAD_DIM_A, N_HEADS_A * BLOCK_Q), BF16),
                        pltpu.VMEM((V7X_LANES, IDX_HEADS * BLOCK_Q), BF16),
                        pltpu.VMEM((n_seq_chunks, KEY_CHUNK, BLOCK_Q), F32),
                        pltpu.VMEM((2, KEY_CHUNK, IDX_HEADS * BLOCK_Q), F32),
                        pltpu.VMEM((N_KV_A, 1, GRP_LANES), F32),
                        pltpu.VMEM((N_KV_A, 1, GRP_LANES), F32),
                        pltpu.VMEM((N_KV_A, HEAD_DIM_A, GRP_LANES), F32),
                        pltpu.VMEM((KEY_CHUNK, KEY_CHUNK), BF16),
                        pltpu.VMEM((1, BLOCK_Q), F32)],
        compiler_params=_params(2),
        name="sparse_attention",
    )(bias_tiles, q_t, q_t, wi_t, k_n, k_n, v_t)


def _s5_input_kernel(u_ref, lr_ref, li_ref, ldt_ref, bre_ref, bim_ref,
                     bur_ref, bui_ref, ar_ref, ai_ref, bbr_ref, bbi_ref):
    @pl.when(pl.program_id(0) == 0)
    def _():
        dt = jnp.exp(ldt_ref[...])
        lr = lr_ref[...]
        li = li_ref[...]
        mag = jnp.exp(lr * dt)
        ar = mag * jnp.cos(li * dt)
        ai = mag * jnp.sin(li * dt)
        den = lr * lr + li * li
        nr = ar - 1.0
        cr = (nr * lr + ai * li) / den
        ci = (ai * lr - nr * li) / den
        bre = bre_ref[...]
        bim = bim_ref[...]
        bbr_ref[...] = (cr * bre - ci * bim).astype(BF16)
        bbi_ref[...] = (cr * bim + ci * bre).astype(BF16)
        ar_ref[...] = ar
        ai_ref[...] = ai

    u = u_ref[...].astype(BF16)
    bur_ref[...] = jnp.dot(u, bbr_ref[...], preferred_element_type=F32)
    bui_ref[...] = jnp.dot(u, bbi_ref[...], preferred_element_type=F32)


def _s5_input(us, lam_re, lam_im, log_dt, bre_bd, bim_bd):
    m = us.shape[0]
    tm = SMALL_ROW_TILE
    row = pl.BlockSpec((1, SSM_STATES), lambda i: (0, 0))
    mat = pl.BlockSpec((SSM_WIDTH, SSM_STATES), lambda i: (0, 0))
    out = pl.BlockSpec((tm, SSM_STATES), lambda i: (i, 0))
    return pl.pallas_call(
        _s5_input_kernel,
        grid=(m // tm,),
        in_specs=[pl.BlockSpec((tm, SSM_WIDTH), lambda i: (i, 0)), row, row, row, mat, mat],
        out_specs=[out, out, row, row],
        out_shape=[jax.ShapeDtypeStruct((m, SSM_STATES), F32)] * 2
        + [jax.ShapeDtypeStruct((1, SSM_STATES), F32)] * 2,
        scratch_shapes=[pltpu.VMEM((SSM_WIDTH, SSM_STATES), BF16)] * 2,
        compiler_params=_params(1),
        name="s5_input",
    )(us, lam_re, lam_im, log_dt, bre_bd, bim_bd)


def _s5_scan_kernel(ar_ref, ai_ref, br_ref, bi_ref, xr_ref, xi_ref, sr_ref, si_ref):
    @pl.when(pl.program_id(0) == 0)
    def _():
        sr_ref[...] = jnp.zeros(sr_ref.shape, F32)
        si_ref[...] = jnp.zeros(si_ref.shape, F32)

    ar = ar_ref[...]
    ai = ai_ref[...]
    steps = br_ref.shape[1]

    def step(t, carry):
        xr, xi = carry
        nr = ar * xr - ai * xi + br_ref[:, t]
        ni = ar * xi + ai * xr + bi_ref[:, t]
        xr_ref[:, t] = nr.astype(xr_ref.dtype)
        xi_ref[:, t] = ni.astype(xi_ref.dtype)
        return nr, ni

    xr, xi = lax.fori_loop(0, steps, step, (sr_ref[...], si_ref[...]), unroll=8)
    sr_ref[...] = xr
    si_ref[...] = xi


def _s5_scan(a_re, a_im, bu_re, bu_im):
    batch, seq, rows, lanes = bu_re.shape
    blk = pl.BlockSpec((batch, SCAN_CHUNK, rows, lanes), lambda c: (0, c, 0, 0))
    par = pl.BlockSpec((rows, lanes), lambda c: (0, 0))
    return pl.pallas_call(
        _s5_scan_kernel,
        grid=(seq // SCAN_CHUNK,),
        in_specs=[par, par, blk, blk],
        out_specs=[blk, blk],
        out_shape=[jax.ShapeDtypeStruct(bu_re.shape, BF16)] * 2,
        scratch_shapes=[pltpu.VMEM((batch, rows, lanes), F32)] * 2,
        compiler_params=_params(1),
        name="s5_scan",
    )(a_re, a_im, bu_re, bu_im)


def _s5_output_kernel(xr_ref, xi_ref, u_ref, cr_ref, ci_ref, d_ref, wg_ref, bg_ref, o_ref):
    y = (jnp.dot(xr_ref[...], cr_ref[...], preferred_element_type=F32)
         - jnp.dot(xi_ref[...], ci_ref[...], preferred_element_type=F32)
         + d_ref[...] * u_ref[...])
    g = jax.nn.gelu(y)
    gate = jax.nn.sigmoid(jnp.dot(g.astype(BF16), wg_ref[...], preferred_element_type=F32) + bg_ref[...])
    o_ref[...] = (g * gate).astype(o_ref.dtype)


def _s5_output(xr, xi, us, cre_bd, cim_bd, d_row, w_glu, b_glu):
    m = us.shape[0]
    tm = SMALL_ROW_TILE
    state = pl.BlockSpec((tm, SSM_STATES), lambda i: (i, 0))
    chan = pl.BlockSpec((tm, SSM_WIDTH), lambda i: (i, 0))
    cmat = pl.BlockSpec((SSM_STATES, SSM_WIDTH), lambda i: (0, 0))
    row = pl.BlockSpec((1, SSM_WIDTH), lambda i: (0, 0))
    return pl.pallas_call(
        _s5_output_kernel,
        grid=(m // tm,),
        in_specs=[state, state, chan, cmat, cmat, row,
                  pl.BlockSpec((SSM_WIDTH, SSM_WIDTH), lambda i: (0, 0)), row],
        out_specs=chan,
        out_shape=jax.ShapeDtypeStruct((m, SSM_WIDTH), BF16),
        compiler_params=_params(1),
        name="s5_output",
    )(xr, xi, us, cre_bd, cim_bd, d_row, w_glu, b_glu)


def _block_diag(blocks):
    g, r, c = blocks.shape
    eye = jnp.eye(g, dtype=blocks.dtype)
    return (eye[:, None, :, None] * blocks[:, :, None, :]).reshape(g * r, g * c)


def _post_mixer_kernel(x_ref, ya_ref, yb_ref, ga_ref, gb_ref, wa_ref, wb_ref, wo_ref, gc_ref, wq_ref,
                       kv_ref, wco_ref, o_ref):
    width = X_HEADS * X_HEAD_DIM
    scale = X_HEAD_DIM ** -0.5
    za = jnp.dot(ya_ref[...], wa_ref[...], preferred_element_type=F32)
    zb = jnp.dot(yb_ref[...], wb_ref[...], preferred_element_type=F32)
    z = (ga_ref[...].astype(F32) * za + gb_ref[...].astype(F32) * zb).astype(BF16)
    x1 = x_ref[...] + jnp.dot(z, wo_ref[...], preferred_element_type=F32)
    xn = _rms(x1, gc_ref[...], EPS).astype(BF16)
    q = jnp.dot(xn, wq_ref[...], preferred_element_type=F32).astype(BF16)
    heads = []
    for h in range(X_HEADS):
        cols = slice(h * X_HEAD_DIM, (h + 1) * X_HEAD_DIM)
        vcols = slice(width + h * X_HEAD_DIM, width + (h + 1) * X_HEAD_DIM)
        lg = lax.dot_general(q[:, cols], kv_ref[:, cols], _NT, preferred_element_type=F32) * scale
        p = jnp.exp(lg - jnp.max(lg, axis=-1, keepdims=True))
        o = jnp.dot(p.astype(BF16), kv_ref[:, vcols], preferred_element_type=F32)
        heads.append((o / jnp.sum(p, axis=-1, keepdims=True)).astype(BF16))
    attn = jnp.concatenate(heads, axis=1)
    o_ref[...] = x1 + jnp.dot(attn, wco_ref[...], preferred_element_type=F32)


def _post_mixer(x, ya, yb, gates, w_a, w_b, w_out, gain_cross, w_q, kv, w_co, batch, seq, mem_len):
    d = x.shape[1]
    width = X_HEADS * X_HEAD_DIM
    tm = SMALL_ROW_TILE
    nt = seq // tm
    row = pl.BlockSpec((tm, d), lambda b, i: (b * nt + i, 0))
    return pl.pallas_call(
        _post_mixer_kernel,
        grid=(batch, nt),
        in_specs=[row,
                  pl.BlockSpec((tm, ya.shape[1]), lambda b, i: (b * nt + i, 0)),
                  pl.BlockSpec((tm, yb.shape[1]), lambda b, i: (b * nt + i, 0)),
                  row,
                  pl.BlockSpec((tm, d), lambda b, i: (b * nt + i, 1)),
                  pl.BlockSpec(w_a.shape, lambda b, i: (0, 0)),
                  pl.BlockSpec(w_b.shape, lambda b, i: (0, 0)),
                  pl.BlockSpec((d, d), lambda b, i: (0, 0)),
                  pl.BlockSpec((1, d), lambda b, i: (0, 0)),
                  pl.BlockSpec((d, width), lambda b, i: (0, 0)),
                  pl.BlockSpec((mem_len, 2 * width), lambda b, i: (b, 0)),
                  pl.BlockSpec((width, d), lambda b, i: (0, 0))],
        out_specs=row,
        out_shape=jax.ShapeDtypeStruct(x.shape, F32),
        compiler_params=_params(2),
        name="post_mixer",
    )(x, ya, yb, gates, gates, w_a, w_b, w_out, gain_cross.reshape(1, d), w_q, kv, w_co)


def _conv_ffn_kernel(x_ref, g_ref, wa_ref, wb_ref, cwa_ref, cwb_ref, cba_ref, cbb_ref, wd_ref, gf_ref,
                     o_ref, xn_ref, ha_ref, hb_ref, ta_ref, tb_ref, *, tiles_per_seq):
    i = pl.program_id(0)
    j = pl.program_id(1)
    tm = x_ref.shape[0]
    halo = V7X_SUBLANES

    @pl.when(j == 0)
    def _():
        xn_ref[...] = _rms(x_ref[...], g_ref[...], EPS).astype(BF16)
        o_ref[...] = jnp.zeros(o_ref.shape, F32)

    @pl.when(i % tiles_per_seq == 0)
    def _():
        ha_ref[0:halo] = jnp.zeros((halo, ha_ref.shape[1]), F32)
        hb_ref[0:halo] = jnp.zeros((halo, hb_ref.shape[1]), F32)

    @pl.when(i % tiles_per_seq != 0)
    def _():
        ha_ref[0:halo] = ta_ref[j]
        hb_ref[0:halo] = tb_ref[j]

    rows = tm // FFN_ROW_SLABS
    for s in range(FFN_ROW_SLABS):
        xn = xn_ref[s * rows:(s + 1) * rows]
        ha_ref[halo + s * rows:halo + (s + 1) * rows] = jnp.dot(xn, wa_ref[...], preferred_element_type=F32)
        hb_ref[halo + s * rows:halo + (s + 1) * rows] = jnp.dot(xn, wb_ref[...], preferred_element_type=F32)
    ta_ref[j] = ha_ref[tm:tm + halo]
    tb_ref[j] = hb_ref[tm:tm + halo]

    def conv(h_ref, cw_ref, cb_ref, r0):
        out = cb_ref[...]
        for tap in range(CONV_WIDTH):
            start = r0 + halo - (CONV_WIDTH - 1) + tap
            out = out + h_ref[start:start + rows] * cw_ref[tap:tap + 1]
        return out

    for s in range(FFN_ROW_SLABS):
        a = conv(ha_ref, cwa_ref, cba_ref, s * rows)
        b = conv(hb_ref, cwb_ref, cbb_ref, s * rows)
        act = (a * jax.nn.sigmoid(a) * b).astype(BF16)
        o_ref[s * rows:(s + 1) * rows] += jnp.dot(act, wd_ref[...], preferred_element_type=F32)

    @pl.when(j == pl.num_programs(1) - 1)
    def _():
        o_ref[...] = _rms(x_ref[...] + o_ref[...], gf_ref[...], EPS)


def _conv_ffn(x, gain, w_up, conv_w, conv_b, w_down, gain_final, seq):
    m, d = x.shape
    d_ff = w_down.shape[0]
    tm, tf = FFN_ROW_TILE, COL_TILE
    nj = d_ff // tf
    halo = V7X_SUBLANES
    return pl.pallas_call(
        functools.partial(_conv_ffn_kernel, tiles_per_seq=seq // tm),
        grid=(m // tm, nj),
        in_specs=[pl.BlockSpec((tm, d), lambda i, j: (i, 0)),
                  pl.BlockSpec((1, d), lambda i, j: (0, 0)),
                  pl.BlockSpec((d, tf), lambda i, j: (0, j)),
                  pl.BlockSpec((d, tf), lambda i, j: (0, j + nj)),
                  pl.BlockSpec((CONV_WIDTH, tf), lambda i, j: (0, j)),
                  pl.BlockSpec((CONV_WIDTH, tf), lambda i, j: (0, j + nj)),
                  pl.BlockSpec((1, tf), lambda i, j: (0, j)),
                  pl.BlockSpec((1, tf), lambda i, j: (0, j + nj)),
                  pl.BlockSpec((tf, d), lambda i, j: (j, 0)),
                  pl.BlockSpec((1, d), lambda i, j: (0, 0))],
        out_specs=pl.BlockSpec((tm, d), lambda i, j: (i, 0)),
        out_shape=jax.ShapeDtypeStruct((m, d), F32),
        scratch_shapes=[pltpu.VMEM((tm, d), BF16),
                        pltpu.VMEM((tm + halo, tf), F32),
                        pltpu.VMEM((tm + halo, tf), F32),
                        pltpu.VMEM((nj, halo, tf), F32),
                        pltpu.VMEM((nj, halo, tf), F32)],
        compiler_params=_params(2),
        name="conv_ffn",
    )(x, gain.reshape(1, d), w_up, w_up, conv_w, conv_w, conv_b.reshape(1, -1), conv_b.reshape(1, -1),
      w_down, gain_final.reshape(1, d))


def kernel(x, mem, rel_bias, norm_mix, w_in, ssm_lambda_re, ssm_lambda_im, ssm_log_dt, ssm_b_re, ssm_b_im, ssm_c_re, ssm_c_im, ssm_d, ssm_w_glu, ssm_b_glu, w_branch_a, w_branch_b, w_out, norm_cross, norm_mem, w_cross_q, w_cross_kv, w_cross_o, norm_ffn, w_up, ffn_conv_w, ffn_conv_b, w_down, norm_final):
    batch, seq, d_model = x.shape
    mem_len = mem.shape[1]
    depth = w_in.shape[0]
    assert depth == 1, "the final rmsnorm is fused into the last layer's ConvFFN kernel"
    assert seq % ROW_TILE == 0 and seq % KEY_CHUNK == 0 and seq >= 4 * TOPK_MAX
    m = batch * seq
    xf = x.reshape(m, d_model)
    memf = mem.reshape(batch * mem_len, d_model)
    bias_tiles = _bias_tiles(rel_bias)

    splits = (ATTN_WIDTH, KV_WIDTH, KV_WIDTH, IDX_HEADS * IDX_DIM, IDX_DIM, IDX_HEADS, SSM_WIDTH, d_model, d_model)
    offs = [int(o) for o in np.cumsum((0,) + splits)]
    for l in range(depth):
        w = w_in[l]
        wq, wk, wv, wqi, wki, wwi, wss, wga, wgb = [w[:, offs[n]:offs[n + 1]] for n in range(len(splits))]
        pad = jnp.zeros((d_model, V7X_LANES - IDX_DIM - IDX_HEADS), w.dtype)
        w_t = jnp.concatenate([wq, wqi, wv, wwi], axis=1).T.astype(BF16)
        w_n = jnp.concatenate([wk, wki, wwi, pad], axis=1).astype(BF16)
        w_gate = jnp.concatenate([wga, wgb], axis=1).astype(BF16)

        q_t, v_t, wi_t, k_n = _proj_attn(xf, norm_mix[l], w_t, w_n)
        us = _norm_matmul(xf, norm_mix[l], wss.astype(BF16), F32, COL_TILE, "proj_ssm")
        gates = _norm_matmul(xf, norm_mix[l], w_gate, BF16, GATE_COL_TILE, "proj_gates", sigmoid=True)

        y_a = _sparse_attention(bias_tiles, q_t, v_t, wi_t, k_n, batch, seq)

        bre_bd = _block_diag(jnp.transpose(ssm_b_re[l], (0, 2, 1)))
        bim_bd = _block_diag(jnp.transpose(ssm_b_im[l], (0, 2, 1)))
        cre_bd = _block_diag(jnp.transpose(ssm_c_re[l], (0, 2, 1))).astype(BF16)
        cim_bd = _block_diag(jnp.transpose(ssm_c_im[l], (0, 2, 1))).astype(BF16)
        log_dt = jnp.broadcast_to(ssm_log_dt[l][:, None], (SSM_GROUPS, SSM_STATE)).reshape(1, SSM_STATES)
        bu_re, bu_im, a_re, a_im = _s5_input(us, ssm_lambda_re[l].reshape(1, SSM_STATES),
                                             ssm_lambda_im[l].reshape(1, SSM_STATES), log_dt, bre_bd, bim_bd)
        state_rows = SSM_STATES // V7X_LANES
        sshape = (batch, seq, state_rows, V7X_LANES)
        xs_re, xs_im = _s5_scan(a_re.reshape(state_rows, V7X_LANES), a_im.reshape(state_rows, V7X_LANES),
                                bu_re.reshape(sshape), bu_im.reshape(sshape))
        y_b = _s5_output(xs_re.reshape(m, SSM_STATES), xs_im.reshape(m, SSM_STATES), us, cre_bd, cim_bd,
                         ssm_d[l].reshape(1, SSM_WIDTH), ssm_w_glu[l].astype(BF16),
                         ssm_b_glu[l].reshape(1, SSM_WIDTH))

        kvx = _norm_matmul(memf, norm_mem[l], w_cross_kv[l].astype(BF16), BF16, COL_TILE, "cross_kv")
        xf = _post_mixer(xf, y_a, y_b, gates, w_branch_a[l].astype(BF16), w_branch_b[l].astype(BF16),
                         w_out[l].astype(BF16), norm_cross[l], w_cross_q[l].astype(BF16), kvx,
                         w_cross_o[l].astype(BF16), batch, seq, mem_len)

        xf = _conv_ffn(xf, norm_ffn[l], w_up[l].astype(BF16), ffn_conv_w[l], ffn_conv_b[l],
                       w_down[l].astype(BF16), norm_final, seq)
    return xf.reshape(batch, seq, d_model)
```

```python
import functools
import math

import jax
import jax.numpy as jnp
import numpy as np
from jax import lax
from jax.experimental import pallas as pl
from jax.experimental.pallas import tpu as pltpu

N_HEADS_A = 8
N_KV_A = 2
HEAD_DIM_A = 128
ATTN_WIDTH = N_HEADS_A * HEAD_DIM_A
KV_WIDTH = N_KV_A * HEAD_DIM_A
IDX_HEADS = 16
IDX_DIM = 64
TOPK_MAX = 256
BLOCK_Q = 128
REL_BUCKETS = 32
REL_MAX_DIST = 128
SSM_GROUP = 16
SSM_GROUPS = 32
SSM_STATE = 64
SSM_WIDTH = SSM_GROUP * SSM_GROUPS
SSM_STATES = SSM_GROUPS * SSM_STATE
X_HEADS = 4
X_HEAD_DIM = 128
CONV_WIDTH = 3
EPS = 1e-6

V7X_LANES = 128
V7X_SUBLANES = 8
V7X_VMEM_BYTES = 64 * 1024 * 1024
VMEM_LIMIT_BYTES = V7X_VMEM_BYTES - 8 * 1024 * 1024

GRP = N_HEADS_A // N_KV_A
GRP_LANES = GRP * BLOCK_Q
V_ROWS = HEAD_DIM_A + 16
KEY_CHUNK = 2 * BLOCK_Q
N_BIAS_TILES = 4
MAX_BISECT_ITERS = 64
BISECT_FIRST = 12
BISECT_ROUND = 2
COUNT_ROWS = 64
COUNT_STEP = 4
LOG2_E = math.log2(math.e)
ROW_TILE = 1024
SMALL_ROW_TILE = 512
COL_TILE = 512
GATE_COL_TILE = 1024
FFN_ROW_TILE = 512
FFN_ROW_SLABS = 2
SCAN_CHUNK = 128

F32 = jnp.float32
BF16 = jnp.bfloat16
_NT = (((1,), (1,)), ((), ()))


def _params(n_axes):
    return pltpu.CompilerParams(dimension_semantics=("arbitrary",) * n_axes,
                                vmem_limit_bytes=VMEM_LIMIT_BYTES)


def _rms(x, gain, eps):
    return x * lax.rsqrt(jnp.mean(x * x, axis=-1, keepdims=True) + eps) * gain


def _norm_matmul_kernel(x_ref, g_ref, w_ref, o_ref, xn_ref, *, sigmoid):
    @pl.when(pl.program_id(1) == 0)
    def _():
        xn_ref[...] = _rms(x_ref[...], g_ref[...], EPS).astype(BF16)

    tm = o_ref.shape[0]
    rows = min(tm, SMALL_ROW_TILE) if sigmoid else tm
    for r in range(0, tm, rows):
        y = jnp.dot(xn_ref[r:r + rows], w_ref[...], preferred_element_type=F32)
        if sigmoid:
            y = 0.5 * jnp.tanh(0.5 * y) + 0.5
        o_ref[r:r + rows] = y.astype(o_ref.dtype)


def _norm_matmul(x, gain, w, out_dtype, tn, name, sigmoid=False):
    m, k = x.shape
    n = w.shape[1]
    tm = min(ROW_TILE, m)
    return pl.pallas_call(
        functools.partial(_norm_matmul_kernel, sigmoid=sigmoid),
        grid=(m // tm, n // tn),
        in_specs=[pl.BlockSpec((tm, k), lambda i, j: (i, 0)),
                  pl.BlockSpec((1, k), lambda i, j: (0, 0)),
                  pl.BlockSpec((k, tn), lambda i, j: (0, j))],
        out_specs=pl.BlockSpec((tm, tn), lambda i, j: (i, j)),
        out_shape=jax.ShapeDtypeStruct((m, n), out_dtype),
        scratch_shapes=[pltpu.VMEM((tm, k), BF16)],
        compiler_params=_params(2),
        name=name,
    )(x, gain.reshape(1, k), w)


def _proj_attn_kernel(x_ref, g_ref, wt_ref, wn_ref, qt_ref, vt_ref, wit_ref, kn_ref, us_ref):
    xn = _rms(x_ref[...], g_ref[...], EPS).astype(BF16)
    res = lax.dot_general(wt_ref[...], xn, _NT, preferred_element_type=F32)
    nq = qt_ref.shape[0]
    qt_ref[0:ATTN_WIDTH] = (res[0:ATTN_WIDTH] * (HEAD_DIM_A ** -0.5 * LOG2_E)).astype(BF16)
    qt_ref[ATTN_WIDTH:nq] = res[ATTN_WIDTH:nq].astype(BF16)
    pad_rows = V_ROWS - HEAD_DIM_A
    ones_row = (lax.broadcasted_iota(jnp.int32, (pad_rows, KEY_CHUNK), 0) == 0).astype(BF16)
    for c in range(vt_ref.shape[0]):
        for g in range(N_KV_A):
            rows = res[nq + g * HEAD_DIM_A:nq + (g + 1) * HEAD_DIM_A, c * KEY_CHUNK:(c + 1) * KEY_CHUNK]
            vt_ref[c, g * V_ROWS:g * V_ROWS + HEAD_DIM_A] = rows.astype(BF16)
            vt_ref[c, g * V_ROWS + HEAD_DIM_A:(g + 1) * V_ROWS] = ones_row
    wit_ref[...] = res[nq + KV_WIDTH:nq + KV_WIDTH + IDX_HEADS]
    plain = jnp.dot(xn, wn_ref[...], preferred_element_type=F32)
    n_kn = kn_ref.shape[1]
    kn_ref[...] = plain[:, 0:n_kn].astype(BF16)
    us_ref[...] = plain[:, n_kn:]


def _proj_attn(x, gain, w_t, w_n):
    m, k = x.shape
    tm = SMALL_ROW_TILE
    nq = ATTN_WIDTH + IDX_HEADS * IDX_DIM
    return pl.pallas_call(
        _proj_attn_kernel,
        grid=(m // tm,),
        in_specs=[pl.BlockSpec((tm, k), lambda i: (i, 0)),
                  pl.BlockSpec((1, k), lambda i: (0, 0)),
                  pl.BlockSpec(w_t.shape, lambda i: (0, 0)),
                  pl.BlockSpec(w_n.shape, lambda i: (0, 0))],
        out_specs=[pl.BlockSpec((nq, tm), lambda i: (0, i)),
                   pl.BlockSpec((tm // KEY_CHUNK, N_KV_A * V_ROWS, KEY_CHUNK), lambda i: (i, 0, 0)),
                   pl.BlockSpec((IDX_HEADS, tm), lambda i: (0, i)),
                   pl.BlockSpec((tm, w_n.shape[1] - SSM_WIDTH), lambda i: (i, 0)),
                   pl.BlockSpec((tm, SSM_WIDTH), lambda i: (i, 0))],
        out_shape=[jax.ShapeDtypeStruct((nq, m), BF16),
                   jax.ShapeDtypeStruct((m // KEY_CHUNK, N_KV_A * V_ROWS, KEY_CHUNK), BF16),
                   jax.ShapeDtypeStruct((IDX_HEADS, m), F32),
                   jax.ShapeDtypeStruct((m, w_n.shape[1] - SSM_WIDTH), BF16),
                   jax.ShapeDtypeStruct((m, SSM_WIDTH), F32)],
        compiler_params=_params(1),
        name="proj_attn",
    )(x, gain.reshape(1, k), w_t, w_n)


def _t5_bucket(n):
    max_exact = REL_BUCKETS // 2
    n = jnp.maximum(n, 0)
    nf = jnp.maximum(n, 1).astype(F32)
    large = max_exact + (jnp.log(nf / max_exact) / math.log(REL_MAX_DIST / max_exact)
                         * (REL_BUCKETS - max_exact)).astype(jnp.int32)
    large = jnp.minimum(large, REL_BUCKETS - 1)
    return jnp.where(n < max_exact, n, large)


def _bias_tiles_kernel(rb_ref, o_ref):
    shape = (KEY_CHUNK, BLOCK_Q)
    key = lax.broadcasted_iota(jnp.int32, shape, 0)
    qry = lax.broadcasted_iota(jnp.int32, shape, 1)
    for o in range(N_BIAS_TILES):
        bucket = _t5_bucket(o * BLOCK_Q + qry - key)
        for h in range(N_HEADS_A):
            def pick(b, t, bucket=bucket, h=h):
                return jnp.where(bucket == b, rb_ref[b, h], t)
            tile = lax.fori_loop(0, REL_BUCKETS, pick, jnp.zeros(shape, F32))
            j = h % GRP
            o_ref[o, h // GRP, :, j * BLOCK_Q:(j + 1) * BLOCK_Q] = (tile - rb_ref[REL_BUCKETS - 1, h]) * LOG2_E


def _bias_tiles(rel_bias):
    return pl.pallas_call(
        _bias_tiles_kernel,
        in_specs=[pl.BlockSpec(memory_space=pltpu.SMEM)],
        out_specs=pl.BlockSpec(memory_space=pltpu.VMEM),
        out_shape=jax.ShapeDtypeStruct((N_BIAS_TILES, N_KV_A, KEY_CHUNK, GRP_LANES), F32),
        name="rel_bias_tiles",
    )(rel_bias)


def _sparse_attn_kernel(bias_ref, qt_ref, qit_ref, wit_ref, kw_ref, k_ref, vt_ref, o_ref,
                        qz_ref, qiz_ref, sc_ref, prod_ref, m_ref, l_ref, acc_ref, tri_ref, ties_ref):
    i = pl.program_id(1)
    t0 = i * BLOCK_Q
    n_chunks = (i + 2) // 2
    tile = (KEY_CHUNK, BLOCK_Q)
    row1 = (1, BLOCK_Q)
    k_sel = float(TOPK_MAX)

    @pl.when((pl.program_id(0) == 0) & (i == 0))
    def _():
        qiz_ref[...] = jnp.zeros(qiz_ref.shape, BF16)

    for h in range(IDX_HEADS):
        qiz_ref[0:IDX_DIM, h * BLOCK_Q:(h + 1) * BLOCK_Q] = qit_ref[h * IDX_DIM:(h + 1) * IDX_DIM, :]
    for h in range(N_HEADS_A):
        qz_ref[:, h * BLOCK_Q:(h + 1) * BLOCK_Q] = qt_ref[h * HEAD_DIM_A:(h + 1) * HEAD_DIM_A, :]
    w_rows = wit_ref[...] * ((IDX_HEADS * IDX_DIM) ** -0.5)

    key_row = lax.broadcasted_iota(jnp.int32, tile, 0)
    q_pos = t0 + lax.broadcasted_iota(jnp.int32, tile, 1)

    last = n_chunks - 1
    n_pairs = (n_chunks + 1) // 2

    def idx_products(kc, slot):
        ks = pl.multiple_of(jnp.minimum(kc, last) * KEY_CHUNK, KEY_CHUNK)
        prod_ref[slot] = jnp.dot(kw_ref[pl.ds(ks, KEY_CHUNK), :], qiz_ref[...], preferred_element_type=F32)

    def reduce_scores(kc, slot, carry):
        rmax, rmin = carry
        kc = jnp.minimum(kc, last)
        s = jnp.zeros(tile, F32)
        for h in range(IDX_HEADS):
            d = prod_ref[slot, :, h * BLOCK_Q:(h + 1) * BLOCK_Q]
            s = s + jnp.maximum(d, 0.0) * w_rows[h:h + 1, :]
        causal = (kc * KEY_CHUNK + key_row) <= q_pos
        sc_ref[kc] = jnp.where(causal, s, -jnp.inf)
        rmax = jnp.maximum(rmax, jnp.max(jnp.where(causal, s, -jnp.inf), axis=0, keepdims=True))
        rmin = jnp.minimum(rmin, jnp.min(jnp.where(causal, s, jnp.inf), axis=0, keepdims=True))
        return rmax, rmin

    def run_pairs(start, stop, pair_body, carry):
        n_double = (stop - start) // 2

        def double(t, c):
            j = start + 2 * t
            return pair_body(j + 1, pair_body(j, c))

        carry = lax.fori_loop(0, n_double, double, carry)
        return lax.fori_loop(start + 2 * n_double, stop, pair_body, carry)

    def score_pair(j, carry, prefetch=True):
        idx_products(2 * j + 1, 1)
        carry = reduce_scores(2 * j, 0, carry)
        if prefetch:
            idx_products(2 * j + 2, 0)
        return reduce_scores(2 * j + 1, 1, carry)

    idx_products(0, 0)
    carry = lax.fori_loop(0, n_pairs - 1, score_pair,
                          (jnp.full(row1, -jnp.inf, F32), jnp.full(row1, jnp.inf, F32)))
    rmax, rmin = score_pair(n_pairs - 1, carry, prefetch=False)

    n_causal = (t0 + lax.broadcasted_iota(jnp.int32, row1, 1) + 1).astype(F32)
    short = n_causal <= k_sel

    n_steps = (n_chunks + COUNT_STEP - 1) // COUNT_STEP
    for d in range(COUNT_STEP - 1):
        @pl.when(n_chunks + d < n_steps * COUNT_STEP)
        def _(d=d):
            sc_ref[n_chunks + d] = jnp.full(tile, -jnp.inf, F32)

    def count_ge(t):
        def body(j, c):
            for d in range(COUNT_STEP):
                hit = jnp.where(sc_ref[COUNT_STEP * j + d] >= t, 1.0, 0.0)
                c = c + jnp.sum(hit.reshape(KEY_CHUNK // COUNT_ROWS, COUNT_ROWS, BLOCK_Q), axis=0)
            return c
        c = lax.fori_loop(0, n_steps, body, jnp.zeros((COUNT_ROWS, BLOCK_Q), F32))
        return jnp.sum(c, axis=0, keepdims=True)

    def halve(_, carry):
        lo, hi, c_lo, _ = carry
        mid = lo + (hi - lo) * 0.5
        cnt = count_ge(mid)
        enough = cnt >= k_sel
        shrinks = jnp.where((mid > lo) & (mid < hi), 1.0, 0.0)
        return jnp.where(enough, mid, lo), jnp.where(enough, hi, mid), jnp.where(enough, cnt, c_lo), shrinks

    def halvings(count, lo, hi, c_lo):
        lo, hi, c_lo, shrinks = lax.fori_loop(0, count, halve, (lo, hi, c_lo, jnp.ones(row1, F32)))
        settled = short | (c_lo == k_sel) | (shrinks == 0.0)
        return jnp.sum(jnp.where(settled, 0.0, 1.0)), lo, hi, c_lo

    def pending(carry):
        done, todo, _, _, _ = carry
        return (done < MAX_BISECT_ITERS) & (todo > 0.0)

    def bisect_round(carry):
        done, _, lo, hi, c_lo = carry
        return (done + BISECT_ROUND,) + halvings(BISECT_ROUND, lo, hi, c_lo)

    hi0 = rmax + jnp.maximum(jnp.abs(rmax) * 2.0 ** -22, jnp.finfo(F32).tiny)
    first = (jnp.int32(BISECT_FIRST),) + halvings(BISECT_FIRST, rmin, hi0, n_causal)
    _, _, lo, hi, c_lo = lax.while_loop(pending, bisect_round, first)
    thr = jnp.where(short, jnp.finfo(F32).min, lo)
    tied = jnp.logical_not(short) & (c_lo > k_sel)
    n_tied = jnp.sum(jnp.where(tied, 1.0, 0.0))

    m_ref[...] = jnp.full(m_ref.shape, -1e30, F32)
    l_ref[...] = jnp.zeros(l_ref.shape, F32)
    acc_ref[...] = jnp.zeros(acc_ref.shape, F32)

    def qk_products(kc, slot):
        ks = pl.multiple_of(jnp.minimum(kc, last) * KEY_CHUNK, KEY_CHUNK)
        for g in range(N_KV_A):
            k_c = k_ref[pl.ds(ks, KEY_CHUNK), g * HEAD_DIM_A:(g + 1) * HEAD_DIM_A]
            prod_ref[slot, :, g * GRP_LANES:(g + 1) * GRP_LANES] = jnp.dot(
                k_c, qz_ref[:, g * GRP_LANES:(g + 1) * GRP_LANES], preferred_element_type=F32)

    def softmax_pv(kc, slot, near, ties, upper=None, quota=None):
        live = (kc <= last) if near else True
        kc = jnp.minimum(kc, last) if near else kc
        sc = sc_ref[kc]
        if ties:
            above = sc >= upper
            tie = (sc >= thr) & jnp.logical_not(above)
            tie_f = jnp.where(tie, 1.0, 0.0)
            upto = jnp.dot(tri_ref[...], tie_f.astype(BF16), preferred_element_type=F32)
            before = ties_ref[...]
            keep = (above | (tie & (before + upto - tie_f < quota))) & live
            ties_ref[...] = before + jnp.where(live, upto[KEY_CHUNK - 1:KEY_CHUNK], 0.0)
        else:
            keep = (sc >= thr) & live
        mask = jnp.where(keep, 0.0, -jnp.inf)
        mask = jnp.concatenate([mask] * GRP, axis=1)
        for g in range(N_KV_A):
            lg = prod_ref[slot, :, g * GRP_LANES:(g + 1) * GRP_LANES]
            if near:
                lg = lg + bias_ref[jnp.minimum(i - 2 * kc, N_BIAS_TILES - 1), g]
            lg = lg + mask
            m_old = m_ref[g]
            m_new = jnp.maximum(m_old, jnp.max(lg, axis=0, keepdims=True))
            alpha = jnp.exp2(m_old - m_new)
            p = jnp.exp2(lg - m_new)
            pv = jnp.dot(vt_ref[kc, g * V_ROWS:(g + 1) * V_ROWS, :], p.astype(BF16),
                         preferred_element_type=F32)
            l_ref[g] = alpha * l_ref[g] + pv[HEAD_DIM_A:HEAD_DIM_A + 1]
            acc_ref[g] = alpha * acc_ref[g] + pv[0:HEAD_DIM_A]
            m_ref[g] = m_new

    def attend(near, ties, prefetch=True, **tie_args):
        def body(j, carry):
            qk_products(2 * j + 1, 1)
            softmax_pv(2 * j, 0, near, ties, **tie_args)
            if prefetch:
                qk_products(2 * j + 2, 0)
            softmax_pv(2 * j + 1, 1, near, ties, **tie_args)
            return carry
        return body

    def attend_all(ties, **tie_args):
        run_pairs(0, n_far, attend(False, ties, **tie_args), 0)
        lax.fori_loop(n_far, n_pairs - 1, attend(True, ties, **tie_args), 0)
        attend(True, ties, prefetch=False, **tie_args)(n_pairs - 1, 0)

    n_far = jnp.maximum(n_pairs - 2, 0)
    qk_products(0, 0)

    @pl.when(n_tied == 0.0)
    def _():
        attend_all(False)

    @pl.when(n_tied > 0.0)
    def _():
        row = lax.broadcasted_iota(jnp.int32, (KEY_CHUNK, KEY_CHUNK), 0)
        col = lax.broadcasted_iota(jnp.int32, (KEY_CHUNK, KEY_CHUNK), 1)
        tri_ref[...] = (col <= row).astype(BF16)
        ties_ref[...] = jnp.zeros(ties_ref.shape, F32)
        attend_all(True, upper=jnp.where(tied, hi, thr), quota=jnp.where(tied, k_sel - count_ge(hi), 0.0))

    for g in range(N_KV_A):
        out_t = acc_ref[g] / l_ref[g]
        for j in range(GRP):
            h = g * GRP + j
            o_ref[:, h * HEAD_DIM_A:(h + 1) * HEAD_DIM_A] = (
                out_t[:, j * BLOCK_Q:(j + 1) * BLOCK_Q].T.astype(o_ref.dtype))


def _sparse_attention(bias_tiles, q_t, v_t, wi_t, k_n, batch, seq):
    nb = seq // BLOCK_Q
    n_seq_chunks = seq // KEY_CHUNK
    kw_blk = KV_WIDTH // V7X_LANES
    return pl.pallas_call(
        _sparse_attn_kernel,
        grid=(batch, nb),
        in_specs=[pl.BlockSpec(bias_tiles.shape, lambda b, i: (0, 0, 0, 0)),
                  pl.BlockSpec((ATTN_WIDTH, BLOCK_Q), lambda b, i: (0, b * nb + i)),
                  pl.BlockSpec((IDX_HEADS * IDX_DIM, BLOCK_Q), lambda b, i: (1, b * nb + i)),
                  pl.BlockSpec((IDX_HEADS, BLOCK_Q), lambda b, i: (0, b * nb + i)),
                  pl.BlockSpec((seq, V7X_LANES), lambda b, i: (b, kw_blk)),
                  pl.BlockSpec((seq, KV_WIDTH), lambda b, i: (b, 0)),
                  pl.BlockSpec((n_seq_chunks, N_KV_A * V_ROWS, KEY_CHUNK), lambda b, i: (b, 0, 0))],
        out_specs=pl.BlockSpec((BLOCK_Q, ATTN_WIDTH), lambda b, i: (b * nb + i, 0)),
        out_shape=jax.ShapeDtypeStruct((batch * seq, ATTN_WIDTH), BF16),
        scratch_shapes=[pltpu.VMEM((HEAD_DIM_A, N_HEADS_A * BLOCK_Q), BF16),
                        pltpu.VMEM((V7X_LANES, IDX_HEADS * BLOCK_Q), BF16),
                        pltpu.VMEM((n_seq_chunks, KEY_CHUNK, BLOCK_Q), F32),
                        pltpu.VMEM((2, KEY_CHUNK, IDX_HEADS * BLOCK_Q), F32),
                        pltpu.VMEM((N_KV_A, 1, GRP_LANES), F32),
                        pltpu.VMEM((N_KV_A, 1, GRP_LANES), F32),
                        pltpu.VMEM((N_KV_A, HEAD_DIM_A, GRP_LANES), F32),
                        pltpu.VMEM((KEY_CHUNK, KEY_CHUNK), BF16),
                        pltpu.VMEM((1, BLOCK_Q), F32)],
        compiler_params=_params(2),
        name="sparse_attention",
    )(bias_tiles, q_t, q_t, wi_t, k_n, k_n, v_t)


def _s5_input_kernel(u_ref, lr_ref, li_ref, ldt_ref, bre_ref, bim_ref,
                     bur_ref, bui_ref, ar_ref, ai_ref, bbr_ref, bbi_ref):
    @pl.when(pl.program_id(0) == 0)
    def _():
        dt = jnp.exp(ldt_ref[...])
        lr = lr_ref[...]
        li = li_ref[...]
        mag = jnp.exp(lr * dt)
        ar = mag * jnp.cos(li * dt)
        ai = mag * jnp.sin(li * dt)
        den = lr * lr + li * li
        nr = ar - 1.0
        cr = (nr * lr + ai * li) / den
        ci = (ai * lr - nr * li) / den
        bre = bre_ref[...]
        bim = bim_ref[...]
        bbr_ref[...] = (cr * bre - ci * bim).astype(BF16)
        bbi_ref[...] = (cr * bim + ci * bre).astype(BF16)
        ar_ref[...] = ar
        ai_ref[...] = ai

    u = u_ref[...].astype(BF16)
    bur_ref[...] = jnp.dot(u, bbr_ref[...], preferred_element_type=F32)
    bui_ref[...] = jnp.dot(u, bbi_ref[...], preferred_element_type=F32)


def _s5_input(us, lam_re, lam_im, log_dt, bre_bd, bim_bd):
    m = us.shape[0]
    tm = SMALL_ROW_TILE
    row = pl.BlockSpec((1, SSM_STATES), lambda i: (0, 0))
    mat = pl.BlockSpec((SSM_WIDTH, SSM_STATES), lambda i: (0, 0))
    out = pl.BlockSpec((tm, SSM_STATES), lambda i: (i, 0))
    return pl.pallas_call(
        _s5_input_kernel,
        grid=(m // tm,),
        in_specs=[pl.BlockSpec((tm, SSM_WIDTH), lambda i: (i, 0)), row, row, row, mat, mat],
        out_specs=[out, out, row, row],
        out_shape=[jax.ShapeDtypeStruct((m, SSM_STATES), F32)] * 2
        + [jax.ShapeDtypeStruct((1, SSM_STATES), F32)] * 2,
        scratch_shapes=[pltpu.VMEM((SSM_WIDTH, SSM_STATES), BF16)] * 2,
        compiler_params=_params(1),
        name="s5_input",
    )(us, lam_re, lam_im, log_dt, bre_bd, bim_bd)


def _s5_scan_kernel(ar_ref, ai_ref, br_ref, bi_ref, xr_ref, xi_ref, sr_ref, si_ref):
    @pl.when(pl.program_id(0) == 0)
    def _():
        sr_ref[...] = jnp.zeros(sr_ref.shape, F32)
        si_ref[...] = jnp.zeros(si_ref.shape, F32)

    ar = ar_ref[...]
    ai = ai_ref[...]
    steps = br_ref.shape[1]

    def step(t, carry):
        xr, xi = carry
        nr = ar * xr - ai * xi + br_ref[:, t]
        ni = ar * xi + ai * xr + bi_ref[:, t]
        xr_ref[:, t] = nr.astype(xr_ref.dtype)
        xi_ref[:, t] = ni.astype(xi_ref.dtype)
        return nr, ni

    xr, xi = lax.fori_loop(0, steps, step, (sr_ref[...], si_ref[...]), unroll=8)
    sr_ref[...] = xr
    si_ref[...] = xi


def _s5_scan(a_re, a_im, bu_re, bu_im):
    batch, seq, rows, lanes = bu_re.shape
    blk = pl.BlockSpec((batch, SCAN_CHUNK, rows, lanes), lambda c: (0, c, 0, 0))
    par = pl.BlockSpec((rows, lanes), lambda c: (0, 0))
    return pl.pallas_call(
        _s5_scan_kernel,
        grid=(seq // SCAN_CHUNK,),
        in_specs=[par, par, blk, blk],
        out_specs=[blk, blk],
        out_shape=[jax.ShapeDtypeStruct(bu_re.shape, BF16)] * 2,
        scratch_shapes=[pltpu.VMEM((batch, rows, lanes), F32)] * 2,
        compiler_params=_params(1),
        name="s5_scan",
    )(a_re, a_im, bu_re, bu_im)


def _s5_output_kernel(xr_ref, xi_ref, u_ref, cr_ref, ci_ref, d_ref, wg_ref, bg_ref, o_ref):
    y = (jnp.dot(xr_ref[...], cr_ref[...], preferred_element_type=F32)
         - jnp.dot(xi_ref[...], ci_ref[...], preferred_element_type=F32)
         + d_ref[...] * u_ref[...])
    g = jax.nn.gelu(y)
    gate = jax.nn.sigmoid(jnp.dot(g.astype(BF16), wg_ref[...], preferred_element_type=F32) + bg_ref[...])
    o_ref[...] = (g * gate).astype(o_ref.dtype)


def _s5_output(xr, xi, us, cre_bd, cim_bd, d_row, w_glu, b_glu):
    m = us.shape[0]
    tm = SMALL_ROW_TILE
    state = pl.BlockSpec((tm, SSM_STATES), lambda i: (i, 0))
    chan = pl.BlockSpec((tm, SSM_WIDTH), lambda i: (i, 0))
    cmat = pl.BlockSpec((SSM_STATES, SSM_WIDTH), lambda i: (0, 0))
    row = pl.BlockSpec((1, SSM_WIDTH), lambda i: (0, 0))
    return pl.pallas_call(
        _s5_output_kernel,
        grid=(m // tm,),
        in_specs=[state, state, chan, cmat, cmat, row,
                  pl.BlockSpec((SSM_WIDTH, SSM_WIDTH), lambda i: (0, 0)), row],
        out_specs=chan,
        out_shape=jax.ShapeDtypeStruct((m, SSM_WIDTH), BF16),
        compiler_params=_params(1),
        name="s5_output",
    )(xr, xi, us, cre_bd, cim_bd, d_row, w_glu, b_glu)


def _block_diag(blocks):
    g, r, c = blocks.shape
    eye = jnp.eye(g, dtype=blocks.dtype)
    return (eye[:, None, :, None] * blocks[:, :, None, :]).reshape(g * r, g * c)


def _post_mixer_kernel(x_ref, ya_ref, yb_ref, ga_ref, gb_ref, wa_ref, wb_ref, wo_ref, gc_ref, wq_ref,
                       kv_ref, wco_ref, o_ref):
    width = X_HEADS * X_HEAD_DIM
    scale = X_HEAD_DIM ** -0.5
    za = jnp.dot(ya_ref[...], wa_ref[...], preferred_element_type=F32)
    zb = jnp.dot(yb_ref[...], wb_ref[...], preferred_element_type=F32)
    z = (ga_ref[...].astype(F32) * za + gb_ref[...].astype(F32) * zb).astype(BF16)
    x1 = x_ref[...] + jnp.dot(z, wo_ref[...], preferred_element_type=F32)
    xn = _rms(x1, gc_ref[...], EPS).astype(BF16)
    q = jnp.dot(xn, wq_ref[...], preferred_element_type=F32).astype(BF16)
    heads = []
    for h in range(X_HEADS):
        cols = slice(h * X_HEAD_DIM, (h + 1) * X_HEAD_DIM)
        vcols = slice(width + h * X_HEAD_DIM, width + (h + 1) * X_HEAD_DIM)
        lg = lax.dot_general(q[:, cols], kv_ref[:, cols], _NT, preferred_element_type=F32) * scale
        p = jnp.exp(lg - jnp.max(lg, axis=-1, keepdims=True))
        o = jnp.dot(p.astype(BF16), kv_ref[:, vcols], preferred_element_type=F32)
        heads.append((o / jnp.sum(p, axis=-1, keepdims=True)).astype(BF16))
    attn = jnp.concatenate(heads, axis=1)
    o_ref[...] = x1 + jnp.dot(attn, wco_ref[...], preferred_element_type=F32)


def _post_mixer(x, ya, yb, gates, w_a, w_b, w_out, gain_cross, w_q, kv, w_co, batch, seq, mem_len):
    d = x.shape[1]
    width = X_HEADS * X_HEAD_DIM
    tm = SMALL_ROW_TILE
    nt = seq // tm
    row = pl.BlockSpec((tm, d), lambda b, i: (b * nt + i, 0))
    return pl.pallas_call(
        _post_mixer_kernel,
        grid=(batch, nt),
        in_specs=[row,
                  pl.BlockSpec((tm, ya.shape[1]), lambda b, i: (b * nt + i, 0)),
                  pl.BlockSpec((tm, yb.shape[1]), lambda b, i: (b * nt + i, 0)),
                  row,
                  pl.BlockSpec((tm, d), lambda b, i: (b * nt + i, 1)),
                  pl.BlockSpec(w_a.shape, lambda b, i: (0, 0)),
                  pl.BlockSpec(w_b.shape, lambda b, i: (0, 0)),
                  pl.BlockSpec((d, d), lambda b, i: (0, 0)),
                  pl.BlockSpec((1, d), lambda b, i: (0, 0)),
                  pl.BlockSpec((d, width), lambda b, i: (0, 0)),
                  pl.BlockSpec((mem_len, 2 * width), lambda b, i: (b, 0)),
                  pl.BlockSpec((width, d), lambda b, i: (0, 0))],
        out_specs=row,
        out_shape=jax.ShapeDtypeStruct(x.shape, F32),
        compiler_params=_params(2),
        name="post_mixer",
    )(x, ya, yb, gates, gates, w_a, w_b, w_out, gain_cross.reshape(1, d), w_q, kv, w_co)


def _conv_ffn_kernel(x_ref, g_ref, wa_ref, wb_ref, cwa_ref, cwb_ref, cba_ref, cbb_ref, wd_ref, gf_ref,
                     o_ref, xn_ref, ha_ref, hb_ref, ta_ref, tb_ref, *, tiles_per_seq):
    i = pl.program_id(0)
    j = pl.program_id(1)
    tm = x_ref.shape[0]
    halo = V7X_SUBLANES

    @pl.when(j == 0)
    def _():
        xn_ref[...] = _rms(x_ref[...], g_ref[...], EPS).astype(BF16)
        o_ref[...] = jnp.zeros(o_ref.shape, F32)

    @pl.when(i % tiles_per_seq == 0)
    def _():
        ha_ref[0:halo] = jnp.zeros((halo, ha_ref.shape[1]), F32)
        hb_ref[0:halo] = jnp.zeros((halo, hb_ref.shape[1]), F32)

    @pl.when(i % tiles_per_seq != 0)
    def _():
        ha_ref[0:halo] = ta_ref[j]
        hb_ref[0:halo] = tb_ref[j]

    rows = tm // FFN_ROW_SLABS
    for s in range(FFN_ROW_SLABS):
        xn = xn_ref[s * rows:(s + 1) * rows]
        ha_ref[halo + s * rows:halo + (s + 1) * rows] = jnp.dot(xn, wa_ref[...], preferred_element_type=F32)
        hb_ref[halo + s * rows:halo + (s + 1) * rows] = jnp.dot(xn, wb_ref[...], preferred_element_type=F32)
    ta_ref[j] = ha_ref[tm:tm + halo]
    tb_ref[j] = hb_ref[tm:tm + halo]

    def conv(h_ref, cw_ref, cb_ref, r0):
        out = cb_ref[...]
        for tap in range(CONV_WIDTH):
            start = r0 + halo - (CONV_WIDTH - 1) + tap
            out = out + h_ref[start:start + rows] * cw_ref[tap:tap + 1]
        return out

    for s in range(FFN_ROW_SLABS):
        a = conv(ha_ref, cwa_ref, cba_ref, s * rows)
        b = conv(hb_ref, cwb_ref, cbb_ref, s * rows)
        act = (a * jax.nn.sigmoid(a) * b).astype(BF16)
        o_ref[s * rows:(s + 1) * rows] += jnp.dot(act, wd_ref[...], preferred_element_type=F32)

    @pl.when(j == pl.num_programs(1) - 1)
    def _():
        o_ref[...] = _rms(x_ref[...] + o_ref[...], gf_ref[...], EPS)


def _conv_ffn(x, gain, w_up, conv_w, conv_b, w_down, gain_final, seq):
    m, d = x.shape
    d_ff = w_down.shape[0]
    tm, tf = FFN_ROW_TILE, COL_TILE
    nj = d_ff // tf
    halo = V7X_SUBLANES
    return pl.pallas_call(
        functools.partial(_conv_ffn_kernel, tiles_per_seq=seq // tm),
        grid=(m // tm, nj),
        in_specs=[pl.BlockSpec((tm, d), lambda i, j: (i, 0)),
                  pl.BlockSpec((1, d), lambda i, j: (0, 0)),
                  pl.BlockSpec((d, tf), lambda i, j: (0, j)),
                  pl.BlockSpec((d, tf), lambda i, j: (0, j + nj)),
                  pl.BlockSpec((CONV_WIDTH, tf), lambda i, j: (0, j)),
                  pl.BlockSpec((CONV_WIDTH, tf), lambda i, j: (0, j + nj)),
                  pl.BlockSpec((1, tf), lambda i, j: (0, j)),
                  pl.BlockSpec((1, tf), lambda i, j: (0, j + nj)),
                  pl.BlockSpec((tf, d), lambda i, j: (j, 0)),
                  pl.BlockSpec((1, d), lambda i, j: (0, 0))],
        out_specs=pl.BlockSpec((tm, d), lambda i, j: (i, 0)),
        out_shape=jax.ShapeDtypeStruct((m, d), F32),
        scratch_shapes=[pltpu.VMEM((tm, d), BF16),
                        pltpu.VMEM((tm + halo, tf), F32),
                        pltpu.VMEM((tm + halo, tf), F32),
                        pltpu.VMEM((nj, halo, tf), F32),
                        pltpu.VMEM((nj, halo, tf), F32)],
        compiler_params=_params(2),
        name="conv_ffn",
    )(x, gain.reshape(1, d), w_up, w_up, conv_w, conv_w, conv_b.reshape(1, -1), conv_b.reshape(1, -1),
      w_down, gain_final.reshape(1, d))


def kernel(x, mem, rel_bias, norm_mix, w_in, ssm_lambda_re, ssm_lambda_im, ssm_log_dt, ssm_b_re, ssm_b_im, ssm_c_re, ssm_c_im, ssm_d, ssm_w_glu, ssm_b_glu, w_branch_a, w_branch_b, w_out, norm_cross, norm_mem, w_cross_q, w_cross_kv, w_cross_o, norm_ffn, w_up, ffn_conv_w, ffn_conv_b, w_down, norm_final):
    batch, seq, d_model = x.shape
    mem_len = mem.shape[1]
    depth = w_in.shape[0]
    assert depth == 1, "the final rmsnorm is fused into the last layer's ConvFFN kernel"
    assert seq % ROW_TILE == 0 and seq % (COUNT_STEP * KEY_CHUNK) == 0 and seq >= 4 * TOPK_MAX
    m = batch * seq
    xf = x.reshape(m, d_model)
    memf = mem.reshape(batch * mem_len, d_model)
    bias_tiles = _bias_tiles(rel_bias)

    splits = (ATTN_WIDTH, KV_WIDTH, KV_WIDTH, IDX_HEADS * IDX_DIM, IDX_DIM, IDX_HEADS, SSM_WIDTH, d_model, d_model)
    offs = [int(o) for o in np.cumsum((0,) + splits)]
    for l in range(depth):
        w = w_in[l]
        wq, wk, wv, wqi, wki, wwi, wss, wga, wgb = [w[:, offs[n]:offs[n + 1]] for n in range(len(splits))]
        pad = jnp.zeros((d_model, V7X_LANES - IDX_DIM - IDX_HEADS), w.dtype)
        w_t = jnp.concatenate([wq, wqi, wv, wwi], axis=1).T.astype(BF16)
        w_n = jnp.concatenate([wk, wki, wwi, pad, wss], axis=1).astype(BF16)
        w_gate = jnp.concatenate([wga, wgb], axis=1).astype(BF16)

        q_t, v_t, wi_t, k_n, us = _proj_attn(xf, norm_mix[l], w_t, w_n)
        gates = _norm_matmul(xf, norm_mix[l], w_gate, BF16, GATE_COL_TILE, "proj_gates", sigmoid=True)

        y_a = _sparse_attention(bias_tiles, q_t, v_t, wi_t, k_n, batch, seq)

        bre_bd = _block_diag(jnp.transpose(ssm_b_re[l], (0, 2, 1)))
        bim_bd = _block_diag(jnp.transpose(ssm_b_im[l], (0, 2, 1)))
        cre_bd = _block_diag(jnp.transpose(ssm_c_re[l], (0, 2, 1))).astype(BF16)
        cim_bd = _block_diag(jnp.transpose(ssm_c_im[l], (0, 2, 1))).astype(BF16)
        log_dt = jnp.broadcast_to(ssm_log_dt[l][:, None], (SSM_GROUPS, SSM_STATE)).reshape(1, SSM_STATES)
        bu_re, bu_im, a_re, a_im = _s5_input(us, ssm_lambda_re[l].reshape(1, SSM_STATES),
                                             ssm_lambda_im[l].reshape(1, SSM_STATES), log_dt, bre_bd, bim_bd)
        state_rows = SSM_STATES // V7X_LANES
        sshape = (batch, seq, state_rows, V7X_LANES)
        xs_re, xs_im = _s5_scan(a_re.reshape(state_rows, V7X_LANES), a_im.reshape(state_rows, V7X_LANES),
                                bu_re.reshape(sshape), bu_im.reshape(sshape))
        y_b = _s5_output(xs_re.reshape(m, SSM_STATES), xs_im.reshape(m, SSM_STATES), us, cre_bd, cim_bd,
                         ssm_d[l].reshape(1, SSM_WIDTH), ssm_w_glu[l].astype(BF16),
                         ssm_b_glu[l].reshape(1, SSM_WIDTH))

        kvx = _norm_matmul(memf, norm_mem[l], w_cross_kv[l].astype(BF16), BF16, COL_TILE, "cross_kv")
        xf = _post_mixer(xf, y_a, y_b, gates, w_branch_a[l].astype(BF16), w_branch_b[l].astype(BF16),
                         w_out[l].astype(BF16), norm_cross[l], w_cross_q[l].astype(BF16), kvx,
                         w_cross_o[l].astype(BF16), batch, seq, mem_len)

        xf = _conv_ffn(xf, norm_ffn[l], w_up[l].astype(BF16), ffn_conv_w[l], ffn_conv_b[l],
                       w_down[l].astype(BF16), norm_final, seq)
    return xf.reshape(batch, seq, d_model)
```

```python
import functools
import math

import jax
import jax.numpy as jnp
import numpy as np
from jax import lax
from jax.experimental import pallas as pl
from jax.experimental.pallas import tpu as pltpu

N_HEADS_A = 8
N_KV_A = 2
HEAD_DIM_A = 128
ATTN_WIDTH = N_HEADS_A * HEAD_DIM_A
KV_WIDTH = N_KV_A * HEAD_DIM_A
IDX_HEADS = 16
IDX_DIM = 64
TOPK_MAX = 256
BLOCK_Q = 128
REL_BUCKETS = 32
REL_MAX_DIST = 128
SSM_GROUP = 16
SSM_GROUPS = 32
SSM_STATE = 64
SSM_WIDTH = SSM_GROUP * SSM_GROUPS
SSM_STATES = SSM_GROUPS * SSM_STATE
X_HEADS = 4
X_HEAD_DIM = 128
CONV_WIDTH = 3
EPS = 1e-6

V7X_LANES = 128
V7X_SUBLANES = 8
V7X_BF16_ROWS = 16
V7X_VMEM_BYTES = 64 * 1024 * 1024
VMEM_RESERVE_BYTES = 8 * 1024 * 1024
VMEM_LIMIT_BYTES = V7X_VMEM_BYTES - VMEM_RESERVE_BYTES

GRP = N_HEADS_A // N_KV_A
GRP_LANES = GRP * BLOCK_Q
V_ROWS = HEAD_DIM_A + V7X_BF16_ROWS
KEY_CHUNK = 2 * BLOCK_Q
N_BIAS_TILES = 4
MAX_BISECT_ITERS = 64
BISECT_FIRST = 12
BISECT_ROUND = 2
COUNT_ROWS = 64
COUNT_STEP = 4
LOG2_E = math.log2(math.e)
MAX_INIT = -1e30
ROW_TILE = 1024
SMALL_ROW_TILE = 512
COL_TILE = 512
GATE_COL_TILE = 2048
FFN_ROW_TILE = 512
FFN_ROW_SLABS = 2
SCAN_CHUNK = 128

F32 = jnp.float32
BF16 = jnp.bfloat16
_NT = (((1,), (1,)), ((), ()))


def _params(n_axes):
    return pltpu.CompilerParams(dimension_semantics=("arbitrary",) * n_axes,
                                vmem_limit_bytes=VMEM_LIMIT_BYTES)


def _rms(x, gain, eps):
    return x * lax.rsqrt(jnp.mean(x * x, axis=-1, keepdims=True) + eps) * gain


def _norm_matmul_kernel(x_ref, g_ref, w_ref, o_ref, xn_ref, *, sigmoid):
    @pl.when(pl.program_id(1) == 0)
    def _():
        xn_ref[...] = _rms(x_ref[...], g_ref[...], EPS).astype(BF16)

    tm = o_ref.shape[0]
    rows = min(tm, SMALL_ROW_TILE) if sigmoid else tm
    for r in range(0, tm, rows):
        y = jnp.dot(xn_ref[r:r + rows], w_ref[...], preferred_element_type=F32)
        if sigmoid:
            y = 0.5 * jnp.tanh(0.5 * y) + 0.5
        o_ref[r:r + rows] = y.astype(o_ref.dtype)


def _norm_matmul(x, gain, w, out_dtype, tn, name, sigmoid=False):
    m, k = x.shape
    n = w.shape[1]
    tm = min(ROW_TILE, m)
    return pl.pallas_call(
        functools.partial(_norm_matmul_kernel, sigmoid=sigmoid),
        grid=(m // tm, n // tn),
        in_specs=[pl.BlockSpec((tm, k), lambda i, j: (i, 0)),
                  pl.BlockSpec((1, k), lambda i, j: (0, 0)),
                  pl.BlockSpec((k, tn), lambda i, j: (0, j))],
        out_specs=pl.BlockSpec((tm, tn), lambda i, j: (i, j)),
        out_shape=jax.ShapeDtypeStruct((m, n), out_dtype),
        scratch_shapes=[pltpu.VMEM((tm, k), BF16)],
        compiler_params=_params(2),
        name=name,
    )(x, gain.reshape(1, k), w)


def _proj_attn_kernel(x_ref, g_ref, wt_ref, wn_ref, qt_ref, vt_ref, wit_ref, kn_ref, us_ref):
    xn = _rms(x_ref[...], g_ref[...], EPS).astype(BF16)
    res = lax.dot_general(wt_ref[...], xn, _NT, preferred_element_type=F32)
    nq = qt_ref.shape[0]
    qt_ref[0:ATTN_WIDTH] = (res[0:ATTN_WIDTH] * (HEAD_DIM_A ** -0.5 * LOG2_E)).astype(BF16)
    qt_ref[ATTN_WIDTH:nq] = res[ATTN_WIDTH:nq].astype(BF16)
    pad_rows = V_ROWS - HEAD_DIM_A
    ones_row = (lax.broadcasted_iota(jnp.int32, (pad_rows, KEY_CHUNK), 0) == 0).astype(BF16)
    for c in range(vt_ref.shape[0]):
        for g in range(N_KV_A):
            rows = res[nq + g * HEAD_DIM_A:nq + (g + 1) * HEAD_DIM_A, c * KEY_CHUNK:(c + 1) * KEY_CHUNK]
            vt_ref[c, g * V_ROWS:g * V_ROWS + HEAD_DIM_A] = rows.astype(BF16)
            vt_ref[c, g * V_ROWS + HEAD_DIM_A:(g + 1) * V_ROWS] = ones_row
    wit_ref[...] = res[nq + KV_WIDTH:nq + KV_WIDTH + IDX_HEADS]
    plain = jnp.dot(xn, wn_ref[...], preferred_element_type=F32)
    n_kn = kn_ref.shape[1]
    kn_ref[...] = plain[:, 0:n_kn].astype(BF16)
    us_ref[...] = plain[:, n_kn:]


def _proj_attn(x, gain, w_t, w_n):
    m, k = x.shape
    tm = SMALL_ROW_TILE
    nq = ATTN_WIDTH + IDX_HEADS * IDX_DIM
    return pl.pallas_call(
        _proj_attn_kernel,
        grid=(m // tm,),
        in_specs=[pl.BlockSpec((tm, k), lambda i: (i, 0)),
                  pl.BlockSpec((1, k), lambda i: (0, 0)),
                  pl.BlockSpec(w_t.shape, lambda i: (0, 0)),
                  pl.BlockSpec(w_n.shape, lambda i: (0, 0))],
        out_specs=[pl.BlockSpec((nq, tm), lambda i: (0, i)),
                   pl.BlockSpec((tm // KEY_CHUNK, N_KV_A * V_ROWS, KEY_CHUNK), lambda i: (i, 0, 0)),
                   pl.BlockSpec((IDX_HEADS, tm), lambda i: (0, i)),
                   pl.BlockSpec((tm, w_n.shape[1] - SSM_WIDTH), lambda i: (i, 0)),
                   pl.BlockSpec((tm, SSM_WIDTH), lambda i: (i, 0))],
        out_shape=[jax.ShapeDtypeStruct((nq, m), BF16),
                   jax.ShapeDtypeStruct((m // KEY_CHUNK, N_KV_A * V_ROWS, KEY_CHUNK), BF16),
                   jax.ShapeDtypeStruct((IDX_HEADS, m), F32),
                   jax.ShapeDtypeStruct((m, w_n.shape[1] - SSM_WIDTH), BF16),
                   jax.ShapeDtypeStruct((m, SSM_WIDTH), F32)],
        compiler_params=_params(1),
        name="proj_attn",
    )(x, gain.reshape(1, k), w_t, w_n)


def _t5_bucket(n):
    max_exact = REL_BUCKETS // 2
    n = jnp.maximum(n, 0)
    nf = jnp.maximum(n, 1).astype(F32)
    large = max_exact + (jnp.log(nf / max_exact) / math.log(REL_MAX_DIST / max_exact)
                         * (REL_BUCKETS - max_exact)).astype(jnp.int32)
    large = jnp.minimum(large, REL_BUCKETS - 1)
    return jnp.where(n < max_exact, n, large)


def _bias_tiles_kernel(rb_ref, o_ref):
    shape = (KEY_CHUNK, BLOCK_Q)
    key = lax.broadcasted_iota(jnp.int32, shape, 0)
    qry = lax.broadcasted_iota(jnp.int32, shape, 1)
    for o in range(N_BIAS_TILES):
        bucket = _t5_bucket(o * BLOCK_Q + qry - key)
        for h in range(N_HEADS_A):
            def pick(b, t, bucket=bucket, h=h):
                return jnp.where(bucket == b, rb_ref[b, h], t)
            tile = lax.fori_loop(0, REL_BUCKETS, pick, jnp.zeros(shape, F32))
            j = h % GRP
            o_ref[o, h // GRP, :, j * BLOCK_Q:(j + 1) * BLOCK_Q] = (tile - rb_ref[REL_BUCKETS - 1, h]) * LOG2_E


def _bias_tiles(rel_bias):
    return pl.pallas_call(
        _bias_tiles_kernel,
        in_specs=[pl.BlockSpec(memory_space=pltpu.SMEM)],
        out_specs=pl.BlockSpec(memory_space=pltpu.VMEM),
        out_shape=jax.ShapeDtypeStruct((N_BIAS_TILES, N_KV_A, KEY_CHUNK, GRP_LANES), F32),
        name="rel_bias_tiles",
    )(rel_bias)


def _sparse_attn_kernel(bias_ref, qt_ref, qit_ref, wit_ref, kw_ref, k_ref, vt_ref, o_ref,
                        qz_ref, qiz_ref, sc_ref, prod_ref, m_ref, l_ref, acc_ref, tri_ref, ties_ref):
    i = pl.program_id(1)
    t0 = i * BLOCK_Q
    n_chunks = (i + 2) // 2
    tile = (KEY_CHUNK, BLOCK_Q)
    row1 = (1, BLOCK_Q)
    k_sel = float(TOPK_MAX)

    @pl.when((pl.program_id(0) == 0) & (i == 0))
    def _():
        qiz_ref[...] = jnp.zeros(qiz_ref.shape, BF16)

    for h in range(IDX_HEADS):
        qiz_ref[0:IDX_DIM, h * BLOCK_Q:(h + 1) * BLOCK_Q] = qit_ref[h * IDX_DIM:(h + 1) * IDX_DIM, :]
    for h in range(N_HEADS_A):
        qz_ref[:, h * BLOCK_Q:(h + 1) * BLOCK_Q] = qt_ref[h * HEAD_DIM_A:(h + 1) * HEAD_DIM_A, :]
    w_rows = wit_ref[...] * ((IDX_HEADS * IDX_DIM) ** -0.5)

    key_row = lax.broadcasted_iota(jnp.int32, tile, 0)
    q_pos = t0 + lax.broadcasted_iota(jnp.int32, tile, 1)

    last = n_chunks - 1
    n_pairs = (n_chunks + 1) // 2

    def idx_products(kc, slot):
        ks = pl.multiple_of(jnp.minimum(kc, last) * KEY_CHUNK, KEY_CHUNK)
        prod_ref[slot] = jnp.dot(kw_ref[pl.ds(ks, KEY_CHUNK), :], qiz_ref[...], preferred_element_type=F32)

    def reduce_scores(kc, slot, carry):
        rmax, rmin = carry
        kc = jnp.minimum(kc, last)
        s = jnp.zeros(tile, F32)
        for h in range(IDX_HEADS):
            d = prod_ref[slot, :, h * BLOCK_Q:(h + 1) * BLOCK_Q]
            s = s + jnp.maximum(d, 0.0) * w_rows[h:h + 1, :]
        causal = (kc * KEY_CHUNK + key_row) <= q_pos
        sc_ref[kc] = jnp.where(causal, s, -jnp.inf)
        rmax = jnp.maximum(rmax, jnp.max(jnp.where(causal, s, -jnp.inf), axis=0, keepdims=True))
        rmin = jnp.minimum(rmin, jnp.min(jnp.where(causal, s, jnp.inf), axis=0, keepdims=True))
        return rmax, rmin

    def run_pairs(start, stop, pair_body, carry):
        n_double = (stop - start) // 2

        def double(t, c):
            j = start + 2 * t
            return pair_body(j + 1, pair_body(j, c))

        carry = lax.fori_loop(0, n_double, double, carry)
        return lax.fori_loop(start + 2 * n_double, stop, pair_body, carry)

    def score_pair(j, carry, prefetch=True):
        idx_products(2 * j + 1, 1)
        carry = reduce_scores(2 * j, 0, carry)
        if prefetch:
            idx_products(2 * j + 2, 0)
        return reduce_scores(2 * j + 1, 1, carry)

    idx_products(0, 0)
    carry = lax.fori_loop(0, n_pairs - 1, score_pair,
                          (jnp.full(row1, -jnp.inf, F32), jnp.full(row1, jnp.inf, F32)))
    rmax, rmin = score_pair(n_pairs - 1, carry, prefetch=False)

    n_causal = (t0 + lax.broadcasted_iota(jnp.int32, row1, 1) + 1).astype(F32)
    short = n_causal <= k_sel

    n_steps = (n_chunks + COUNT_STEP - 1) // COUNT_STEP
    for d in range(COUNT_STEP - 1):
        @pl.when(n_chunks + d < n_steps * COUNT_STEP)
        def _(d=d):
            sc_ref[n_chunks + d] = jnp.full(tile, -jnp.inf, F32)

    def count_ge(t):
        def body(j, c):
            for d in range(COUNT_STEP):
                hit = jnp.where(sc_ref[COUNT_STEP * j + d] >= t, 1.0, 0.0)
                c = c + jnp.sum(hit.reshape(KEY_CHUNK // COUNT_ROWS, COUNT_ROWS, BLOCK_Q), axis=0)
            return c
        c = lax.fori_loop(0, n_steps, body, jnp.zeros((COUNT_ROWS, BLOCK_Q), F32))
        return jnp.sum(c, axis=0, keepdims=True)

    def halve(_, carry):
        lo, hi, c_lo, _ = carry
        mid = lo + (hi - lo) * 0.5
        cnt = count_ge(mid)
        enough = cnt >= k_sel
        shrinks = jnp.where((mid > lo) & (mid < hi), 1.0, 0.0)
        return jnp.where(enough, mid, lo), jnp.where(enough, hi, mid), jnp.where(enough, cnt, c_lo), shrinks

    def halvings(count, lo, hi, c_lo):
        lo, hi, c_lo, shrinks = lax.fori_loop(0, count, halve, (lo, hi, c_lo, jnp.ones(row1, F32)))
        settled = short | (c_lo == k_sel) | (shrinks == 0.0)
        return jnp.sum(jnp.where(settled, 0.0, 1.0)), lo, hi, c_lo

    def pending(carry):
        done, todo, _, _, _ = carry
        return (done < MAX_BISECT_ITERS) & (todo > 0.0)

    def bisect_round(carry):
        done, _, lo, hi, c_lo = carry
        return (done + BISECT_ROUND,) + halvings(BISECT_ROUND, lo, hi, c_lo)

    hi0 = rmax + jnp.maximum(jnp.abs(rmax) * 2.0 ** -22, jnp.finfo(F32).tiny)
    first = (jnp.int32(BISECT_FIRST),) + halvings(BISECT_FIRST, rmin, hi0, n_causal)
    _, _, lo, hi, c_lo = lax.while_loop(pending, bisect_round, first)
    thr = jnp.where(short, jnp.finfo(F32).min, lo)
    tied = jnp.logical_not(short) & (c_lo > k_sel)
    n_tied = jnp.sum(jnp.where(tied, 1.0, 0.0))

    m_ref[...] = jnp.full(m_ref.shape, MAX_INIT, F32)
    l_ref[...] = jnp.zeros(l_ref.shape, F32)
    acc_ref[...] = jnp.zeros(acc_ref.shape, F32)

    def qk_products(kc, slot):
        ks = pl.multiple_of(jnp.minimum(kc, last) * KEY_CHUNK, KEY_CHUNK)
        for g in range(N_KV_A):
            k_c = k_ref[pl.ds(ks, KEY_CHUNK), g * HEAD_DIM_A:(g + 1) * HEAD_DIM_A]
            prod_ref[slot, :, g * GRP_LANES:(g + 1) * GRP_LANES] = jnp.dot(
                k_c, qz_ref[:, g * GRP_LANES:(g + 1) * GRP_LANES], preferred_element_type=F32)

    def softmax_pv(kc, slot, near, ties, upper=None, quota=None):
        live = (kc <= last) if near else True
        kc = jnp.minimum(kc, last) if near else kc
        sc = sc_ref[kc]
        if ties:
            above = sc >= upper
            tie = (sc >= thr) & jnp.logical_not(above)
            tie_f = jnp.where(tie, 1.0, 0.0)
            upto = jnp.dot(tri_ref[...], tie_f.astype(BF16), preferred_element_type=F32)
            before = ties_ref[...]
            keep = (above | (tie & (before + upto - tie_f < quota))) & live
            ties_ref[...] = before + jnp.where(live, upto[KEY_CHUNK - 1:KEY_CHUNK], 0.0)
        else:
            keep = (sc >= thr) & live
        mask = jnp.where(keep, 0.0, -jnp.inf)
        mask = jnp.concatenate([mask] * GRP, axis=1)
        for g in range(N_KV_A):
            lg = prod_ref[slot, :, g * GRP_LANES:(g + 1) * GRP_LANES]
            if near:
                lg = lg + bias_ref[jnp.minimum(i - 2 * kc, N_BIAS_TILES - 1), g]
            lg = lg + mask
            m_old = m_ref[g]
            m_new = jnp.maximum(m_old, jnp.max(lg, axis=0, keepdims=True))
            alpha = jnp.exp2(m_old - m_new)
            p = jnp.exp2(lg - m_new)
            pv = jnp.dot(vt_ref[kc, g * V_ROWS:(g + 1) * V_ROWS, :], p.astype(BF16),
                         preferred_element_type=F32)
            l_ref[g] = alpha * l_ref[g] + pv[HEAD_DIM_A:HEAD_DIM_A + 1]
            acc_ref[g] = alpha * acc_ref[g] + pv[0:HEAD_DIM_A]
            m_ref[g] = m_new

    def attend(near, ties, prefetch=True, **tie_args):
        def body(j, carry):
            qk_products(2 * j + 1, 1)
            softmax_pv(2 * j, 0, near, ties, **tie_args)
            if prefetch:
                qk_products(2 * j + 2, 0)
            softmax_pv(2 * j + 1, 1, near, ties, **tie_args)
            return carry
        return body

    def attend_all(ties, **tie_args):
        run_pairs(0, n_far, attend(False, ties, **tie_args), 0)
        lax.fori_loop(n_far, n_pairs - 1, attend(True, ties, **tie_args), 0)
        attend(True, ties, prefetch=False, **tie_args)(n_pairs - 1, 0)

    n_far = jnp.maximum(n_pairs - 2, 0)
    qk_products(0, 0)

    @pl.when(n_tied == 0.0)
    def _():
        attend_all(False)

    @pl.when(n_tied > 0.0)
    def _():
        row = lax.broadcasted_iota(jnp.int32, (KEY_CHUNK, KEY_CHUNK), 0)
        col = lax.broadcasted_iota(jnp.int32, (KEY_CHUNK, KEY_CHUNK), 1)
        tri_ref[...] = (col <= row).astype(BF16)
        ties_ref[...] = jnp.zeros(ties_ref.shape, F32)
        attend_all(True, upper=jnp.where(tied, hi, thr), quota=jnp.where(tied, k_sel - count_ge(hi), 0.0))

    for g in range(N_KV_A):
        out_t = acc_ref[g] / l_ref[g]
        for j in range(GRP):
            h = g * GRP + j
            o_ref[:, h * HEAD_DIM_A:(h + 1) * HEAD_DIM_A] = (
                out_t[:, j * BLOCK_Q:(j + 1) * BLOCK_Q].T.astype(o_ref.dtype))


def _sparse_attention(bias_tiles, q_t, v_t, wi_t, k_n, batch, seq):
    nb = seq // BLOCK_Q
    n_seq_chunks = seq // KEY_CHUNK
    kw_blk = KV_WIDTH // V7X_LANES
    return pl.pallas_call(
        _sparse_attn_kernel,
        grid=(batch, nb),
        in_specs=[pl.BlockSpec(bias_tiles.shape, lambda b, i: (0, 0, 0, 0)),
                  pl.BlockSpec((ATTN_WIDTH, BLOCK_Q), lambda b, i: (0, b * nb + i)),
                  pl.BlockSpec((IDX_HEADS * IDX_DIM, BLOCK_Q), lambda b, i: (1, b * nb + i)),
                  pl.BlockSpec((IDX_HEADS, BLOCK_Q), lambda b, i: (0, b * nb + i)),
                  pl.BlockSpec((seq, V7X_LANES), lambda b, i: (b, kw_blk)),
                  pl.BlockSpec((seq, KV_WIDTH), lambda b, i: (b, 0)),
                  pl.BlockSpec((n_seq_chunks, N_KV_A * V_ROWS, KEY_CHUNK), lambda b, i: (b, 0, 0))],
        out_specs=pl.BlockSpec((BLOCK_Q, ATTN_WIDTH), lambda b, i: (b * nb + i, 0)),
        out_shape=jax.ShapeDtypeStruct((batch * seq, ATTN_WIDTH), BF16),
        scratch_shapes=[pltpu.VMEM((HEAD_DIM_A, N_HEADS_A * BLOCK_Q), BF16),
                        pltpu.VMEM((V7X_LANES, IDX_HEADS * BLOCK_Q), BF16),
                        pltpu.VMEM((n_seq_chunks, KEY_CHUNK, BLOCK_Q), F32),
                        pltpu.VMEM((2, KEY_CHUNK, IDX_HEADS * BLOCK_Q), F32),
                        pltpu.VMEM((N_KV_A, 1, GRP_LANES), F32),
                        pltpu.VMEM((N_KV_A, 1, GRP_LANES), F32),
                        pltpu.VMEM((N_KV_A, HEAD_DIM_A, GRP_LANES), F32),
                        pltpu.VMEM((KEY_CHUNK, KEY_CHUNK), BF16),
                        pltpu.VMEM((1, BLOCK_Q), F32)],
        compiler_params=_params(2),
        name="sparse_attention",
    )(bias_tiles, q_t, q_t, wi_t, k_n, k_n, v_t)


def _s5_input_kernel(u_ref, lr_ref, li_ref, ldt_ref, bre_ref, bim_ref,
                     bur_ref, bui_ref, ar_ref, ai_ref, bbr_ref, bbi_ref):
    @pl.when(pl.program_id(0) == 0)
    def _():
        dt = jnp.exp(ldt_ref[...])
        lr = lr_ref[...]
        li = li_ref[...]
        mag = jnp.exp(lr * dt)
        ar = mag * jnp.cos(li * dt)
        ai = mag * jnp.sin(li * dt)
        den = lr * lr + li * li
        nr = ar - 1.0
        cr = (nr * lr + ai * li) / den
        ci = (ai * lr - nr * li) / den
        bre = bre_ref[...]
        bim = bim_ref[...]
        bbr_ref[...] = (cr * bre - ci * bim).astype(BF16)
        bbi_ref[...] = (cr * bim + ci * bre).astype(BF16)
        ar_ref[...] = ar
        ai_ref[...] = ai

    u = u_ref[...].astype(BF16)
    bur_ref[...] = jnp.dot(u, bbr_ref[...], preferred_element_type=F32)
    bui_ref[...] = jnp.dot(u, bbi_ref[...], preferred_element_type=F32)


def _s5_input(us, lam_re, lam_im, log_dt, bre_bd, bim_bd):
    m = us.shape[0]
    tm = SMALL_ROW_TILE
    row = pl.BlockSpec((1, SSM_STATES), lambda i: (0, 0))
    mat = pl.BlockSpec((SSM_WIDTH, SSM_STATES), lambda i: (0, 0))
    out = pl.BlockSpec((tm, SSM_STATES), lambda i: (i, 0))
    return pl.pallas_call(
        _s5_input_kernel,
        grid=(m // tm,),
        in_specs=[pl.BlockSpec((tm, SSM_WIDTH), lambda i: (i, 0)), row, row, row, mat, mat],
        out_specs=[out, out, row, row],
        out_shape=[jax.ShapeDtypeStruct((m, SSM_STATES), F32)] * 2
        + [jax.ShapeDtypeStruct((1, SSM_STATES), F32)] * 2,
        scratch_shapes=[pltpu.VMEM((SSM_WIDTH, SSM_STATES), BF16)] * 2,
        compiler_params=_params(1),
        name="s5_input",
    )(us, lam_re, lam_im, log_dt, bre_bd, bim_bd)


def _s5_scan_kernel(ar_ref, ai_ref, br_ref, bi_ref, xr_ref, xi_ref, sr_ref, si_ref):
    @pl.when(pl.program_id(0) == 0)
    def _():
        sr_ref[...] = jnp.zeros(sr_ref.shape, F32)
        si_ref[...] = jnp.zeros(si_ref.shape, F32)

    ar = ar_ref[...]
    ai = ai_ref[...]
    steps = br_ref.shape[1]

    def step(t, carry):
        xr, xi = carry
        nr = ar * xr - ai * xi + br_ref[:, t]
        ni = ar * xi + ai * xr + bi_ref[:, t]
        xr_ref[:, t] = nr.astype(xr_ref.dtype)
        xi_ref[:, t] = ni.astype(xi_ref.dtype)
        return nr, ni

    xr, xi = lax.fori_loop(0, steps, step, (sr_ref[...], si_ref[...]), unroll=8)
    sr_ref[...] = xr
    si_ref[...] = xi


def _s5_scan(a_re, a_im, bu_re, bu_im):
    batch, seq, rows, lanes = bu_re.shape
    blk = pl.BlockSpec((batch, SCAN_CHUNK, rows, lanes), lambda c: (0, c, 0, 0))
    par = pl.BlockSpec((rows, lanes), lambda c: (0, 0))
    return pl.pallas_call(
        _s5_scan_kernel,
        grid=(seq // SCAN_CHUNK,),
        in_specs=[par, par, blk, blk],
        out_specs=[blk, blk],
        out_shape=[jax.ShapeDtypeStruct(bu_re.shape, BF16)] * 2,
        scratch_shapes=[pltpu.VMEM((batch, rows, lanes), F32)] * 2,
        compiler_params=_params(1),
        name="s5_scan",
    )(a_re, a_im, bu_re, bu_im)


def _s5_output_kernel(xr_ref, xi_ref, u_ref, cr_ref, ci_ref, d_ref, wg_ref, bg_ref, o_ref):
    y = (jnp.dot(xr_ref[...], cr_ref[...], preferred_element_type=F32)
         - jnp.dot(xi_ref[...], ci_ref[...], preferred_element_type=F32)
         + d_ref[...] * u_ref[...])
    g = jax.nn.gelu(y)
    gate = jax.nn.sigmoid(jnp.dot(g.astype(BF16), wg_ref[...], preferred_element_type=F32) + bg_ref[...])
    o_ref[...] = (g * gate).astype(o_ref.dtype)


def _s5_output(xr, xi, us, cre_bd, cim_bd, d_row, w_glu, b_glu):
    m = us.shape[0]
    tm = SMALL_ROW_TILE
    state = pl.BlockSpec((tm, SSM_STATES), lambda i: (i, 0))
    chan = pl.BlockSpec((tm, SSM_WIDTH), lambda i: (i, 0))
    cmat = pl.BlockSpec((SSM_STATES, SSM_WIDTH), lambda i: (0, 0))
    row = pl.BlockSpec((1, SSM_WIDTH), lambda i: (0, 0))
    return pl.pallas_call(
        _s5_output_kernel,
        grid=(m // tm,),
        in_specs=[state, state, chan, cmat, cmat, row,
                  pl.BlockSpec((SSM_WIDTH, SSM_WIDTH), lambda i: (0, 0)), row],
        out_specs=chan,
        out_shape=jax.ShapeDtypeStruct((m, SSM_WIDTH), BF16),
        compiler_params=_params(1),
        name="s5_output",
    )(xr, xi, us, cre_bd, cim_bd, d_row, w_glu, b_glu)


def _block_diag(blocks):
    g, r, c = blocks.shape
    eye = jnp.eye(g, dtype=blocks.dtype)
    return (eye[:, None, :, None] * blocks[:, :, None, :]).reshape(g * r, g * c)


def _post_mixer_kernel(x_ref, ya_ref, yb_ref, ga_ref, gb_ref, wa_ref, wb_ref, wo_ref, gc_ref, wq_ref,
                       kv_ref, wco_ref, o_ref):
    width = X_HEADS * X_HEAD_DIM
    scale = X_HEAD_DIM ** -0.5
    za = jnp.dot(ya_ref[...], wa_ref[...], preferred_element_type=F32)
    zb = jnp.dot(yb_ref[...], wb_ref[...], preferred_element_type=F32)
    z = (ga_ref[...].astype(F32) * za + gb_ref[...].astype(F32) * zb).astype(BF16)
    x1 = x_ref[...] + jnp.dot(z, wo_ref[...], preferred_element_type=F32)
    xn = _rms(x1, gc_ref[...], EPS).astype(BF16)
    q = jnp.dot(xn, wq_ref[...], preferred_element_type=F32).astype(BF16)
    heads = []
    for h in range(X_HEADS):
        cols = slice(h * X_HEAD_DIM, (h + 1) * X_HEAD_DIM)
        vcols = slice(width + h * X_HEAD_DIM, width + (h + 1) * X_HEAD_DIM)
        lg = lax.dot_general(q[:, cols], kv_ref[:, cols], _NT, preferred_element_type=F32) * scale
        p = jnp.exp(lg - jnp.max(lg, axis=-1, keepdims=True))
        o = jnp.dot(p.astype(BF16), kv_ref[:, vcols], preferred_element_type=F32)
        heads.append((o / jnp.sum(p, axis=-1, keepdims=True)).astype(BF16))
    attn = jnp.concatenate(heads, axis=1)
    o_ref[...] = x1 + jnp.dot(attn, wco_ref[...], preferred_element_type=F32)


def _post_mixer(x, ya, yb, gates, w_a, w_b, w_out, gain_cross, w_q, kv, w_co, batch, seq, mem_len):
    d = x.shape[1]
    width = X_HEADS * X_HEAD_DIM
    tm = SMALL_ROW_TILE
    nt = seq // tm
    row = pl.BlockSpec((tm, d), lambda b, i: (b * nt + i, 0))
    return pl.pallas_call(
        _post_mixer_kernel,
        grid=(batch, nt),
        in_specs=[row,
                  pl.BlockSpec((tm, ya.shape[1]), lambda b, i: (b * nt + i, 0)),
                  pl.BlockSpec((tm, yb.shape[1]), lambda b, i: (b * nt + i, 0)),
                  row,
                  pl.BlockSpec((tm, d), lambda b, i: (b * nt + i, 1)),
                  pl.BlockSpec(w_a.shape, lambda b, i: (0, 0)),
                  pl.BlockSpec(w_b.shape, lambda b, i: (0, 0)),
                  pl.BlockSpec((d, d), lambda b, i: (0, 0)),
                  pl.BlockSpec((1, d), lambda b, i: (0, 0)),
                  pl.BlockSpec((d, width), lambda b, i: (0, 0)),
                  pl.BlockSpec((mem_len, 2 * width), lambda b, i: (b, 0)),
                  pl.BlockSpec((width, d), lambda b, i: (0, 0))],
        out_specs=row,
        out_shape=jax.ShapeDtypeStruct(x.shape, F32),
        compiler_params=_params(2),
        name="post_mixer",
    )(x, ya, yb, gates, gates, w_a, w_b, w_out, gain_cross.reshape(1, d), w_q, kv, w_co)


def _conv_ffn_kernel(x_ref, g_ref, wa_ref, wb_ref, cwa_ref, cwb_ref, cba_ref, cbb_ref, wd_ref, gf_ref,
                     o_ref, xn_ref, ha_ref, hb_ref, ta_ref, tb_ref, *, tiles_per_seq):
    i = pl.program_id(0)
    j = pl.program_id(1)
    tm = x_ref.shape[0]
    halo = V7X_SUBLANES

    @pl.when(j == 0)
    def _():
        xn_ref[...] = _rms(x_ref[...], g_ref[...], EPS).astype(BF16)
        o_ref[...] = jnp.zeros(o_ref.shape, F32)

    @pl.when(i % tiles_per_seq == 0)
    def _():
        ha_ref[0:halo] = jnp.zeros((halo, ha_ref.shape[1]), F32)
        hb_ref[0:halo] = jnp.zeros((halo, hb_ref.shape[1]), F32)

    @pl.when(i % tiles_per_seq != 0)
    def _():
        ha_ref[0:halo] = ta_ref[j]
        hb_ref[0:halo] = tb_ref[j]

    rows = tm // FFN_ROW_SLABS
    for s in range(FFN_ROW_SLABS):
        xn = xn_ref[s * rows:(s + 1) * rows]
        ha_ref[halo + s * rows:halo + (s + 1) * rows] = jnp.dot(xn, wa_ref[...], preferred_element_type=F32)
        hb_ref[halo + s * rows:halo + (s + 1) * rows] = jnp.dot(xn, wb_ref[...], preferred_element_type=F32)
    ta_ref[j] = ha_ref[tm:tm + halo]
    tb_ref[j] = hb_ref[tm:tm + halo]

    def conv(h_ref, cw_ref, cb_ref, r0):
        out = cb_ref[...]
        for tap in range(CONV_WIDTH):
            start = r0 + halo - (CONV_WIDTH - 1) + tap
            out = out + h_ref[start:start + rows] * cw_ref[tap:tap + 1]
        return out

    for s in range(FFN_ROW_SLABS):
        a = conv(ha_ref, cwa_ref, cba_ref, s * rows)
        b = conv(hb_ref, cwb_ref, cbb_ref, s * rows)
        act = (a * jax.nn.sigmoid(a) * b).astype(BF16)
        o_ref[s * rows:(s + 1) * rows] += jnp.dot(act, wd_ref[...], preferred_element_type=F32)

    @pl.when(j == pl.num_programs(1) - 1)
    def _():
        o_ref[...] = _rms(x_ref[...] + o_ref[...], gf_ref[...], EPS)


def _conv_ffn(x, gain, w_up, conv_w, conv_b, w_down, gain_final, seq):
    m, d = x.shape
    d_ff = w_down.shape[0]
    tm, tf = FFN_ROW_TILE, COL_TILE
    nj = d_ff // tf
    halo = V7X_SUBLANES
    return pl.pallas_call(
        functools.partial(_conv_ffn_kernel, tiles_per_seq=seq // tm),
        grid=(m // tm, nj),
        in_specs=[pl.BlockSpec((tm, d), lambda i, j: (i, 0)),
                  pl.BlockSpec((1, d), lambda i, j: (0, 0)),
                  pl.BlockSpec((d, tf), lambda i, j: (0, j)),
                  pl.BlockSpec((d, tf), lambda i, j: (0, j + nj)),
                  pl.BlockSpec((CONV_WIDTH, tf), lambda i, j: (0, j)),
                  pl.BlockSpec((CONV_WIDTH, tf), lambda i, j: (0, j + nj)),
                  pl.BlockSpec((1, tf), lambda i, j: (0, j)),
                  pl.BlockSpec((1, tf), lambda i, j: (0, j + nj)),
                  pl.BlockSpec((tf, d), lambda i, j: (j, 0)),
                  pl.BlockSpec((1, d), lambda i, j: (0, 0))],
        out_specs=pl.BlockSpec((tm, d), lambda i, j: (i, 0)),
        out_shape=jax.ShapeDtypeStruct((m, d), F32),
        scratch_shapes=[pltpu.VMEM((tm, d), BF16),
                        pltpu.VMEM((tm + halo, tf), F32),
                        pltpu.VMEM((tm + halo, tf), F32),
                        pltpu.VMEM((nj, halo, tf), F32),
                        pltpu.VMEM((nj, halo, tf), F32)],
        compiler_params=_params(2),
        name="conv_ffn",
    )(x, gain.reshape(1, d), w_up, w_up, conv_w, conv_w, conv_b.reshape(1, -1), conv_b.reshape(1, -1),
      w_down, gain_final.reshape(1, d))


def kernel(x, mem, rel_bias, norm_mix, w_in, ssm_lambda_re, ssm_lambda_im, ssm_log_dt, ssm_b_re, ssm_b_im, ssm_c_re, ssm_c_im, ssm_d, ssm_w_glu, ssm_b_glu, w_branch_a, w_branch_b, w_out, norm_cross, norm_mem, w_cross_q, w_cross_kv, w_cross_o, norm_ffn, w_up, ffn_conv_w, ffn_conv_b, w_down, norm_final):
    batch, seq, d_model = x.shape
    mem_len = mem.shape[1]
    depth = w_in.shape[0]
    assert depth == 1, "the final rmsnorm is fused into the last layer's ConvFFN kernel"
    assert seq % ROW_TILE == 0 and seq % (COUNT_STEP * KEY_CHUNK) == 0 and seq >= 4 * TOPK_MAX
    m = batch * seq
    xf = x.reshape(m, d_model)
    memf = mem.reshape(batch * mem_len, d_model)
    bias_tiles = _bias_tiles(rel_bias)

    splits = (ATTN_WIDTH, KV_WIDTH, KV_WIDTH, IDX_HEADS * IDX_DIM, IDX_DIM, IDX_HEADS, SSM_WIDTH, d_model, d_model)
    offs = [int(o) for o in np.cumsum((0,) + splits)]
    for l in range(depth):
        w = w_in[l]
        wq, wk, wv, wqi, wki, wwi, wss, wga, wgb = [w[:, offs[n]:offs[n + 1]] for n in range(len(splits))]
        pad = jnp.zeros((d_model, V7X_LANES - IDX_DIM - IDX_HEADS), w.dtype)
        w_t = jnp.concatenate([wq, wqi, wv, wwi], axis=1).T.astype(BF16)
        w_n = jnp.concatenate([wk, wki, wwi, pad, wss], axis=1).astype(BF16)
        w_gate = jnp.concatenate([wga, wgb], axis=1).astype(BF16)

        q_t, v_t, wi_t, k_n, us = _proj_attn(xf, norm_mix[l], w_t, w_n)
        gates = _norm_matmul(xf, norm_mix[l], w_gate, BF16, GATE_COL_TILE, "proj_gates", sigmoid=True)

        y_a = _sparse_attention(bias_tiles, q_t, v_t, wi_t, k_n, batch, seq)

        bre_bd = _block_diag(jnp.transpose(ssm_b_re[l], (0, 2, 1)))
        bim_bd = _block_diag(jnp.transpose(ssm_b_im[l], (0, 2, 1)))
        cre_bd = _block_diag(jnp.transpose(ssm_c_re[l], (0, 2, 1))).astype(BF16)
        cim_bd = _block_diag(jnp.transpose(ssm_c_im[l], (0, 2, 1))).astype(BF16)
        log_dt = jnp.broadcast_to(ssm_log_dt[l][:, None], (SSM_GROUPS, SSM_STATE)).reshape(1, SSM_STATES)
        bu_re, bu_im, a_re, a_im = _s5_input(us, ssm_lambda_re[l].reshape(1, SSM_STATES),
                                             ssm_lambda_im[l].reshape(1, SSM_STATES), log_dt, bre_bd, bim_bd)
        state_rows = SSM_STATES // V7X_LANES
        sshape = (batch, seq, state_rows, V7X_LANES)
        xs_re, xs_im = _s5_scan(a_re.reshape(state_rows, V7X_LANES), a_im.reshape(state_rows, V7X_LANES),
                                bu_re.reshape(sshape), bu_im.reshape(sshape))
        y_b = _s5_output(xs_re.reshape(m, SSM_STATES), xs_im.reshape(m, SSM_STATES), us, cre_bd, cim_bd,
                         ssm_d[l].reshape(1, SSM_WIDTH), ssm_w_glu[l].astype(BF16),
                         ssm_b_glu[l].reshape(1, SSM_WIDTH))

        kvx = _norm_matmul(memf, norm_mem[l], w_cross_kv[l].astype(BF16), BF16, COL_TILE, "cross_kv")
        xf = _post_mixer(xf, y_a, y_b, gates, w_branch_a[l].astype(BF16), w_branch_b[l].astype(BF16),
                         w_out[l].astype(BF16), norm_cross[l], w_cross_q[l].astype(BF16), kvx,
                         w_cross_o[l].astype(BF16), batch, seq, mem_len)

        xf = _conv_ffn(xf, norm_ffn[l], w_up[l].astype(BF16), ffn_conv_w[l], ffn_conv_b[l],
                       w_down[l].astype(BF16), norm_final, seq)
    return xf.reshape(batch, seq, d_model)
```

```python
import functools
import math

import jax
import jax.numpy as jnp
import numpy as np
from jax import lax
from jax.experimental import pallas as pl
from jax.experimental.pallas import tpu as pltpu

N_HEADS_A = 8
N_KV_A = 2
HEAD_DIM_A = 128
ATTN_WIDTH = N_HEADS_A * HEAD_DIM_A
KV_WIDTH = N_KV_A * HEAD_DIM_A
IDX_HEADS = 16
IDX_DIM = 64
TOPK_MAX = 256
BLOCK_Q = 128
REL_BUCKETS = 32
REL_MAX_DIST = 128
SSM_GROUP = 16
SSM_GROUPS = 32
SSM_STATE = 64
SSM_WIDTH = SSM_GROUP * SSM_GROUPS
SSM_STATES = SSM_GROUPS * SSM_STATE
X_HEADS = 4
X_HEAD_DIM = 128
CONV_WIDTH = 3
EPS = 1e-6

V7X_LANES = 128
V7X_SUBLANES = 8
V7X_BF16_ROWS = 16
V7X_VMEM_BYTES = 64 * 1024 * 1024
VMEM_RESERVE_BYTES = 8 * 1024 * 1024
VMEM_LIMIT_BYTES = V7X_VMEM_BYTES - VMEM_RESERVE_BYTES

GRP = N_HEADS_A // N_KV_A
GRP_LANES = GRP * BLOCK_Q
V_ROWS = HEAD_DIM_A + V7X_BF16_ROWS
KEY_CHUNK = 2 * BLOCK_Q
N_BIAS_TILES = 4
MAX_BISECT_ITERS = 64
BISECT_FIRST = 12
BISECT_ROUND = 2
COUNT_ROWS = 64
COUNT_STEP = 4
LOG2_E = math.log2(math.e)
MAX_INIT = -1e30
ROW_TILE = 1024
SMALL_ROW_TILE = 512
COL_TILE = 512
GATE_COL_TILE = 2048
FFN_ROW_TILE = 512
FFN_ROW_SLABS = 2
SCAN_CHUNK = 128
SCAN_GROUP = 16
STAGE_PITCH = 20

F32 = jnp.float32
BF16 = jnp.bfloat16
_NT = (((1,), (1,)), ((), ()))


def _params(n_axes):
    return pltpu.CompilerParams(dimension_semantics=("arbitrary",) * n_axes,
                                vmem_limit_bytes=VMEM_LIMIT_BYTES)


def _rms(x, gain, eps):
    return x * lax.rsqrt(jnp.mean(x * x, axis=-1, keepdims=True) + eps) * gain


def _norm_matmul_kernel(x_ref, g_ref, w_ref, o_ref, xn_ref, *, sigmoid):
    @pl.when(pl.program_id(1) == 0)
    def _():
        xn_ref[...] = _rms(x_ref[...], g_ref[...], EPS).astype(BF16)

    tm = o_ref.shape[0]
    rows = min(tm, SMALL_ROW_TILE) if sigmoid else tm
    for r in range(0, tm, rows):
        y = jnp.dot(xn_ref[r:r + rows], w_ref[...], preferred_element_type=F32)
        if sigmoid:
            y = 0.5 * jnp.tanh(0.5 * y) + 0.5
        o_ref[r:r + rows] = y.astype(o_ref.dtype)


def _norm_matmul(x, gain, w, out_dtype, tn, name, sigmoid=False):
    m, k = x.shape
    n = w.shape[1]
    tm = min(ROW_TILE, m)
    return pl.pallas_call(
        functools.partial(_norm_matmul_kernel, sigmoid=sigmoid),
        grid=(m // tm, n // tn),
        in_specs=[pl.BlockSpec((tm, k), lambda i, j: (i, 0)),
                  pl.BlockSpec((1, k), lambda i, j: (0, 0)),
                  pl.BlockSpec((k, tn), lambda i, j: (0, j))],
        out_specs=pl.BlockSpec((tm, tn), lambda i, j: (i, j)),
        out_shape=jax.ShapeDtypeStruct((m, n), out_dtype),
        scratch_shapes=[pltpu.VMEM((tm, k), BF16)],
        compiler_params=_params(2),
        name=name,
    )(x, gain.reshape(1, k), w)


def _proj_attn_kernel(x_ref, g_ref, wt_ref, wn_ref, qt_ref, vt_ref, wit_ref, kn_ref, us_ref):
    xn = _rms(x_ref[...], g_ref[...], EPS).astype(BF16)
    res = lax.dot_general(wt_ref[...], xn, _NT, preferred_element_type=F32)
    nq = qt_ref.shape[0]
    qt_ref[0:ATTN_WIDTH] = (res[0:ATTN_WIDTH] * (HEAD_DIM_A ** -0.5 * LOG2_E)).astype(BF16)
    qt_ref[ATTN_WIDTH:nq] = res[ATTN_WIDTH:nq].astype(BF16)
    pad_rows = V_ROWS - HEAD_DIM_A
    ones_row = (lax.broadcasted_iota(jnp.int32, (pad_rows, KEY_CHUNK), 0) == 0).astype(BF16)
    for c in range(vt_ref.shape[0]):
        for g in range(N_KV_A):
            rows = res[nq + g * HEAD_DIM_A:nq + (g + 1) * HEAD_DIM_A, c * KEY_CHUNK:(c + 1) * KEY_CHUNK]
            vt_ref[c, g * V_ROWS:g * V_ROWS + HEAD_DIM_A] = rows.astype(BF16)
            vt_ref[c, g * V_ROWS + HEAD_DIM_A:(g + 1) * V_ROWS] = ones_row
    wit_ref[...] = res[nq + KV_WIDTH:nq + KV_WIDTH + IDX_HEADS]
    plain = jnp.dot(xn, wn_ref[...], preferred_element_type=F32)
    n_kn = kn_ref.shape[1]
    kn_ref[...] = plain[:, 0:n_kn].astype(BF16)
    us_ref[...] = plain[:, n_kn:]


def _proj_attn(x, gain, w_t, w_n):
    m, k = x.shape
    tm = SMALL_ROW_TILE
    nq = ATTN_WIDTH + IDX_HEADS * IDX_DIM
    return pl.pallas_call(
        _proj_attn_kernel,
        grid=(m // tm,),
        in_specs=[pl.BlockSpec((tm, k), lambda i: (i, 0)),
                  pl.BlockSpec((1, k), lambda i: (0, 0)),
                  pl.BlockSpec(w_t.shape, lambda i: (0, 0)),
                  pl.BlockSpec(w_n.shape, lambda i: (0, 0))],
        out_specs=[pl.BlockSpec((nq, tm), lambda i: (0, i)),
                   pl.BlockSpec((tm // KEY_CHUNK, N_KV_A * V_ROWS, KEY_CHUNK), lambda i: (i, 0, 0)),
                   pl.BlockSpec((IDX_HEADS, tm), lambda i: (0, i)),
                   pl.BlockSpec((tm, w_n.shape[1] - SSM_WIDTH), lambda i: (i, 0)),
                   pl.BlockSpec((tm, SSM_WIDTH), lambda i: (i, 0))],
        out_shape=[jax.ShapeDtypeStruct((nq, m), BF16),
                   jax.ShapeDtypeStruct((m // KEY_CHUNK, N_KV_A * V_ROWS, KEY_CHUNK), BF16),
                   jax.ShapeDtypeStruct((IDX_HEADS, m), F32),
                   jax.ShapeDtypeStruct((m, w_n.shape[1] - SSM_WIDTH), BF16),
                   jax.ShapeDtypeStruct((m, SSM_WIDTH), F32)],
        compiler_params=_params(1),
        name="proj_attn",
    )(x, gain.reshape(1, k), w_t, w_n)


def _t5_bucket(n):
    max_exact = REL_BUCKETS // 2
    n = jnp.maximum(n, 0)
    nf = jnp.maximum(n, 1).astype(F32)
    large = max_exact + (jnp.log(nf / max_exact) / math.log(REL_MAX_DIST / max_exact)
                         * (REL_BUCKETS - max_exact)).astype(jnp.int32)
    large = jnp.minimum(large, REL_BUCKETS - 1)
    return jnp.where(n < max_exact, n, large)


def _bias_tiles_kernel(rb_ref, o_ref):
    shape = (KEY_CHUNK, BLOCK_Q)
    key = lax.broadcasted_iota(jnp.int32, shape, 0)
    qry = lax.broadcasted_iota(jnp.int32, shape, 1)
    for o in range(N_BIAS_TILES):
        bucket = _t5_bucket(o * BLOCK_Q + qry - key)
        for h in range(N_HEADS_A):
            def pick(b, t, bucket=bucket, h=h):
                return jnp.where(bucket == b, rb_ref[b, h], t)
            tile = lax.fori_loop(0, REL_BUCKETS, pick, jnp.zeros(shape, F32))
            j = h % GRP
            o_ref[o, h // GRP, :, j * BLOCK_Q:(j + 1) * BLOCK_Q] = (tile - rb_ref[REL_BUCKETS - 1, h]) * LOG2_E


def _bias_tiles(rel_bias):
    return pl.pallas_call(
        _bias_tiles_kernel,
        in_specs=[pl.BlockSpec(memory_space=pltpu.SMEM)],
        out_specs=pl.BlockSpec(memory_space=pltpu.VMEM),
        out_shape=jax.ShapeDtypeStruct((N_BIAS_TILES, N_KV_A, KEY_CHUNK, GRP_LANES), F32),
        name="rel_bias_tiles",
    )(rel_bias)


def _sparse_attn_kernel(bias_ref, qt_ref, qit_ref, wit_ref, kw_ref, k_ref, vt_ref, o_ref,
                        qz_ref, qiz_ref, sc_ref, prod_ref, m_ref, l_ref, acc_ref, tri_ref, ties_ref):
    i = pl.program_id(1)
    t0 = i * BLOCK_Q
    n_chunks = (i + 2) // 2
    tile = (KEY_CHUNK, BLOCK_Q)
    row1 = (1, BLOCK_Q)
    k_sel = float(TOPK_MAX)

    @pl.when((pl.program_id(0) == 0) & (i == 0))
    def _():
        qiz_ref[...] = jnp.zeros(qiz_ref.shape, BF16)

    for h in range(IDX_HEADS):
        qiz_ref[0:IDX_DIM, h * BLOCK_Q:(h + 1) * BLOCK_Q] = qit_ref[h * IDX_DIM:(h + 1) * IDX_DIM, :]
    for h in range(N_HEADS_A):
        qz_ref[:, h * BLOCK_Q:(h + 1) * BLOCK_Q] = qt_ref[h * HEAD_DIM_A:(h + 1) * HEAD_DIM_A, :]
    w_rows = wit_ref[...] * ((IDX_HEADS * IDX_DIM) ** -0.5)

    key_row = lax.broadcasted_iota(jnp.int32, tile, 0)
    q_pos = t0 + lax.broadcasted_iota(jnp.int32, tile, 1)

    last = n_chunks - 1
    n_pairs = (n_chunks + 1) // 2

    def idx_products(kc, slot):
        ks = pl.multiple_of(jnp.minimum(kc, last) * KEY_CHUNK, KEY_CHUNK)
        prod_ref[slot] = jnp.dot(kw_ref[pl.ds(ks, KEY_CHUNK), :], qiz_ref[...], preferred_element_type=F32)

    def reduce_scores(kc, slot, carry):
        rmax, rmin = carry
        kc = jnp.minimum(kc, last)
        s = jnp.zeros(tile, F32)
        for h in range(IDX_HEADS):
            d = prod_ref[slot, :, h * BLOCK_Q:(h + 1) * BLOCK_Q]
            s = s + jnp.maximum(d, 0.0) * w_rows[h:h + 1, :]
        causal = (kc * KEY_CHUNK + key_row) <= q_pos
        sc_ref[kc] = jnp.where(causal, s, -jnp.inf)
        rmax = jnp.maximum(rmax, jnp.max(jnp.where(causal, s, -jnp.inf), axis=0, keepdims=True))
        rmin = jnp.minimum(rmin, jnp.min(jnp.where(causal, s, jnp.inf), axis=0, keepdims=True))
        return rmax, rmin

    def run_pairs(start, stop, pair_body, carry):
        n_double = (stop - start) // 2

        def double(t, c):
            j = start + 2 * t
            return pair_body(j + 1, pair_body(j, c))

        carry = lax.fori_loop(0, n_double, double, carry)
        return lax.fori_loop(start + 2 * n_double, stop, pair_body, carry)

    def score_pair(j, carry, prefetch=True):
        idx_products(2 * j + 1, 1)
        carry = reduce_scores(2 * j, 0, carry)
        if prefetch:
            idx_products(2 * j + 2, 0)
        return reduce_scores(2 * j + 1, 1, carry)

    idx_products(0, 0)
    carry = lax.fori_loop(0, n_pairs - 1, score_pair,
                          (jnp.full(row1, -jnp.inf, F32), jnp.full(row1, jnp.inf, F32)))
    rmax, rmin = score_pair(n_pairs - 1, carry, prefetch=False)

    n_causal = (t0 + lax.broadcasted_iota(jnp.int32, row1, 1) + 1).astype(F32)
    short = n_causal <= k_sel

    n_steps = (n_chunks + COUNT_STEP - 1) // COUNT_STEP
    for d in range(COUNT_STEP - 1):
        @pl.when(n_chunks + d < n_steps * COUNT_STEP)
        def _(d=d):
            sc_ref[n_chunks + d] = jnp.full(tile, -jnp.inf, F32)

    def count_ge(t):
        def body(j, c):
            for d in range(COUNT_STEP):
                hit = jnp.where(sc_ref[COUNT_STEP * j + d] >= t, 1.0, 0.0)
                c = c + jnp.sum(hit.reshape(KEY_CHUNK // COUNT_ROWS, COUNT_ROWS, BLOCK_Q), axis=0)
            return c
        c = lax.fori_loop(0, n_steps, body, jnp.zeros((COUNT_ROWS, BLOCK_Q), F32))
        return jnp.sum(c, axis=0, keepdims=True)

    def halve(_, carry):
        lo, hi, c_lo, _ = carry
        mid = lo + (hi - lo) * 0.5
        cnt = count_ge(mid)
        enough = cnt >= k_sel
        shrinks = jnp.where((mid > lo) & (mid < hi), 1.0, 0.0)
        return jnp.where(enough, mid, lo), jnp.where(enough, hi, mid), jnp.where(enough, cnt, c_lo), shrinks

    def halvings(count, lo, hi, c_lo):
        lo, hi, c_lo, shrinks = lax.fori_loop(0, count, halve, (lo, hi, c_lo, jnp.ones(row1, F32)))
        settled = short | (c_lo == k_sel) | (shrinks == 0.0)
        return jnp.sum(jnp.where(settled, 0.0, 1.0)), lo, hi, c_lo

    def pending(carry):
        done, todo, _, _, _ = carry
        return (done < MAX_BISECT_ITERS) & (todo > 0.0)

    def bisect_round(carry):
        done, _, lo, hi, c_lo = carry
        return (done + BISECT_ROUND,) + halvings(BISECT_ROUND, lo, hi, c_lo)

    hi0 = rmax + jnp.maximum(jnp.abs(rmax) * 2.0 ** -22, jnp.finfo(F32).tiny)
    first = (jnp.int32(BISECT_FIRST),) + halvings(BISECT_FIRST, rmin, hi0, n_causal)
    _, _, lo, hi, c_lo = lax.while_loop(pending, bisect_round, first)
    thr = jnp.where(short, jnp.finfo(F32).min, lo)
    tied = jnp.logical_not(short) & (c_lo > k_sel)
    n_tied = jnp.sum(jnp.where(tied, 1.0, 0.0))

    m_ref[...] = jnp.full(m_ref.shape, MAX_INIT, F32)
    l_ref[...] = jnp.zeros(l_ref.shape, F32)
    acc_ref[...] = jnp.zeros(acc_ref.shape, F32)

    def qk_products(kc, slot):
        ks = pl.multiple_of(jnp.minimum(kc, last) * KEY_CHUNK, KEY_CHUNK)
        for g in range(N_KV_A):
            k_c = k_ref[pl.ds(ks, KEY_CHUNK), g * HEAD_DIM_A:(g + 1) * HEAD_DIM_A]
            prod_ref[slot, :, g * GRP_LANES:(g + 1) * GRP_LANES] = jnp.dot(
                k_c, qz_ref[:, g * GRP_LANES:(g + 1) * GRP_LANES], preferred_element_type=F32)

    def softmax_pv(kc, slot, near, ties, upper=None, quota=None):
        live = (kc <= last) if near else True
        kc = jnp.minimum(kc, last) if near else kc
        sc = sc_ref[kc]
        if ties:
            above = sc >= upper
            tie = (sc >= thr) & jnp.logical_not(above)
            tie_f = jnp.where(tie, 1.0, 0.0)
            upto = jnp.dot(tri_ref[...], tie_f.astype(BF16), preferred_element_type=F32)
            before = ties_ref[...]
            keep = (above | (tie & (before + upto - tie_f < quota))) & live
            ties_ref[...] = before + jnp.where(live, upto[KEY_CHUNK - 1:KEY_CHUNK], 0.0)
        else:
            keep = (sc >= thr) & live
        mask = jnp.where(keep, 0.0, -jnp.inf)
        mask = jnp.concatenate([mask] * GRP, axis=1)
        for g in range(N_KV_A):
            lg = prod_ref[slot, :, g * GRP_LANES:(g + 1) * GRP_LANES]
            if near:
                lg = lg + bias_ref[jnp.minimum(i - 2 * kc, N_BIAS_TILES - 1), g]
            lg = lg + mask
            m_old = m_ref[g]
            m_new = jnp.maximum(m_old, jnp.max(lg, axis=0, keepdims=True))
            alpha = jnp.exp2(m_old - m_new)
            p = jnp.exp2(lg - m_new)
            pv = jnp.dot(vt_ref[kc, g * V_ROWS:(g + 1) * V_ROWS, :], p.astype(BF16),
                         preferred_element_type=F32)
            l_ref[g] = alpha * l_ref[g] + pv[HEAD_DIM_A:HEAD_DIM_A + 1]
            acc_ref[g] = alpha * acc_ref[g] + pv[0:HEAD_DIM_A]
            m_ref[g] = m_new

    def attend(near, ties, prefetch=True, **tie_args):
        def body(j, carry):
            qk_products(2 * j + 1, 1)
            softmax_pv(2 * j, 0, near, ties, **tie_args)
            if prefetch:
                qk_products(2 * j + 2, 0)
            softmax_pv(2 * j + 1, 1, near, ties, **tie_args)
            return carry
        return body

    def attend_all(ties, **tie_args):
        run_pairs(0, n_far, attend(False, ties, **tie_args), 0)
        lax.fori_loop(n_far, n_pairs - 1, attend(True, ties, **tie_args), 0)
        attend(True, ties, prefetch=False, **tie_args)(n_pairs - 1, 0)

    n_far = jnp.maximum(n_pairs - 2, 0)
    qk_products(0, 0)

    @pl.when(n_tied == 0.0)
    def _():
        attend_all(False)

    @pl.when(n_tied > 0.0)
    def _():
        row = lax.broadcasted_iota(jnp.int32, (KEY_CHUNK, KEY_CHUNK), 0)
        col = lax.broadcasted_iota(jnp.int32, (KEY_CHUNK, KEY_CHUNK), 1)
        tri_ref[...] = (col <= row).astype(BF16)
        ties_ref[...] = jnp.zeros(ties_ref.shape, F32)
        attend_all(True, upper=jnp.where(tied, hi, thr), quota=jnp.where(tied, k_sel - count_ge(hi), 0.0))

    for g in range(N_KV_A):
        out_t = acc_ref[g] / l_ref[g]
        for j in range(GRP):
            h = g * GRP + j
            o_ref[:, h * HEAD_DIM_A:(h + 1) * HEAD_DIM_A] = (
                out_t[:, j * BLOCK_Q:(j + 1) * BLOCK_Q].T.astype(o_ref.dtype))


def _sparse_attention(bias_tiles, q_t, v_t, wi_t, k_n, batch, seq):
    nb = seq // BLOCK_Q
    n_seq_chunks = seq // KEY_CHUNK
    kw_blk = KV_WIDTH // V7X_LANES
    return pl.pallas_call(
        _sparse_attn_kernel,
        grid=(batch, nb),
        in_specs=[pl.BlockSpec(bias_tiles.shape, lambda b, i: (0, 0, 0, 0)),
                  pl.BlockSpec((ATTN_WIDTH, BLOCK_Q), lambda b, i: (0, b * nb + i)),
                  pl.BlockSpec((IDX_HEADS * IDX_DIM, BLOCK_Q), lambda b, i: (1, b * nb + i)),
                  pl.BlockSpec((IDX_HEADS, BLOCK_Q), lambda b, i: (0, b * nb + i)),
                  pl.BlockSpec((seq, V7X_LANES), lambda b, i: (b, kw_blk)),
                  pl.BlockSpec((seq, KV_WIDTH), lambda b, i: (b, 0)),
                  pl.BlockSpec((n_seq_chunks, N_KV_A * V_ROWS, KEY_CHUNK), lambda b, i: (b, 0, 0))],
        out_specs=pl.BlockSpec((BLOCK_Q, ATTN_WIDTH), lambda b, i: (b * nb + i, 0)),
        out_shape=jax.ShapeDtypeStruct((batch * seq, ATTN_WIDTH), BF16),
        scratch_shapes=[pltpu.VMEM((HEAD_DIM_A, N_HEADS_A * BLOCK_Q), BF16),
                        pltpu.VMEM((V7X_LANES, IDX_HEADS * BLOCK_Q), BF16),
                        pltpu.VMEM((n_seq_chunks, KEY_CHUNK, BLOCK_Q), F32),
                        pltpu.VMEM((2, KEY_CHUNK, IDX_HEADS * BLOCK_Q), F32),
                        pltpu.VMEM((N_KV_A, 1, GRP_LANES), F32),
                        pltpu.VMEM((N_KV_A, 1, GRP_LANES), F32),
                        pltpu.VMEM((N_KV_A, HEAD_DIM_A, GRP_LANES), F32),
                        pltpu.VMEM((KEY_CHUNK, KEY_CHUNK), BF16),
                        pltpu.VMEM((1, BLOCK_Q), F32)],
        compiler_params=_params(2),
        name="sparse_attention",
    )(bias_tiles, q_t, q_t, wi_t, k_n, k_n, v_t)


def _s5_input_kernel(u_ref, lr_ref, li_ref, ldt_ref, bre_ref, bim_ref,
                     bur_ref, bui_ref, ar_ref, ai_ref, bbr_ref, bbi_ref):
    @pl.when(pl.program_id(0) == 0)
    def _():
        dt = jnp.exp(ldt_ref[...])
        lr = lr_ref[...]
        li = li_ref[...]
        mag = jnp.exp(lr * dt)
        ar = mag * jnp.cos(li * dt)
        ai = mag * jnp.sin(li * dt)
        den = lr * lr + li * li
        nr = ar - 1.0
        cr = (nr * lr + ai * li) / den
        ci = (ai * lr - nr * li) / den
        bre = bre_ref[...]
        bim = bim_ref[...]
        bbr_ref[...] = (cr * bre - ci * bim).astype(BF16)
        bbi_ref[...] = (cr * bim + ci * bre).astype(BF16)
        ar_ref[...] = ar
        ai_ref[...] = ai

    u = u_ref[...].astype(BF16)
    bur_ref[...] = jnp.dot(u, bbr_ref[...], preferred_element_type=F32)
    bui_ref[...] = jnp.dot(u, bbi_ref[...], preferred_element_type=F32)


def _s5_input(us, lam_re, lam_im, log_dt, bre_bd, bim_bd):
    m = us.shape[0]
    tm = SMALL_ROW_TILE
    row = pl.BlockSpec((1, SSM_STATES), lambda i: (0, 0))
    mat = pl.BlockSpec((SSM_WIDTH, SSM_STATES), lambda i: (0, 0))
    out = pl.BlockSpec((tm, SSM_STATES), lambda i: (i, 0))
    return pl.pallas_call(
        _s5_input_kernel,
        grid=(m // tm,),
        in_specs=[pl.BlockSpec((tm, SSM_WIDTH), lambda i: (i, 0)), row, row, row, mat, mat],
        out_specs=[out, out, row, row],
        out_shape=[jax.ShapeDtypeStruct((m, SSM_STATES), F32)] * 2
        + [jax.ShapeDtypeStruct((1, SSM_STATES), F32)] * 2,
        scratch_shapes=[pltpu.VMEM((SSM_WIDTH, SSM_STATES), BF16)] * 2,
        compiler_params=_params(1),
        name="s5_input",
    )(us, lam_re, lam_im, log_dt, bre_bd, bim_bd)


def _s5_scan_kernel(ar_ref, ai_ref, br_ref, bi_ref, xr_ref, xi_ref, sr_ref, si_ref, stage_ref):
    @pl.when(pl.program_id(0) == 0)
    def _():
        sr_ref[...] = jnp.zeros(sr_ref.shape, F32)
        si_ref[...] = jnp.zeros(si_ref.shape, F32)

    ar = ar_ref[...]
    ai = ai_ref[...]
    batch, steps, rows, lanes = br_ref.shape

    def group(gi, carry):
        xr, xi = carry
        t0 = pl.multiple_of(gi * SCAN_GROUP, SCAN_GROUP)
        for s in range(SCAN_GROUP):
            nr = ar * xr - ai * xi + br_ref[:, t0 + s]
            ni = ar * xi + ai * xr + bi_ref[:, t0 + s]
            stage_ref[0, :, s * STAGE_PITCH:s * STAGE_PITCH + rows] = nr
            stage_ref[1, :, s * STAGE_PITCH:s * STAGE_PITCH + rows] = ni
            xr, xi = nr, ni
        for part, out_ref in ((0, xr_ref), (1, xi_ref)):
            for b in range(batch):
                for j in range(rows):
                    tile = stage_ref[part, b, pl.ds(j, SCAN_GROUP, stride=STAGE_PITCH), :]
                    out_ref[b, pl.ds(t0, SCAN_GROUP), j * lanes:(j + 1) * lanes] = tile.astype(out_ref.dtype)
        return xr, xi

    xr, xi = lax.fori_loop(0, steps // SCAN_GROUP, group, (sr_ref[...], si_ref[...]))
    sr_ref[...] = xr
    si_ref[...] = xi


def _s5_scan(a_re, a_im, bu_re, bu_im):
    batch, seq, rows, lanes = bu_re.shape
    blk = pl.BlockSpec((batch, SCAN_CHUNK, rows, lanes), lambda c: (0, c, 0, 0))
    out = pl.BlockSpec((batch, SCAN_CHUNK, rows * lanes), lambda c: (0, c, 0))
    par = pl.BlockSpec((rows, lanes), lambda c: (0, 0))
    return pl.pallas_call(
        _s5_scan_kernel,
        grid=(seq // SCAN_CHUNK,),
        in_specs=[par, par, blk, blk],
        out_specs=[out, out],
        out_shape=[jax.ShapeDtypeStruct((batch, seq, rows * lanes), BF16)] * 2,
        scratch_shapes=[pltpu.VMEM((batch, rows, lanes), F32)] * 2
        + [pltpu.VMEM((2, batch, SCAN_GROUP * STAGE_PITCH, lanes), F32)],
        compiler_params=_params(1),
        name="s5_scan",
    )(a_re, a_im, bu_re, bu_im)


def _s5_output_kernel(xr_ref, xi_ref, u_ref, cr_ref, ci_ref, d_ref, wg_ref, bg_ref, o_ref):
    y = (jnp.dot(xr_ref[...], cr_ref[...], preferred_element_type=F32)
         - jnp.dot(xi_ref[...], ci_ref[...], preferred_element_type=F32)
         + d_ref[...] * u_ref[...])
    g = jax.nn.gelu(y)
    gate = jax.nn.sigmoid(jnp.dot(g.astype(BF16), wg_ref[...], preferred_element_type=F32) + bg_ref[...])
    o_ref[...] = (g * gate).astype(o_ref.dtype)


def _s5_output(xr, xi, us, cre_bd, cim_bd, d_row, w_glu, b_glu):
    m = us.shape[0]
    tm = SMALL_ROW_TILE
    state = pl.BlockSpec((tm, SSM_STATES), lambda i: (i, 0))
    chan = pl.BlockSpec((tm, SSM_WIDTH), lambda i: (i, 0))
    cmat = pl.BlockSpec((SSM_STATES, SSM_WIDTH), lambda i: (0, 0))
    row = pl.BlockSpec((1, SSM_WIDTH), lambda i: (0, 0))
    return pl.pallas_call(
        _s5_output_kernel,
        grid=(m // tm,),
        in_specs=[state, state, chan, cmat, cmat, row,
                  pl.BlockSpec((SSM_WIDTH, SSM_WIDTH), lambda i: (0, 0)), row],
        out_specs=chan,
        out_shape=jax.ShapeDtypeStruct((m, SSM_WIDTH), BF16),
        compiler_params=_params(1),
        name="s5_output",
    )(xr, xi, us, cre_bd, cim_bd, d_row, w_glu, b_glu)


def _block_diag(blocks):
    g, r, c = blocks.shape
    eye = jnp.eye(g, dtype=blocks.dtype)
    return (eye[:, None, :, None] * blocks[:, :, None, :]).reshape(g * r, g * c)


def _post_mixer_kernel(x_ref, ya_ref, yb_ref, ga_ref, gb_ref, wa_ref, wb_ref, wo_ref, gc_ref, wq_ref,
                       kv_ref, wco_ref, o_ref):
    width = X_HEADS * X_HEAD_DIM
    scale = X_HEAD_DIM ** -0.5
    za = jnp.dot(ya_ref[...], wa_ref[...], preferred_element_type=F32)
    zb = jnp.dot(yb_ref[...], wb_ref[...], preferred_element_type=F32)
    z = (ga_ref[...].astype(F32) * za + gb_ref[...].astype(F32) * zb).astype(BF16)
    x1 = x_ref[...] + jnp.dot(z, wo_ref[...], preferred_element_type=F32)
    xn = _rms(x1, gc_ref[...], EPS).astype(BF16)
    q = jnp.dot(xn, wq_ref[...], preferred_element_type=F32).astype(BF16)
    heads = []
    for h in range(X_HEADS):
        cols = slice(h * X_HEAD_DIM, (h + 1) * X_HEAD_DIM)
        vcols = slice(width + h * X_HEAD_DIM, width + (h + 1) * X_HEAD_DIM)
        lg = lax.dot_general(q[:, cols], kv_ref[:, cols], _NT, preferred_element_type=F32) * scale
        p = jnp.exp(lg - jnp.max(lg, axis=-1, keepdims=True))
        o = jnp.dot(p.astype(BF16), kv_ref[:, vcols], preferred_element_type=F32)
        heads.append((o / jnp.sum(p, axis=-1, keepdims=True)).astype(BF16))
    attn = jnp.concatenate(heads, axis=1)
    o_ref[...] = x1 + jnp.dot(attn, wco_ref[...], preferred_element_type=F32)


def _post_mixer(x, ya, yb, gates, w_a, w_b, w_out, gain_cross, w_q, kv, w_co, batch, seq, mem_len):
    d = x.shape[1]
    width = X_HEADS * X_HEAD_DIM
    tm = SMALL_ROW_TILE
    nt = seq // tm
    row = pl.BlockSpec((tm, d), lambda b, i: (b * nt + i, 0))
    return pl.pallas_call(
        _post_mixer_kernel,
        grid=(batch, nt),
        in_specs=[row,
                  pl.BlockSpec((tm, ya.shape[1]), lambda b, i: (b * nt + i, 0)),
                  pl.BlockSpec((tm, yb.shape[1]), lambda b, i: (b * nt + i, 0)),
                  row,
                  pl.BlockSpec((tm, d), lambda b, i: (b * nt + i, 1)),
                  pl.BlockSpec(w_a.shape, lambda b, i: (0, 0)),
                  pl.BlockSpec(w_b.shape, lambda b, i: (0, 0)),
                  pl.BlockSpec((d, d), lambda b, i: (0, 0)),
                  pl.BlockSpec((1, d), lambda b, i: (0, 0)),
                  pl.BlockSpec((d, width), lambda b, i: (0, 0)),
                  pl.BlockSpec((mem_len, 2 * width), lambda b, i: (b, 0)),
                  pl.BlockSpec((width, d), lambda b, i: (0, 0))],
        out_specs=row,
        out_shape=jax.ShapeDtypeStruct(x.shape, F32),
        compiler_params=_params(2),
        name="post_mixer",
    )(x, ya, yb, gates, gates, w_a, w_b, w_out, gain_cross.reshape(1, d), w_q, kv, w_co)


def _conv_ffn_kernel(x_ref, g_ref, wa_ref, wb_ref, cwa_ref, cwb_ref, cba_ref, cbb_ref, wd_ref, gf_ref,
                     o_ref, xn_ref, ha_ref, hb_ref, ta_ref, tb_ref, *, tiles_per_seq):
    i = pl.program_id(0)
    j = pl.program_id(1)
    tm = x_ref.shape[0]
    halo = V7X_SUBLANES

    @pl.when(j == 0)
    def _():
        xn_ref[...] = _rms(x_ref[...], g_ref[...], EPS).astype(BF16)
        o_ref[...] = jnp.zeros(o_ref.shape, F32)

    @pl.when(i % tiles_per_seq == 0)
    def _():
        ha_ref[0:halo] = jnp.zeros((halo, ha_ref.shape[1]), F32)
        hb_ref[0:halo] = jnp.zeros((halo, hb_ref.shape[1]), F32)

    @pl.when(i % tiles_per_seq != 0)
    def _():
        ha_ref[0:halo] = ta_ref[j]
        hb_ref[0:halo] = tb_ref[j]

    rows = tm // FFN_ROW_SLABS
    for s in range(FFN_ROW_SLABS):
        xn = xn_ref[s * rows:(s + 1) * rows]
        ha_ref[halo + s * rows:halo + (s + 1) * rows] = jnp.dot(xn, wa_ref[...], preferred_element_type=F32)
        hb_ref[halo + s * rows:halo + (s + 1) * rows] = jnp.dot(xn, wb_ref[...], preferred_element_type=F32)
    ta_ref[j] = ha_ref[tm:tm + halo]
    tb_ref[j] = hb_ref[tm:tm + halo]

    def conv(h_ref, cw_ref, cb_ref, r0):
        out = cb_ref[...]
        for tap in range(CONV_WIDTH):
            start = r0 + halo - (CONV_WIDTH - 1) + tap
            out = out + h_ref[start:start + rows] * cw_ref[tap:tap + 1]
        return out

    for s in range(FFN_ROW_SLABS):
        a = conv(ha_ref, cwa_ref, cba_ref, s * rows)
        b = conv(hb_ref, cwb_ref, cbb_ref, s * rows)
        act = (a * jax.nn.sigmoid(a) * b).astype(BF16)
        o_ref[s * rows:(s + 1) * rows] += jnp.dot(act, wd_ref[...], preferred_element_type=F32)

    @pl.when(j == pl.num_programs(1) - 1)
    def _():
        o_ref[...] = _rms(x_ref[...] + o_ref[...], gf_ref[...], EPS)


def _conv_ffn(x, gain, w_up, conv_w, conv_b, w_down, gain_final, seq):
    m, d = x.shape
    d_ff = w_down.shape[0]
    tm, tf = FFN_ROW_TILE, COL_TILE
    nj = d_ff // tf
    halo = V7X_SUBLANES
    return pl.pallas_call(
        functools.partial(_conv_ffn_kernel, tiles_per_seq=seq // tm),
        grid=(m // tm, nj),
        in_specs=[pl.BlockSpec((tm, d), lambda i, j: (i, 0)),
                  pl.BlockSpec((1, d), lambda i, j: (0, 0)),
                  pl.BlockSpec((d, tf), lambda i, j: (0, j)),
                  pl.BlockSpec((d, tf), lambda i, j: (0, j + nj)),
                  pl.BlockSpec((CONV_WIDTH, tf), lambda i, j: (0, j)),
                  pl.BlockSpec((CONV_WIDTH, tf), lambda i, j: (0, j + nj)),
                  pl.BlockSpec((1, tf), lambda i, j: (0, j)),
                  pl.BlockSpec((1, tf), lambda i, j: (0, j + nj)),
                  pl.BlockSpec((tf, d), lambda i, j: (j, 0)),
                  pl.BlockSpec((1, d), lambda i, j: (0, 0))],
        out_specs=pl.BlockSpec((tm, d), lambda i, j: (i, 0)),
        out_shape=jax.ShapeDtypeStruct((m, d), F32),
        scratch_shapes=[pltpu.VMEM((tm, d), BF16),
                        pltpu.VMEM((tm + halo, tf), F32),
                        pltpu.VMEM((tm + halo, tf), F32),
                        pltpu.VMEM((nj, halo, tf), F32),
                        pltpu.VMEM((nj, halo, tf), F32)],
        compiler_params=_params(2),
        name="conv_ffn",
    )(x, gain.reshape(1, d), w_up, w_up, conv_w, conv_w, conv_b.reshape(1, -1), conv_b.reshape(1, -1),
      w_down, gain_final.reshape(1, d))


def kernel(x, mem, rel_bias, norm_mix, w_in, ssm_lambda_re, ssm_lambda_im, ssm_log_dt, ssm_b_re, ssm_b_im, ssm_c_re, ssm_c_im, ssm_d, ssm_w_glu, ssm_b_glu, w_branch_a, w_branch_b, w_out, norm_cross, norm_mem, w_cross_q, w_cross_kv, w_cross_o, norm_ffn, w_up, ffn_conv_w, ffn_conv_b, w_down, norm_final):
    batch, seq, d_model = x.shape
    mem_len = mem.shape[1]
    depth = w_in.shape[0]
    assert depth == 1, "the final rmsnorm is fused into the last layer's ConvFFN kernel"
    assert seq % ROW_TILE == 0 and seq % (COUNT_STEP * KEY_CHUNK) == 0 and seq >= 4 * TOPK_MAX
    m = batch * seq
    xf = x.reshape(m, d_model)
    memf = mem.reshape(batch * mem_len, d_model)
    bias_tiles = _bias_tiles(rel_bias)

    splits = (ATTN_WIDTH, KV_WIDTH, KV_WIDTH, IDX_HEADS * IDX_DIM, IDX_DIM, IDX_HEADS, SSM_WIDTH, d_model, d_model)
    offs = [int(o) for o in np.cumsum((0,) + splits)]
    for l in range(depth):
        w = w_in[l]
        wq, wk, wv, wqi, wki, wwi, wss, wga, wgb = [w[:, offs[n]:offs[n + 1]] for n in range(len(splits))]
        pad = jnp.zeros((d_model, V7X_LANES - IDX_DIM - IDX_HEADS), w.dtype)
        w_t = jnp.concatenate([wq, wqi, wv, wwi], axis=1).T.astype(BF16)
        w_n = jnp.concatenate([wk, wki, wwi, pad, wss], axis=1).astype(BF16)
        w_gate = jnp.concatenate([wga, wgb], axis=1).astype(BF16)

        q_t, v_t, wi_t, k_n, us = _proj_attn(xf, norm_mix[l], w_t, w_n)
        gates = _norm_matmul(xf, norm_mix[l], w_gate, BF16, GATE_COL_TILE, "proj_gates", sigmoid=True)

        y_a = _sparse_attention(bias_tiles, q_t, v_t, wi_t, k_n, batch, seq)

        bre_bd = _block_diag(jnp.transpose(ssm_b_re[l], (0, 2, 1)))
        bim_bd = _block_diag(jnp.transpose(ssm_b_im[l], (0, 2, 1)))
        cre_bd = _block_diag(jnp.transpose(ssm_c_re[l], (0, 2, 1))).astype(BF16)
        cim_bd = _block_diag(jnp.transpose(ssm_c_im[l], (0, 2, 1))).astype(BF16)
        log_dt = jnp.broadcast_to(ssm_log_dt[l][:, None], (SSM_GROUPS, SSM_STATE)).reshape(1, SSM_STATES)
        bu_re, bu_im, a_re, a_im = _s5_input(us, ssm_lambda_re[l].reshape(1, SSM_STATES),
                                             ssm_lambda_im[l].reshape(1, SSM_STATES), log_dt, bre_bd, bim_bd)
        state_rows = SSM_STATES // V7X_LANES
        sshape = (batch, seq, state_rows, V7X_LANES)
        xs_re, xs_im = _s5_scan(a_re.reshape(state_rows, V7X_LANES), a_im.reshape(state_rows, V7X_LANES),
                                bu_re.reshape(sshape), bu_im.reshape(sshape))
        y_b = _s5_output(xs_re.reshape(m, SSM_STATES), xs_im.reshape(m, SSM_STATES), us, cre_bd, cim_bd,
                         ssm_d[l].reshape(1, SSM_WIDTH), ssm_w_glu[l].astype(BF16),
                         ssm_b_glu[l].reshape(1, SSM_WIDTH))

        kvx = _norm_matmul(memf, norm_mem[l], w_cross_kv[l].astype(BF16), BF16, COL_TILE, "cross_kv")
        xf = _post_mixer(xf, y_a, y_b, gates, w_branch_a[l].astype(BF16), w_branch_b[l].astype(BF16),
                         w_out[l].astype(BF16), norm_cross[l], w_cross_q[l].astype(BF16), kvx,
                         w_cross_o[l].astype(BF16), batch, seq, mem_len)

        xf = _conv_ffn(xf, norm_ffn[l], w_up[l].astype(BF16), ffn_conv_w[l], ffn_conv_b[l],
                       w_down[l].astype(BF16), norm_final, seq)
    return xf.reshape(batch, seq, d_model)
```

```python
import functools
import math

import jax
import jax.numpy as jnp
import numpy as np
from jax import lax
from jax.experimental import pallas as pl
from jax.experimental.pallas import tpu as pltpu

N_HEADS_A = 8
N_KV_A = 2
HEAD_DIM_A = 128
ATTN_WIDTH = N_HEADS_A * HEAD_DIM_A
KV_WIDTH = N_KV_A * HEAD_DIM_A
IDX_HEADS = 16
IDX_DIM = 64
TOPK_MAX = 256
BLOCK_Q = 128
REL_BUCKETS = 32
REL_MAX_DIST = 128
SSM_GROUP = 16
SSM_GROUPS = 32
SSM_STATE = 64
SSM_WIDTH = SSM_GROUP * SSM_GROUPS
SSM_STATES = SSM_GROUPS * SSM_STATE
X_HEADS = 4
X_HEAD_DIM = 128
CONV_WIDTH = 3
EPS = 1e-6

V7X_LANES = 128
V7X_SUBLANES = 8
V7X_BF16_ROWS = 16
V7X_VMEM_BYTES = 64 * 1024 * 1024
VMEM_RESERVE_BYTES = 8 * 1024 * 1024
VMEM_LIMIT_BYTES = V7X_VMEM_BYTES - VMEM_RESERVE_BYTES

GRP = N_HEADS_A // N_KV_A
GRP_LANES = GRP * BLOCK_Q
V_ROWS = HEAD_DIM_A + V7X_BF16_ROWS
KEY_CHUNK = 2 * BLOCK_Q
N_BIAS_TILES = 4
MAX_BISECT_ITERS = 64
BISECT_FIRST = 12
BISECT_ROUND = 2
COUNT_ROWS = 64
COUNT_STEP = 4
LOG2_E = math.log2(math.e)
MAX_INIT = -1e30
ROW_TILE = 1024
SMALL_ROW_TILE = 512
COL_TILE = 512
GATE_COL_TILE = 2048
FFN_ROW_TILE = 512
FFN_ROW_SLABS = 2
SCAN_CHUNK = 128
SCAN_GROUP = 16
STAGE_PITCH = 20

F32 = jnp.float32
BF16 = jnp.bfloat16
_NT = (((1,), (1,)), ((), ()))


def _params(n_axes):
    return pltpu.CompilerParams(dimension_semantics=("arbitrary",) * n_axes,
                                vmem_limit_bytes=VMEM_LIMIT_BYTES)


def _rms(x, gain, eps):
    return x * lax.rsqrt(jnp.mean(x * x, axis=-1, keepdims=True) + eps) * gain


def _norm_matmul_kernel(x_ref, g_ref, w_ref, o_ref, xn_ref, *, sigmoid):
    @pl.when(pl.program_id(1) == 0)
    def _():
        xn_ref[...] = _rms(x_ref[...], g_ref[...], EPS).astype(BF16)

    tm = o_ref.shape[0]
    rows = min(tm, SMALL_ROW_TILE) if sigmoid else tm
    for r in range(0, tm, rows):
        y = jnp.dot(xn_ref[r:r + rows], w_ref[...], preferred_element_type=F32)
        if sigmoid:
            y = 0.5 * jnp.tanh(0.5 * y) + 0.5
        o_ref[r:r + rows] = y.astype(o_ref.dtype)


def _norm_matmul(x, gain, w, out_dtype, tn, name, sigmoid=False):
    m, k = x.shape
    n = w.shape[1]
    tm = min(ROW_TILE, m)
    return pl.pallas_call(
        functools.partial(_norm_matmul_kernel, sigmoid=sigmoid),
        grid=(m // tm, n // tn),
        in_specs=[pl.BlockSpec((tm, k), lambda i, j: (i, 0)),
                  pl.BlockSpec((1, k), lambda i, j: (0, 0)),
                  pl.BlockSpec((k, tn), lambda i, j: (0, j))],
        out_specs=pl.BlockSpec((tm, tn), lambda i, j: (i, j)),
        out_shape=jax.ShapeDtypeStruct((m, n), out_dtype),
        scratch_shapes=[pltpu.VMEM((tm, k), BF16)],
        compiler_params=_params(2),
        name=name,
    )(x, gain.reshape(1, k), w)


def _proj_attn_kernel(x_ref, g_ref, wt_ref, wn_ref, qt_ref, vt_ref, wit_ref, kn_ref, us_ref):
    xn = _rms(x_ref[...], g_ref[...], EPS).astype(BF16)
    res = lax.dot_general(wt_ref[...], xn, _NT, preferred_element_type=F32)
    nq = qt_ref.shape[0]
    qt_ref[0:ATTN_WIDTH] = (res[0:ATTN_WIDTH] * (HEAD_DIM_A ** -0.5 * LOG2_E)).astype(BF16)
    qt_ref[ATTN_WIDTH:nq] = res[ATTN_WIDTH:nq].astype(BF16)
    pad_rows = V_ROWS - HEAD_DIM_A
    ones_row = (lax.broadcasted_iota(jnp.int32, (pad_rows, KEY_CHUNK), 0) == 0).astype(BF16)
    for c in range(vt_ref.shape[0]):
        for g in range(N_KV_A):
            rows = res[nq + g * HEAD_DIM_A:nq + (g + 1) * HEAD_DIM_A, c * KEY_CHUNK:(c + 1) * KEY_CHUNK]
            vt_ref[c, g * V_ROWS:g * V_ROWS + HEAD_DIM_A] = rows.astype(BF16)
            vt_ref[c, g * V_ROWS + HEAD_DIM_A:(g + 1) * V_ROWS] = ones_row
    wit_ref[...] = res[nq + KV_WIDTH:nq + KV_WIDTH + IDX_HEADS]
    plain = jnp.dot(xn, wn_ref[...], preferred_element_type=F32)
    n_kn = kn_ref.shape[1]
    kn_ref[...] = plain[:, 0:n_kn].astype(BF16)
    us_ref[...] = plain[:, n_kn:]


def _proj_attn(x, gain, w_t, w_n):
    m, k = x.shape
    tm = SMALL_ROW_TILE
    nq = ATTN_WIDTH + IDX_HEADS * IDX_DIM
    return pl.pallas_call(
        _proj_attn_kernel,
        grid=(m // tm,),
        in_specs=[pl.BlockSpec((tm, k), lambda i: (i, 0)),
                  pl.BlockSpec((1, k), lambda i: (0, 0)),
                  pl.BlockSpec(w_t.shape, lambda i: (0, 0)),
                  pl.BlockSpec(w_n.shape, lambda i: (0, 0))],
        out_specs=[pl.BlockSpec((nq, tm), lambda i: (0, i)),
                   pl.BlockSpec((tm // KEY_CHUNK, N_KV_A * V_ROWS, KEY_CHUNK), lambda i: (i, 0, 0)),
                   pl.BlockSpec((IDX_HEADS, tm), lambda i: (0, i)),
                   pl.BlockSpec((tm, w_n.shape[1] - SSM_WIDTH), lambda i: (i, 0)),
                   pl.BlockSpec((tm, SSM_WIDTH), lambda i: (i, 0))],
        out_shape=[jax.ShapeDtypeStruct((nq, m), BF16),
                   jax.ShapeDtypeStruct((m // KEY_CHUNK, N_KV_A * V_ROWS, KEY_CHUNK), BF16),
                   jax.ShapeDtypeStruct((IDX_HEADS, m), F32),
                   jax.ShapeDtypeStruct((m, w_n.shape[1] - SSM_WIDTH), BF16),
                   jax.ShapeDtypeStruct((m, SSM_WIDTH), F32)],
        compiler_params=_params(1),
        name="proj_attn",
    )(x, gain.reshape(1, k), w_t, w_n)


def _t5_bucket(n):
    max_exact = REL_BUCKETS // 2
    n = jnp.maximum(n, 0)
    nf = jnp.maximum(n, 1).astype(F32)
    large = max_exact + (jnp.log(nf / max_exact) / math.log(REL_MAX_DIST / max_exact)
                         * (REL_BUCKETS - max_exact)).astype(jnp.int32)
    large = jnp.minimum(large, REL_BUCKETS - 1)
    return jnp.where(n < max_exact, n, large)


def _bias_tiles_kernel(rb_ref, o_ref):
    shape = (KEY_CHUNK, BLOCK_Q)
    key = lax.broadcasted_iota(jnp.int32, shape, 0)
    qry = lax.broadcasted_iota(jnp.int32, shape, 1)
    for o in range(N_BIAS_TILES):
        bucket = _t5_bucket(o * BLOCK_Q + qry - key)
        for h in range(N_HEADS_A):
            def pick(b, t, bucket=bucket, h=h):
                return jnp.where(bucket == b, rb_ref[b, h], t)
            tile = lax.fori_loop(0, REL_BUCKETS, pick, jnp.zeros(shape, F32))
            j = h % GRP
            o_ref[o, h // GRP, :, j * BLOCK_Q:(j + 1) * BLOCK_Q] = (tile - rb_ref[REL_BUCKETS - 1, h]) * LOG2_E


def _bias_tiles(rel_bias):
    return pl.pallas_call(
        _bias_tiles_kernel,
        in_specs=[pl.BlockSpec(memory_space=pltpu.SMEM)],
        out_specs=pl.BlockSpec(memory_space=pltpu.VMEM),
        out_shape=jax.ShapeDtypeStruct((N_BIAS_TILES, N_KV_A, KEY_CHUNK, GRP_LANES), F32),
        name="rel_bias_tiles",
    )(rel_bias)


def _sparse_attn_kernel(bias_ref, qt_ref, qit_ref, wit_ref, kw_ref, k_ref, vt_ref, o_ref,
                        qz_ref, qiz_ref, sc_ref, prod_ref, m_ref, l_ref, acc_ref, tri_ref, ties_ref):
    i = pl.program_id(1)
    t0 = i * BLOCK_Q
    n_chunks = (i + 2) // 2
    tile = (KEY_CHUNK, BLOCK_Q)
    row1 = (1, BLOCK_Q)
    k_sel = float(TOPK_MAX)

    @pl.when((pl.program_id(0) == 0) & (i == 0))
    def _():
        qiz_ref[...] = jnp.zeros(qiz_ref.shape, BF16)

    for h in range(IDX_HEADS):
        qiz_ref[0:IDX_DIM, h * BLOCK_Q:(h + 1) * BLOCK_Q] = qit_ref[h * IDX_DIM:(h + 1) * IDX_DIM, :]
    for h in range(N_HEADS_A):
        qz_ref[:, h * BLOCK_Q:(h + 1) * BLOCK_Q] = qt_ref[h * HEAD_DIM_A:(h + 1) * HEAD_DIM_A, :]
    w_rows = wit_ref[...] * ((IDX_HEADS * IDX_DIM) ** -0.5)

    key_row = lax.broadcasted_iota(jnp.int32, tile, 0)
    q_pos = t0 + lax.broadcasted_iota(jnp.int32, tile, 1)

    last = n_chunks - 1
    n_pairs = (n_chunks + 1) // 2

    def idx_products(kc, slot):
        ks = pl.multiple_of(jnp.minimum(kc, last) * KEY_CHUNK, KEY_CHUNK)
        prod_ref[slot] = jnp.dot(kw_ref[pl.ds(ks, KEY_CHUNK), :], qiz_ref[...], preferred_element_type=F32)

    def reduce_scores(kc, slot, carry):
        rmax, rmin = carry
        kc = jnp.minimum(kc, last)
        s = jnp.zeros(tile, F32)
        for h in range(IDX_HEADS):
            d = prod_ref[slot, :, h * BLOCK_Q:(h + 1) * BLOCK_Q]
            s = s + jnp.maximum(d, 0.0) * w_rows[h:h + 1, :]
        causal = (kc * KEY_CHUNK + key_row) <= q_pos
        sc_ref[kc] = jnp.where(causal, s, -jnp.inf)
        rmax = jnp.maximum(rmax, jnp.max(jnp.where(causal, s, -jnp.inf), axis=0, keepdims=True))
        rmin = jnp.minimum(rmin, jnp.min(jnp.where(causal, s, jnp.inf), axis=0, keepdims=True))
        return rmax, rmin

    def run_pairs(start, stop, pair_body, carry):
        n_triple = (stop - start) // 3

        def triple(t, c):
            j = start + 3 * t
            return pair_body(j + 2, pair_body(j + 1, pair_body(j, c)))

        carry = lax.fori_loop(0, n_triple, triple, carry)
        return lax.fori_loop(start + 3 * n_triple, stop, pair_body, carry)

    def score_pair(j, carry, prefetch=True):
        idx_products(2 * j + 1, 1)
        carry = reduce_scores(2 * j, 0, carry)
        if prefetch:
            idx_products(2 * j + 2, 0)
        return reduce_scores(2 * j + 1, 1, carry)

    idx_products(0, 0)
    carry = lax.fori_loop(0, n_pairs - 1, score_pair,
                          (jnp.full(row1, -jnp.inf, F32), jnp.full(row1, jnp.inf, F32)))
    rmax, rmin = score_pair(n_pairs - 1, carry, prefetch=False)

    n_causal = (t0 + lax.broadcasted_iota(jnp.int32, row1, 1) + 1).astype(F32)
    short = n_causal <= k_sel

    n_steps = (n_chunks + COUNT_STEP - 1) // COUNT_STEP
    for d in range(COUNT_STEP - 1):
        @pl.when(n_chunks + d < n_steps * COUNT_STEP)
        def _(d=d):
            sc_ref[n_chunks + d] = jnp.full(tile, -jnp.inf, F32)

    def count_ge(t):
        def body(j, c):
            for d in range(COUNT_STEP):
                hit = jnp.where(sc_ref[COUNT_STEP * j + d] >= t, 1.0, 0.0)
                c = c + jnp.sum(hit.reshape(KEY_CHUNK // COUNT_ROWS, COUNT_ROWS, BLOCK_Q), axis=0)
            return c
        c = lax.fori_loop(0, n_steps, body, jnp.zeros((COUNT_ROWS, BLOCK_Q), F32))
        return jnp.sum(c, axis=0, keepdims=True)

    def halve(_, carry):
        lo, hi, c_lo, _ = carry
        mid = lo + (hi - lo) * 0.5
        cnt = count_ge(mid)
        enough = cnt >= k_sel
        shrinks = jnp.where((mid > lo) & (mid < hi), 1.0, 0.0)
        return jnp.where(enough, mid, lo), jnp.where(enough, hi, mid), jnp.where(enough, cnt, c_lo), shrinks

    def halvings(count, lo, hi, c_lo):
        lo, hi, c_lo, shrinks = lax.fori_loop(0, count, halve, (lo, hi, c_lo, jnp.ones(row1, F32)))
        settled = short | (c_lo == k_sel) | (shrinks == 0.0)
        return jnp.sum(jnp.where(settled, 0.0, 1.0)), lo, hi, c_lo

    def pending(carry):
        done, todo, _, _, _ = carry
        return (done < MAX_BISECT_ITERS) & (todo > 0.0)

    def bisect_round(carry):
        done, _, lo, hi, c_lo = carry
        return (done + BISECT_ROUND,) + halvings(BISECT_ROUND, lo, hi, c_lo)

    hi0 = rmax + jnp.maximum(jnp.abs(rmax) * 2.0 ** -22, jnp.finfo(F32).tiny)
    first = (jnp.int32(BISECT_FIRST),) + halvings(BISECT_FIRST, rmin, hi0, n_causal)
    _, _, lo, hi, c_lo = lax.while_loop(pending, bisect_round, first)
    thr = jnp.where(short, jnp.finfo(F32).min, lo)
    tied = jnp.logical_not(short) & (c_lo > k_sel)
    n_tied = jnp.sum(jnp.where(tied, 1.0, 0.0))

    m_ref[...] = jnp.full(m_ref.shape, MAX_INIT, F32)
    l_ref[...] = jnp.zeros(l_ref.shape, F32)
    acc_ref[...] = jnp.zeros(acc_ref.shape, F32)

    def qk_products(kc, slot):
        ks = pl.multiple_of(jnp.minimum(kc, last) * KEY_CHUNK, KEY_CHUNK)
        for g in range(N_KV_A):
            k_c = k_ref[pl.ds(ks, KEY_CHUNK), g * HEAD_DIM_A:(g + 1) * HEAD_DIM_A]
            prod_ref[slot, :, g * GRP_LANES:(g + 1) * GRP_LANES] = jnp.dot(
                k_c, qz_ref[:, g * GRP_LANES:(g + 1) * GRP_LANES], preferred_element_type=F32)

    def softmax_pv(kc, slot, near, ties, upper=None, quota=None):
        live = (kc <= last) if near else True
        kc = jnp.minimum(kc, last) if near else kc
        sc = sc_ref[kc]
        if ties:
            above = sc >= upper
            tie = (sc >= thr) & jnp.logical_not(above)
            tie_f = jnp.where(tie, 1.0, 0.0)
            upto = jnp.dot(tri_ref[...], tie_f.astype(BF16), preferred_element_type=F32)
            before = ties_ref[...]
            keep = (above | (tie & (before + upto - tie_f < quota))) & live
            ties_ref[...] = before + jnp.where(live, upto[KEY_CHUNK - 1:KEY_CHUNK], 0.0)
        else:
            keep = (sc >= thr) & live
        mask = jnp.where(keep, 0.0, -jnp.inf)
        mask = jnp.concatenate([mask] * GRP, axis=1)
        for g in range(N_KV_A):
            lg = prod_ref[slot, :, g * GRP_LANES:(g + 1) * GRP_LANES]
            if near:
                lg = lg + bias_ref[jnp.minimum(i - 2 * kc, N_BIAS_TILES - 1), g]
            lg = lg + mask
            m_old = m_ref[g]
            m_new = jnp.maximum(m_old, jnp.max(lg, axis=0, keepdims=True))
            alpha = jnp.exp2(m_old - m_new)
            p = jnp.exp2(lg - m_new)
            pv = jnp.dot(vt_ref[kc, g * V_ROWS:(g + 1) * V_ROWS, :], p.astype(BF16),
                         preferred_element_type=F32)
            l_ref[g] = alpha * l_ref[g] + pv[HEAD_DIM_A:HEAD_DIM_A + 1]
            acc_ref[g] = alpha * acc_ref[g] + pv[0:HEAD_DIM_A]
            m_ref[g] = m_new

    def attend(near, ties, prefetch=True, **tie_args):
        def body(j, carry):
            qk_products(2 * j + 1, 1)
            softmax_pv(2 * j, 0, near, ties, **tie_args)
            if prefetch:
                qk_products(2 * j + 2, 0)
            softmax_pv(2 * j + 1, 1, near, ties, **tie_args)
            return carry
        return body

    def attend_all(ties, **tie_args):
        run_pairs(0, n_far, attend(False, ties, **tie_args), 0)
        lax.fori_loop(n_far, n_pairs - 1, attend(True, ties, **tie_args), 0)
        attend(True, ties, prefetch=False, **tie_args)(n_pairs - 1, 0)

    n_far = jnp.maximum(n_pairs - 2, 0)
    qk_products(0, 0)

    @pl.when(n_tied == 0.0)
    def _():
        attend_all(False)

    @pl.when(n_tied > 0.0)
    def _():
        row = lax.broadcasted_iota(jnp.int32, (KEY_CHUNK, KEY_CHUNK), 0)
        col = lax.broadcasted_iota(jnp.int32, (KEY_CHUNK, KEY_CHUNK), 1)
        tri_ref[...] = (col <= row).astype(BF16)
        ties_ref[...] = jnp.zeros(ties_ref.shape, F32)
        attend_all(True, upper=jnp.where(tied, hi, thr), quota=jnp.where(tied, k_sel - count_ge(hi), 0.0))

    for g in range(N_KV_A):
        out_t = acc_ref[g] / l_ref[g]
        for j in range(GRP):
            h = g * GRP + j
            o_ref[:, h * HEAD_DIM_A:(h + 1) * HEAD_DIM_A] = (
                out_t[:, j * BLOCK_Q:(j + 1) * BLOCK_Q].T.astype(o_ref.dtype))


def _sparse_attention(bias_tiles, q_t, v_t, wi_t, k_n, batch, seq):
    nb = seq // BLOCK_Q
    n_seq_chunks = seq // KEY_CHUNK
    kw_blk = KV_WIDTH // V7X_LANES
    return pl.pallas_call(
        _sparse_attn_kernel,
        grid=(batch, nb),
        in_specs=[pl.BlockSpec(bias_tiles.shape, lambda b, i: (0, 0, 0, 0)),
                  pl.BlockSpec((ATTN_WIDTH, BLOCK_Q), lambda b, i: (0, b * nb + i)),
                  pl.BlockSpec((IDX_HEADS * IDX_DIM, BLOCK_Q), lambda b, i: (1, b * nb + i)),
                  pl.BlockSpec((IDX_HEADS, BLOCK_Q), lambda b, i: (0, b * nb + i)),
                  pl.BlockSpec((seq, V7X_LANES), lambda b, i: (b, kw_blk)),
                  pl.BlockSpec((seq, KV_WIDTH), lambda b, i: (b, 0)),
                  pl.BlockSpec((n_seq_chunks, N_KV_A * V_ROWS, KEY_CHUNK), lambda b, i: (b, 0, 0))],
        out_specs=pl.BlockSpec((BLOCK_Q, ATTN_WIDTH), lambda b, i: (b * nb + i, 0)),
        out_shape=jax.ShapeDtypeStruct((batch * seq, ATTN_WIDTH), BF16),
        scratch_shapes=[pltpu.VMEM((HEAD_DIM_A, N_HEADS_A * BLOCK_Q), BF16),
                        pltpu.VMEM((V7X_LANES, IDX_HEADS * BLOCK_Q), BF16),
                        pltpu.VMEM((n_seq_chunks, KEY_CHUNK, BLOCK_Q), F32),
                        pltpu.VMEM((2, KEY_CHUNK, IDX_HEADS * BLOCK_Q), F32),
                        pltpu.VMEM((N_KV_A, 1, GRP_LANES), F32),
                        pltpu.VMEM((N_KV_A, 1, GRP_LANES), F32),
                        pltpu.VMEM((N_KV_A, HEAD_DIM_A, GRP_LANES), F32),
                        pltpu.VMEM((KEY_CHUNK, KEY_CHUNK), BF16),
                        pltpu.VMEM((1, BLOCK_Q), F32)],
        compiler_params=_params(2),
        name="sparse_attention",
    )(bias_tiles, q_t, q_t, wi_t, k_n, k_n, v_t)


def _s5_input_kernel(u_ref, lr_ref, li_ref, ldt_ref, bre_ref, bim_ref,
                     bur_ref, bui_ref, ar_ref, ai_ref, bbr_ref, bbi_ref):
    @pl.when(pl.program_id(0) == 0)
    def _():
        dt = jnp.exp(ldt_ref[...])
        lr = lr_ref[...]
        li = li_ref[...]
        mag = jnp.exp(lr * dt)
        ar = mag * jnp.cos(li * dt)
        ai = mag * jnp.sin(li * dt)
        den = lr * lr + li * li
        nr = ar - 1.0
        cr = (nr * lr + ai * li) / den
        ci = (ai * lr - nr * li) / den
        bre = bre_ref[...]
        bim = bim_ref[...]
        bbr_ref[...] = (cr * bre - ci * bim).astype(BF16)
        bbi_ref[...] = (cr * bim + ci * bre).astype(BF16)
        ar_ref[...] = ar
        ai_ref[...] = ai

    u = u_ref[...].astype(BF16)
    bur_ref[...] = jnp.dot(u, bbr_ref[...], preferred_element_type=F32)
    bui_ref[...] = jnp.dot(u, bbi_ref[...], preferred_element_type=F32)


def _s5_input(us, lam_re, lam_im, log_dt, bre_bd, bim_bd):
    m = us.shape[0]
    tm = SMALL_ROW_TILE
    row = pl.BlockSpec((1, SSM_STATES), lambda i: (0, 0))
    mat = pl.BlockSpec((SSM_WIDTH, SSM_STATES), lambda i: (0, 0))
    out = pl.BlockSpec((tm, SSM_STATES), lambda i: (i, 0))
    return pl.pallas_call(
        _s5_input_kernel,
        grid=(m // tm,),
        in_specs=[pl.BlockSpec((tm, SSM_WIDTH), lambda i: (i, 0)), row, row, row, mat, mat],
        out_specs=[out, out, row, row],
        out_shape=[jax.ShapeDtypeStruct((m, SSM_STATES), F32)] * 2
        + [jax.ShapeDtypeStruct((1, SSM_STATES), F32)] * 2,
        scratch_shapes=[pltpu.VMEM((SSM_WIDTH, SSM_STATES), BF16)] * 2,
        compiler_params=_params(1),
        name="s5_input",
    )(us, lam_re, lam_im, log_dt, bre_bd, bim_bd)


def _s5_scan_kernel(ar_ref, ai_ref, br_ref, bi_ref, xr_ref, xi_ref, sr_ref, si_ref, stage_ref):
    @pl.when(pl.program_id(0) == 0)
    def _():
        sr_ref[...] = jnp.zeros(sr_ref.shape, F32)
        si_ref[...] = jnp.zeros(si_ref.shape, F32)

    ar = ar_ref[...]
    ai = ai_ref[...]
    batch, steps, rows, lanes = br_ref.shape

    def group(gi, carry):
        xr, xi = carry
        t0 = pl.multiple_of(gi * SCAN_GROUP, SCAN_GROUP)
        for s in range(SCAN_GROUP):
            nr = ar * xr - ai * xi + br_ref[:, t0 + s]
            ni = ar * xi + ai * xr + bi_ref[:, t0 + s]
            stage_ref[0, :, s * STAGE_PITCH:s * STAGE_PITCH + rows] = nr
            stage_ref[1, :, s * STAGE_PITCH:s * STAGE_PITCH + rows] = ni
            xr, xi = nr, ni
        for part, out_ref in ((0, xr_ref), (1, xi_ref)):
            for b in range(batch):
                for j in range(rows):
                    tile = stage_ref[part, b, pl.ds(j, SCAN_GROUP, stride=STAGE_PITCH), :]
                    out_ref[b, pl.ds(t0, SCAN_GROUP), j * lanes:(j + 1) * lanes] = tile.astype(out_ref.dtype)
        return xr, xi

    xr, xi = lax.fori_loop(0, steps // SCAN_GROUP, group, (sr_ref[...], si_ref[...]))
    sr_ref[...] = xr
    si_ref[...] = xi


def _s5_scan(a_re, a_im, bu_re, bu_im):
    batch, seq, rows, lanes = bu_re.shape
    blk = pl.BlockSpec((batch, SCAN_CHUNK, rows, lanes), lambda c: (0, c, 0, 0))
    out = pl.BlockSpec((batch, SCAN_CHUNK, rows * lanes), lambda c: (0, c, 0))
    par = pl.BlockSpec((rows, lanes), lambda c: (0, 0))
    return pl.pallas_call(
        _s5_scan_kernel,
        grid=(seq // SCAN_CHUNK,),
        in_specs=[par, par, blk, blk],
        out_specs=[out, out],
        out_shape=[jax.ShapeDtypeStruct((batch, seq, rows * lanes), BF16)] * 2,
        scratch_shapes=[pltpu.VMEM((batch, rows, lanes), F32)] * 2
        + [pltpu.VMEM((2, batch, SCAN_GROUP * STAGE_PITCH, lanes), F32)],
        compiler_params=_params(1),
        name="s5_scan",
    )(a_re, a_im, bu_re, bu_im)


def _s5_output_kernel(xr_ref, xi_ref, u_ref, cr_ref, ci_ref, d_ref, wg_ref, bg_ref, o_ref):
    y = (jnp.dot(xr_ref[...], cr_ref[...], preferred_element_type=F32)
         - jnp.dot(xi_ref[...], ci_ref[...], preferred_element_type=F32)
         + d_ref[...] * u_ref[...])
    g = jax.nn.gelu(y)
    gate = jax.nn.sigmoid(jnp.dot(g.astype(BF16), wg_ref[...], preferred_element_type=F32) + bg_ref[...])
    o_ref[...] = (g * gate).astype(o_ref.dtype)


def _s5_output(xr, xi, us, cre_bd, cim_bd, d_row, w_glu, b_glu):
    m = us.shape[0]
    tm = SMALL_ROW_TILE
    state = pl.BlockSpec((tm, SSM_STATES), lambda i: (i, 0))
    chan = pl.BlockSpec((tm, SSM_WIDTH), lambda i: (i, 0))
    cmat = pl.BlockSpec((SSM_STATES, SSM_WIDTH), lambda i: (0, 0))
    row = pl.BlockSpec((1, SSM_WIDTH), lambda i: (0, 0))
    return pl.pallas_call(
        _s5_output_kernel,
        grid=(m // tm,),
        in_specs=[state, state, chan, cmat, cmat, row,
                  pl.BlockSpec((SSM_WIDTH, SSM_WIDTH), lambda i: (0, 0)), row],
        out_specs=chan,
        out_shape=jax.ShapeDtypeStruct((m, SSM_WIDTH), BF16),
        compiler_params=_params(1),
        name="s5_output",
    )(xr, xi, us, cre_bd, cim_bd, d_row, w_glu, b_glu)


def _block_diag(blocks):
    g, r, c = blocks.shape
    eye = jnp.eye(g, dtype=blocks.dtype)
    return (eye[:, None, :, None] * blocks[:, :, None, :]).reshape(g * r, g * c)


def _post_mixer_kernel(x_ref, ya_ref, yb_ref, ga_ref, gb_ref, wa_ref, wb_ref, wo_ref, gc_ref, wq_ref,
                       kv_ref, wco_ref, o_ref):
    width = X_HEADS * X_HEAD_DIM
    scale = X_HEAD_DIM ** -0.5
    za = jnp.dot(ya_ref[...], wa_ref[...], preferred_element_type=F32)
    zb = jnp.dot(yb_ref[...], wb_ref[...], preferred_element_type=F32)
    z = (ga_ref[...].astype(F32) * za + gb_ref[...].astype(F32) * zb).astype(BF16)
    x1 = x_ref[...] + jnp.dot(z, wo_ref[...], preferred_element_type=F32)
    xn = _rms(x1, gc_ref[...], EPS).astype(BF16)
    q = jnp.dot(xn, wq_ref[...], preferred_element_type=F32).astype(BF16)
    heads = []
    for h in range(X_HEADS):
        cols = slice(h * X_HEAD_DIM, (h + 1) * X_HEAD_DIM)
        vcols = slice(width + h * X_HEAD_DIM, width + (h + 1) * X_HEAD_DIM)
        lg = lax.dot_general(q[:, cols], kv_ref[:, cols], _NT, preferred_element_type=F32) * scale
        p = jnp.exp(lg - jnp.max(lg, axis=-1, keepdims=True))
        o = jnp.dot(p.astype(BF16), kv_ref[:, vcols], preferred_element_type=F32)
        heads.append((o / jnp.sum(p, axis=-1, keepdims=True)).astype(BF16))
    attn = jnp.concatenate(heads, axis=1)
    o_ref[...] = x1 + jnp.dot(attn, wco_ref[...], preferred_element_type=F32)


def _post_mixer(x, ya, yb, gates, w_a, w_b, w_out, gain_cross, w_q, kv, w_co, batch, seq, mem_len):
    d = x.shape[1]
    width = X_HEADS * X_HEAD_DIM
    tm = SMALL_ROW_TILE
    nt = seq // tm
    row = pl.BlockSpec((tm, d), lambda b, i: (b * nt + i, 0))
    return pl.pallas_call(
        _post_mixer_kernel,
        grid=(batch, nt),
        in_specs=[row,
                  pl.BlockSpec((tm, ya.shape[1]), lambda b, i: (b * nt + i, 0)),
                  pl.BlockSpec((tm, yb.shape[1]), lambda b, i: (b * nt + i, 0)),
                  row,
                  pl.BlockSpec((tm, d), lambda b, i: (b * nt + i, 1)),
                  pl.BlockSpec(w_a.shape, lambda b, i: (0, 0)),
                  pl.BlockSpec(w_b.shape, lambda b, i: (0, 0)),
                  pl.BlockSpec((d, d), lambda b, i: (0, 0)),
                  pl.BlockSpec((1, d), lambda b, i: (0, 0)),
                  pl.BlockSpec((d, width), lambda b, i: (0, 0)),
                  pl.BlockSpec((mem_len, 2 * width), lambda b, i: (b, 0)),
                  pl.BlockSpec((width, d), lambda b, i: (0, 0))],
        out_specs=row,
        out_shape=jax.ShapeDtypeStruct(x.shape, F32),
        compiler_params=_params(2),
        name="post_mixer",
    )(x, ya, yb, gates, gates, w_a, w_b, w_out, gain_cross.reshape(1, d), w_q, kv, w_co)


def _conv_ffn_kernel(x_ref, g_ref, wa_ref, wb_ref, cwa_ref, cwb_ref, cba_ref, cbb_ref, wd_ref, gf_ref,
                     o_ref, xn_ref, ha_ref, hb_ref, ta_ref, tb_ref, *, tiles_per_seq):
    i = pl.program_id(0)
    j = pl.program_id(1)
    tm = x_ref.shape[0]
    halo = V7X_SUBLANES

    @pl.when(j == 0)
    def _():
        xn_ref[...] = _rms(x_ref[...], g_ref[...], EPS).astype(BF16)
        o_ref[...] = jnp.zeros(o_ref.shape, F32)

    @pl.when(i % tiles_per_seq == 0)
    def _():
        ha_ref[0:halo] = jnp.zeros((halo, ha_ref.shape[1]), F32)
        hb_ref[0:halo] = jnp.zeros((halo, hb_ref.shape[1]), F32)

    @pl.when(i % tiles_per_seq != 0)
    def _():
        ha_ref[0:halo] = ta_ref[j]
        hb_ref[0:halo] = tb_ref[j]

    rows = tm // FFN_ROW_SLABS
    for s in range(FFN_ROW_SLABS):
        xn = xn_ref[s * rows:(s + 1) * rows]
        ha_ref[halo + s * rows:halo + (s + 1) * rows] = jnp.dot(xn, wa_ref[...], preferred_element_type=F32)
        hb_ref[halo + s * rows:halo + (s + 1) * rows] = jnp.dot(xn, wb_ref[...], preferred_element_type=F32)
    ta_ref[j] = ha_ref[tm:tm + halo]
    tb_ref[j] = hb_ref[tm:tm + halo]

    def conv(h_ref, cw_ref, cb_ref, r0):
        out = cb_ref[...]
        for tap in range(CONV_WIDTH):
            start = r0 + halo - (CONV_WIDTH - 1) + tap
            out = out + h_ref[start:start + rows] * cw_ref[tap:tap + 1]
        return out

    for s in range(FFN_ROW_SLABS):
        a = conv(ha_ref, cwa_ref, cba_ref, s * rows)
        b = conv(hb_ref, cwb_ref, cbb_ref, s * rows)
        act = (a * jax.nn.sigmoid(a) * b).astype(BF16)
        o_ref[s * rows:(s + 1) * rows] += jnp.dot(act, wd_ref[...], preferred_element_type=F32)

    @pl.when(j == pl.num_programs(1) - 1)
    def _():
        o_ref[...] = _rms(x_ref[...] + o_ref[...], gf_ref[...], EPS)


def _conv_ffn(x, gain, w_up, conv_w, conv_b, w_down, gain_final, seq):
    m, d = x.shape
    d_ff = w_down.shape[0]
    tm, tf = FFN_ROW_TILE, COL_TILE
    nj = d_ff // tf
    halo = V7X_SUBLANES
    return pl.pallas_call(
        functools.partial(_conv_ffn_kernel, tiles_per_seq=seq // tm),
        grid=(m // tm, nj),
        in_specs=[pl.BlockSpec((tm, d), lambda i, j: (i, 0)),
                  pl.BlockSpec((1, d), lambda i, j: (0, 0)),
                  pl.BlockSpec((d, tf), lambda i, j: (0, j)),
                  pl.BlockSpec((d, tf), lambda i, j: (0, j + nj)),
                  pl.BlockSpec((CONV_WIDTH, tf), lambda i, j: (0, j)),
                  pl.BlockSpec((CONV_WIDTH, tf), lambda i, j: (0, j + nj)),
                  pl.BlockSpec((1, tf), lambda i, j: (0, j)),
                  pl.BlockSpec((1, tf), lambda i, j: (0, j + nj)),
                  pl.BlockSpec((tf, d), lambda i, j: (j, 0)),
                  pl.BlockSpec((1, d), lambda i, j: (0, 0))],
        out_specs=pl.BlockSpec((tm, d), lambda i, j: (i, 0)),
        out_shape=jax.ShapeDtypeStruct((m, d), F32),
        scratch_shapes=[pltpu.VMEM((tm, d), BF16),
                        pltpu.VMEM((tm + halo, tf), F32),
                        pltpu.VMEM((tm + halo, tf), F32),
                        pltpu.VMEM((nj, halo, tf), F32),
                        pltpu.VMEM((nj, halo, tf), F32)],
        compiler_params=_params(2),
        name="conv_ffn",
    )(x, gain.reshape(1, d), w_up, w_up, conv_w, conv_w, conv_b.reshape(1, -1), conv_b.reshape(1, -1),
      w_down, gain_final.reshape(1, d))


def kernel(x, mem, rel_bias, norm_mix, w_in, ssm_lambda_re, ssm_lambda_im, ssm_log_dt, ssm_b_re, ssm_b_im, ssm_c_re, ssm_c_im, ssm_d, ssm_w_glu, ssm_b_glu, w_branch_a, w_branch_b, w_out, norm_cross, norm_mem, w_cross_q, w_cross_kv, w_cross_o, norm_ffn, w_up, ffn_conv_w, ffn_conv_b, w_down, norm_final):
    batch, seq, d_model = x.shape
    mem_len = mem.shape[1]
    depth = w_in.shape[0]
    assert depth == 1, "the final rmsnorm is fused into the last layer's ConvFFN kernel"
    assert seq % ROW_TILE == 0 and seq % (COUNT_STEP * KEY_CHUNK) == 0 and seq >= 4 * TOPK_MAX
    m = batch * seq
    xf = x.reshape(m, d_model)
    memf = mem.reshape(batch * mem_len, d_model)
    bias_tiles = _bias_tiles(rel_bias)

    splits = (ATTN_WIDTH, KV_WIDTH, KV_WIDTH, IDX_HEADS * IDX_DIM, IDX_DIM, IDX_HEADS, SSM_WIDTH, d_model, d_model)
    offs = [int(o) for o in np.cumsum((0,) + splits)]
    for l in range(depth):
        w = w_in[l]
        wq, wk, wv, wqi, wki, wwi, wss, wga, wgb = [w[:, offs[n]:offs[n + 1]] for n in range(len(splits))]
        pad = jnp.zeros((d_model, V7X_LANES - IDX_DIM - IDX_HEADS), w.dtype)
        w_t = jnp.concatenate([wq, wqi, wv, wwi], axis=1).T.astype(BF16)
        w_n = jnp.concatenate([wk, wki, wwi, pad, wss], axis=1).astype(BF16)
        w_gate = jnp.concatenate([wga, wgb], axis=1).astype(BF16)

        q_t, v_t, wi_t, k_n, us = _proj_attn(xf, norm_mix[l], w_t, w_n)
        gates = _norm_matmul(xf, norm_mix[l], w_gate, BF16, GATE_COL_TILE, "proj_gates", sigmoid=True)

        y_a = _sparse_attention(bias_tiles, q_t, v_t, wi_t, k_n, batch, seq)

        bre_bd = _block_diag(jnp.transpose(ssm_b_re[l], (0, 2, 1)))
        bim_bd = _block_diag(jnp.transpose(ssm_b_im[l], (0, 2, 1)))
        cre_bd = _block_diag(jnp.transpose(ssm_c_re[l], (0, 2, 1))).astype(BF16)
        cim_bd = _block_diag(jnp.transpose(ssm_c_im[l], (0, 2, 1))).astype(BF16)
        log_dt = jnp.broadcast_to(ssm_log_dt[l][:, None], (SSM_GROUPS, SSM_STATE)).reshape(1, SSM_STATES)
        bu_re, bu_im, a_re, a_im = _s5_input(us, ssm_lambda_re[l].reshape(1, SSM_STATES),
                                             ssm_lambda_im[l].reshape(1, SSM_STATES), log_dt, bre_bd, bim_bd)
        state_rows = SSM_STATES // V7X_LANES
        sshape = (batch, seq, state_rows, V7X_LANES)
        xs_re, xs_im = _s5_scan(a_re.reshape(state_rows, V7X_LANES), a_im.reshape(state_rows, V7X_LANES),
                                bu_re.reshape(sshape), bu_im.reshape(sshape))
        y_b = _s5_output(xs_re.reshape(m, SSM_STATES), xs_im.reshape(m, SSM_STATES), us, cre_bd, cim_bd,
                         ssm_d[l].reshape(1, SSM_WIDTH), ssm_w_glu[l].astype(BF16),
                         ssm_b_glu[l].reshape(1, SSM_WIDTH))

        kvx = _norm_matmul(memf, norm_mem[l], w_cross_kv[l].astype(BF16), BF16, COL_TILE, "cross_kv")
        xf = _post_mixer(xf, y_a, y_b, gates, w_branch_a[l].astype(BF16), w_branch_b[l].astype(BF16),
                         w_out[l].astype(BF16), norm_cross[l], w_cross_q[l].astype(BF16), kvx,
                         w_cross_o[l].astype(BF16), batch, seq, mem_len)

        xf = _conv_ffn(xf, norm_ffn[l], w_up[l].astype(BF16), ffn_conv_w[l], ffn_conv_b[l],
                       w_down[l].astype(BF16), norm_final, seq)
    return xf.reshape(batch, seq, d_model)
```
